```python
import math
import jax, jax.numpy as jnp
from jax import lax
import numpy as np

D_MODEL = 1024
BATCH = 16
SEQ = 2048
DEPTH = 4

N_MIXERS = 3
PLE_DIM = 256
D_FF = 4 * D_MODEL
EPS = 1e-6
NEG_INF = -1e30
S5_GROUP = 16
S5_GROUPS = D_MODEL // S5_GROUP
S5_STATE = 64
S5_CHUNK = 128
FOX_HEADS = 16
FOX_HEAD_DIM = D_MODEL // FOX_HEADS
FOX_IN = 3 * D_MODEL + FOX_HEADS
Q_BLOCK = 128
GLA_HEADS = 4
GLA_KD = D_MODEL // 2
GLA_VD = D_MODEL
GLA_DK = GLA_KD // GLA_HEADS
GLA_DV = GLA_VD // GLA_HEADS
GLA_RANK = 16
GLA_GATE_NORM = 16.0
GLA_CHUNK = 64
GLA_IN = 2 * GLA_KD + 2 * GLA_VD + GLA_RANK
N_A = (DEPTH + 2) // 3
N_B = (DEPTH + 1) // 3
N_C = DEPTH // 3

kernel_name = "hybrid_s5_fox_gla_trunk"


def rmsnorm(x, g):
    xf = x.astype(jnp.float32)
    y = xf * lax.rsqrt(jnp.mean(xf * xf, axis=-1, keepdims=True) + EPS)
    return (y * g.astype(jnp.float32)).astype(x.dtype)


def _cplx_combine(e1, e2):
    a1r, a1i, b1r, b1i = e1
    a2r, a2i, b2r, b2i = e2
    ar = a2r * a1r - a2i * a1i
    ai = a2r * a1i + a2i * a1r
    br = a2r * b1r - a2i * b1i + b2r
    bi = a2r * b1i + a2i * b1r + b2i
    return ar, ai, br, bi


def s5_mixer(xn, w_in, lam_re, lam_im, log_dt, b_re, b_im, c_re, c_im, d_skip, w_glu, w_out):
    bsz, t, _ = xn.shape
    f32 = jnp.float32
    n_chunks = t // S5_CHUNK
    u = (xn @ w_in).astype(f32)
    dt = jnp.exp(log_dt.astype(f32))[:, None]
    lr, li = lam_re.astype(f32), lam_im.astype(f32)
    mag = jnp.exp(lr * dt)
    ab_re, ab_im = mag * jnp.cos(li * dt), mag * jnp.sin(li * dt)
    den = lr * lr + li * li
    nr = ab_re - 1.0
    coef_re = (nr * lr + ab_im * li) / den
    coef_im = (ab_im * lr - nr * li) / den
    br, bi = b_re.astype(f32), b_im.astype(f32)
    bbar_re = coef_re[..., None] * br - coef_im[..., None] * bi
    bbar_im = coef_re[..., None] * bi + coef_im[..., None] * br
    cr, ci = c_re.astype(f32), c_im.astype(f32)
    shape = (bsz, S5_CHUNK, S5_GROUPS, S5_STATE)
    a_re = jnp.broadcast_to(ab_re, shape)
    a_im = jnp.broadcast_to(ab_im, shape)
    uc = u.reshape(bsz, n_chunks, S5_CHUNK, S5_GROUPS, S5_GROUP).swapaxes(0, 1)

    def chunk_step(carry, u_blk):
        h_re, h_im = carry
        x_re = jnp.einsum('bsgc,gnc->bsgn', u_blk, bbar_re)
        x_im = jnp.einsum('bsgc,gnc->bsgn', u_blk, bbar_im)
        pa_re, pa_im, s_re, s_im = lax.associative_scan(
            _cplx_combine, (a_re, a_im, x_re, x_im), axis=1)
        s_re = s_re + pa_re * h_re[:, None] - pa_im * h_im[:, None]
        s_im = s_im + pa_re * h_im[:, None] + pa_im * h_re[:, None]
        y = jnp.einsum('bsgn,gcn->bsgc', s_re, cr) - jnp.einsum('bsgn,gcn->bsgc', s_im, ci)
        return (s_re[:, -1], s_im[:, -1]), y

    h0 = (jnp.zeros((bsz, S5_GROUPS, S5_STATE), f32), jnp.zeros((bsz, S5_GROUPS, S5_STATE), f32))
    _, y = lax.scan(chunk_step, h0, uc)
    y = y.swapaxes(0, 1).reshape(bsz, t, D_MODEL) + d_skip.astype(f32) * u
    z = jax.nn.gelu(y).astype(xn.dtype) @ w_glu
    z1, z2 = jnp.split(z, 2, axis=-1)
    return (z1 * jax.nn.sigmoid(z2)) @ w_out


def fox_mixer(xn, w_in, b_f, w_out):
    bsz, t, _ = xn.shape
    n_blocks = t // Q_BLOCK
    proj = xn @ w_in
    q, k, v = (proj[..., i * D_MODEL:(i + 1) * D_MODEL]
               .reshape(bsz, t, FOX_HEADS, FOX_HEAD_DIM).transpose(0, 2, 1, 3) for i in range(3))
    log_f = jax.nn.log_sigmoid((proj[..., 3 * D_MODEL:] + b_f).astype(jnp.float32))
    cum_f = jnp.cumsum(log_f, axis=1).transpose(0, 2, 1)
    scale = FOX_HEAD_DIM ** -0.5
    qb = q.reshape(bsz, FOX_HEADS, n_blocks, Q_BLOCK, FOX_HEAD_DIM).transpose(2, 0, 1, 3, 4)
    fb = cum_f.reshape(bsz, FOX_HEADS, n_blocks, Q_BLOCK).transpose(2, 0, 1, 3)
    q_pos = jnp.arange(t).reshape(n_blocks, Q_BLOCK)
    k_pos = jnp.arange(t)

    def block(args):
        q_blk, f_blk, pos = args
        s = jnp.einsum('bhqd,bhkd->bhqk', q_blk, k).astype(jnp.float32) * scale
        s = s + (f_blk[..., :, None] - cum_f[:, :, None, :])
        s = jnp.where(pos[:, None] >= k_pos[None, :], s, NEG_INF)
        pr = jax.nn.softmax(s, axis=-1)
        return jnp.einsum('bhqk,bhkd->bhqd', pr.astype(v.dtype), v)

    o = lax.map(block, (qb, fb, q_pos))
    o = o.transpose(1, 0, 3, 2, 4).reshape(bsz, t, D_MODEL)
    return o @ w_out


def gla_mixer(xn, w_in, w_g2, b_g, gn_g, w_out):
    bsz, t, _ = xn.shape
    f32 = jnp.float32
    n_chunks = t // GLA_CHUNK
    proj = xn @ w_in
    q = proj[..., :GLA_KD]
    k = proj[..., GLA_KD:2 * GLA_KD]
    v = proj[..., 2 * GLA_KD:2 * GLA_KD + GLA_VD]
    r = proj[..., 2 * GLA_KD + GLA_VD:2 * GLA_KD + 2 * GLA_VD]
    g_lr = proj[..., 2 * GLA_KD + 2 * GLA_VD:]
    log_a = jax.nn.log_sigmoid((g_lr @ w_g2 + b_g).astype(f32)) / GLA_GATE_NORM

    def to_chunks(z, d):
        return z.reshape(bsz, n_chunks, GLA_CHUNK, GLA_HEADS, d).transpose(1, 0, 3, 2, 4)

    qc = to_chunks(q.astype(f32) * GLA_DK ** -0.5, GLA_DK)
    kc = to_chunks(k.astype(f32), GLA_DK)
    vc = to_chunks(v.astype(f32), GLA_DV)
    gc = to_chunks(log_a, GLA_DK)
    causal = jnp.tril(jnp.ones((GLA_CHUNK, GLA_CHUNK), dtype=bool))

    def chunk_step(state, xs):
        q_blk, k_blk, v_blk, g_blk = xs
        bcum = jnp.cumsum(g_blk, axis=2)
        b_last = bcum[:, :, -1:]
        q_dec = q_blk * jnp.exp(bcum)
        k_inv = k_blk * jnp.exp(-bcum)
        k_upd = k_blk * jnp.exp(b_last - bcum)
        att = jnp.where(causal, jnp.einsum('bhtk,bhsk->bhts', q_dec, k_inv), 0.0)
        o = jnp.einsum('bhts,bhsv->bhtv', att, v_blk) + jnp.einsum('bhtk,bhkv->bhtv', q_dec, state)
        state = jnp.exp(b_last).swapaxes(-1, -2) * state + jnp.einsum('bhsk,bhsv->bhkv', k_upd, v_blk)
        return state, o

    s0 = jnp.zeros((bsz, GLA_HEADS, GLA_DK, GLA_DV), f32)
    _, o = lax.scan(chunk_step, s0, (qc, kc, vc, gc))
    o = o.transpose(1, 0, 3, 2, 4).reshape(bsz, t, GLA_HEADS, GLA_DV)
    o = o * lax.rsqrt(jnp.mean(o * o, axis=-1, keepdims=True) + EPS)
    o = o.reshape(bsz, t, GLA_VD) * gn_g.astype(f32)
    o = (o * jax.nn.silu(r.astype(f32))).astype(xn.dtype)
    return o @ w_out


def sqrelu_mlp(xn, w1, w2):
    return jnp.square(jax.nn.relu(xn @ w1)) @ w2


def setup_inputs(seed: int = 0) -> dict:
    key = jax.random.key(seed)
    ks = iter(jax.random.split(key, 40))
    f32 = jnp.float32

    def nrm(shape, scale):
        return jax.random.normal(next(ks), shape, f32) * scale

    def gain(shape):
        return 1.0 + nrm(shape, 0.02)

    D = D_MODEL
    inputs = {}
    inputs['x'] = nrm((BATCH, SEQ, D), 1.0)
    inputs['p'] = nrm((DEPTH, BATCH, SEQ, PLE_DIM), 1.0)
    inputs['norm_mix'] = gain((DEPTH, D))
    inputs['norm_mlp'] = gain((DEPTH, D))
    inputs['norm_ple'] = gain((DEPTH, D))
    inputs['s5_w_in'] = nrm((N_A, D, D), D ** -0.5)
    inputs['s5_lam_re'] = -0.5 + nrm((N_A, S5_GROUPS, S5_STATE), 0.01)
    inputs['s5_lam_im'] = (math.pi * jnp.arange(S5_STATE, dtype=f32))[None, None, :] + nrm((N_A, S5_GROUPS, S5_STATE), 0.01)
    inputs['s5_log_dt'] = jax.random.uniform(next(ks), (N_A, S5_GROUPS), f32, minval=math.log(1e-3), maxval=math.log(1e-1))
    inputs['s5_b_re'] = nrm((N_A, S5_GROUPS, S5_STATE, S5_GROUP), (2 * S5_GROUP) ** -0.5)
    inputs['s5_b_im'] = nrm((N_A, S5_GROUPS, S5_STATE, S5_GROUP), (2 * S5_GROUP) ** -0.5)
    inputs['s5_c_re'] = nrm((N_A, S5_GROUPS, S5_GROUP, S5_STATE), (S5_STATE) ** -0.5)
    inputs['s5_c_im'] = nrm((N_A, S5_GROUPS, S5_GROUP, S5_STATE), (S5_STATE) ** -0.5)
    inputs['s5_d'] = nrm((N_A, D), 1.0)
    inputs['s5_w_glu'] = nrm((N_A, D, 2 * D), D ** -0.5)
    inputs['s5_w_out'] = nrm((N_A, D, D), D ** -0.5)
    inputs['fox_w_in'] = nrm((N_B, D, FOX_IN), D ** -0.5)
    inputs['fox_b_f'] = jax.random.uniform(next(ks), (N_B, FOX_HEADS), f32, minval=1.0, maxval=5.0)
    inputs['fox_w_out'] = nrm((N_B, D, D), D ** -0.5)
    inputs['gla_w_in'] = nrm((N_C, D, GLA_IN), D ** -0.5)
    inputs['gla_w_g2'] = nrm((N_C, GLA_RANK, GLA_KD), GLA_RANK ** -0.5)
    inputs['gla_b_g'] = nrm((N_C, GLA_KD), 0.1)
    inputs['gla_norm'] = gain((N_C, GLA_VD))
    inputs['gla_w_out'] = nrm((N_C, GLA_VD, D), GLA_VD ** -0.5)
    inputs['mlp_w1'] = nrm((DEPTH, D, D_FF), D ** -0.5)
    inputs['mlp_w2'] = nrm((DEPTH, D_FF, D), D_FF ** -0.5)
    inputs['ple_proj'] = nrm((DEPTH, PLE_DIM, D), PLE_DIM ** -0.5)
    inputs['ple_gate'] = nrm((DEPTH, D, D), D ** -0.5)
    inputs['final_norm'] = gain((D,))
    return inputs


def reference(x, p, norm_mix, norm_mlp, norm_ple,
              s5_w_in, s5_lam_re, s5_lam_im, s5_log_dt, s5_b_re, s5_b_im, s5_c_re, s5_c_im,
              s5_d, s5_w_glu, s5_w_out,
              fox_w_in, fox_b_f, fox_w_out,
              gla_w_in, gla_w_g2, gla_b_g, gla_norm, gla_w_out,
              mlp_w1, mlp_w2, ple_proj, ple_gate, final_norm):
    h = x
    for i in range(DEPTH):
        mixer, j = i % N_MIXERS, i // N_MIXERS
        xn = rmsnorm(h, norm_mix[i])
        if mixer == 0:
            y = s5_mixer(xn, s5_w_in[j], s5_lam_re[j], s5_lam_im[j], s5_log_dt[j], s5_b_re[j], s5_b_im[j],
                         s5_c_re[j], s5_c_im[j], s5_d[j], s5_w_glu[j], s5_w_out[j])
        elif mixer == 1:
            y = fox_mixer(xn, fox_w_in[j], fox_b_f[j], fox_w_out[j])
        else:
            y = gla_mixer(xn, gla_w_in[j], gla_w_g2[j], gla_b_g[j], gla_norm[j], gla_w_out[j])
        h = h + y.astype(h.dtype)
        h = h + sqrelu_mlp(rmsnorm(h, norm_mlp[i]), mlp_w1[i], mlp_w2[i])
        gate = jax.nn.sigmoid(rmsnorm(h, norm_ple[i]) @ ple_gate[i])
        h = h + (p[i] @ ple_proj[i]) * gate
    return rmsnorm(h, final_norm)
```

```python
import functools
import math

import jax
import jax.numpy as jnp
from jax import lax
from jax.experimental import pallas as pl
from jax.experimental.pallas import tpu as pltpu

F32 = jnp.float32
BF16 = jnp.bfloat16

D_MODEL = 1024
D_FF = 4 * D_MODEL
PLE_DIM = 256
EPS = 1e-6
NEG_INF = -1e30
LOG2E = 1.4426950408889634

V7X_VMEM_BYTES = 64 * 1024 * 1024
VMEM_LIMIT_BYTES = V7X_VMEM_BYTES - 8 * 1024 * 1024
LANES = 128

S5_GROUP = 16
S5_GROUPS = D_MODEL // S5_GROUP
S5_STATE = 64
S5_L = 16
S5_K = S5_L * S5_GROUP

FOX_HEADS = 16
FOX_HEAD_DIM = D_MODEL // FOX_HEADS
FOX_TQ = 512

GLA_HEADS = 4
GLA_KD = D_MODEL // 2
GLA_VD = D_MODEL
GLA_DK = GLA_KD // GLA_HEADS
GLA_DV = GLA_VD // GLA_HEADS
GLA_RANK = 16
GLA_GATE_NORM = 16.0
GLA_CHUNK = 64

ROW_TILE = 512


def _params(*sem):
    return pltpu.CompilerParams(dimension_semantics=sem, vmem_limit_bytes=VMEM_LIMIT_BYTES)


def _resident(shape):
    zeros = (0,) * len(shape)
    return pl.BlockSpec(shape, lambda *_: zeros, pipeline_mode=pl.Buffered(1))


def _rms(x, g):
    ms = jnp.mean(x * x, axis=-1, keepdims=True)
    return x * lax.rsqrt(ms + EPS) * g


def _log_sigmoid(x):
    return jnp.minimum(x, 0.0) - jnp.log1p(jnp.exp(-jnp.abs(x)))


def _dot(a, b):
    return jnp.dot(a, b, preferred_element_type=F32)


def _dot_nt(a, b):
    return lax.dot_general(a, b, (((1,), (1,)), ((), ())), preferred_element_type=F32)


def _dot_tn(a, b):
    return lax.dot_general(a, b, (((0,), (0,)), ((), ())), preferred_element_type=F32)


def _split3(x):
    hi = x.astype(BF16)
    r1 = x - hi.astype(F32)
    mid = r1.astype(BF16)
    lo = (r1 - mid.astype(F32)).astype(BF16)
    return hi, mid, lo


def _tail_kernel(h_ref, mix_ref, wo_ref, gm_ref, w1_ref, w2_ref, gp_ref, wpg_ref, p_ref,
                 wpp_ref, gf_ref, o_ref, *, final):
    h = h_ref[...] + _dot(mix_ref[...], wo_ref[...])
    xn = _rms(h, gm_ref[...]).astype(BF16)
    ff_chunk = D_FF // 4
    mlp = None
    for c in range(4):
        a = _dot(xn, w1_ref[:, c * ff_chunk:(c + 1) * ff_chunk])
        a = jnp.square(jnp.maximum(a, 0.0)).astype(BF16)
        part = _dot(a, w2_ref[c * ff_chunk:(c + 1) * ff_chunk, :])
        mlp = part if mlp is None else mlp + part
    h = h + mlp
    xg = _rms(h, gp_ref[...]).astype(BF16)
    gate = jax.nn.sigmoid(_dot(xg, wpg_ref[...]))
    pe = _dot(p_ref[...].astype(BF16), wpp_ref[...])
    h = h + pe * gate
    if final:
        h = _rms(h, gf_ref[...])
    o_ref[...] = h


def _tail(h, mix, wo, gm, w1, w2, gp, wpg, p_all, layer, wpp, gf, final):
    m = h.shape[0]
    tm = ROW_TILE
    row = lambda i: (i, 0)
    return pl.pallas_call(
        functools.partial(_tail_kernel, final=final),
        grid=(m // tm,),
        in_specs=[
            pl.BlockSpec((tm, D_MODEL), row),
            pl.BlockSpec((tm, mix.shape[1]), row),
            _resident(wo.shape),
            _resident((1, D_MODEL)),
            _resident(w1.shape),
            _resident(w2.shape),
            _resident((1, D_MODEL)),
            _resident(wpg.shape),
            pl.BlockSpec((None, tm, PLE_DIM), lambda i: (layer, i, 0)),
            _resident(wpp.shape),
            _resident((1, D_MODEL)),
        ],
        out_specs=pl.BlockSpec((tm, D_MODEL), row),
        out_shape=jax.ShapeDtypeStruct((m, D_MODEL), F32),
        compiler_params=_params("parallel"),
    )(h, mix, wo, gm, w1, w2, gp, wpg, p_all, wpp, gf)


def _cmul(ar, ai, br, bi):
    return ar * br - ai * bi, ar * bi + ai * br


def _cpow(ar, ai, n, shape):
    pr = jnp.ones(shape, F32)
    pi = jnp.zeros(shape, F32)
    br, bi = ar, ai
    for bit in range(5):
        tr, ti = _cmul(pr, pi, br, bi)
        take = ((n >> bit) & 1) == 1
        pr = jnp.where(take, tr, pr)
        pi = jnp.where(take, ti, pi)
        br, bi = _cmul(br, bi, br, bi)
    return pr, pi


def _s5_discretize(lr, li, ld):
    dt = jnp.exp(ld)
    mag = jnp.exp(lr * dt)
    ar = mag * jnp.cos(li * dt)
    ai = mag * jnp.sin(li * dt)
    den = lr * lr + li * li
    nr = ar - 1.0
    coef_re = (nr * lr + ai * li) / den
    coef_im = (ai * lr - nr * li) / den
    return ar, ai, coef_re, coef_im


def _s5_prep_kernel(lrc_ref, lic_ref, lrr_ref, lir_ref, ld_ref, brt_ref, bit_ref, cr_ref, ci_ref,
                    crt_ref, cit_ref, d_ref, kall_ref, e_ref, f_ref, al_ref):
    n, k = S5_STATE, S5_K
    ld = ld_ref[...]
    ar, ai, cfr, cfi = _s5_discretize(lrc_ref[...], lic_ref[...], ld)
    bbr = cfr * brt_ref[...] - cfi * bit_ref[...]
    bbi = cfr * bit_ref[...] + cfi * brt_ref[...]
    lag = lax.broadcasted_iota(jnp.int32, (n, k), 1) >> 4
    pr, pi = _cpow(ar, ai, lag, (n, k))
    wr, wi = _cmul(pr, pi, bbr, bbi)
    hp = lax.Precision.HIGHEST
    kall = (jnp.dot(cr_ref[...], wr, precision=hp, preferred_element_type=F32)
            - jnp.dot(ci_ref[...], wi, precision=hp, preferred_element_type=F32))
    row = lax.broadcasted_iota(jnp.int32, (S5_GROUP, k), 0)
    col = lax.broadcasted_iota(jnp.int32, (S5_GROUP, k), 1)
    kall_ref[...] = kall + jnp.where(row == col, d_ref[...], 0.0)
    pr, pi = _cpow(ar, ai, (S5_L - 1) - lag, (n, k))
    er, ei = _cmul(pr, pi, bbr, bbi)
    e_ref[0:n, :] = er
    e_ref[n:2 * n, :] = ei
    a2r, a2i = _cmul(ar, ai, ar, ai)
    a4r, a4i = _cmul(a2r, a2i, a2r, a2i)
    a8r, a8i = _cmul(a4r, a4i, a4r, a4i)
    a16r, a16i = _cmul(a8r, a8i, a8r, a8i)
    al_ref[0] = a16r
    al_ref[1] = a16i
    ar, ai, _, _ = _s5_discretize(lrr_ref[...], lir_ref[...], ld)
    step = (lax.broadcasted_iota(jnp.int32, (k, n), 0) >> 4) + 1
    pr, pi = _cpow(ar, ai, step, (k, n))
    f_ref[:, 0:n] = crt_ref[...] * pr - cit_ref[...] * pi
    f_ref[:, n:2 * n] = -(crt_ref[...] * pi + cit_ref[...] * pr)


def _s5_prep(lam_re, lam_im, log_dt, b_re, b_im, c_re, c_im, d_skip):
    g, n, k = S5_GROUPS, S5_STATE, S5_K
    per_g = lambda *shape: pl.BlockSpec((None,) + shape, lambda i: (i,) + (0,) * len(shape))
    kall, e, f, al = pl.pallas_call(
        _s5_prep_kernel,
        grid=(g,),
        in_specs=[per_g(n, 1), per_g(n, 1), per_g(1, n), per_g(1, n), per_g(1, 1),
                  per_g(n, k), per_g(n, k), per_g(S5_GROUP, n), per_g(S5_GROUP, n),
                  per_g(k, n), per_g(k, n), per_g(S5_GROUP, 1)],
        out_specs=[per_g(S5_GROUP, k), per_g(2 * n, k), per_g(k, 2 * n), per_g(2, n, 1)],
        out_shape=[jax.ShapeDtypeStruct((g, S5_GROUP, k), F32),
                   jax.ShapeDtypeStruct((g, 2 * n, k), F32),
                   jax.ShapeDtypeStruct((g, k, 2 * n), F32),
                   jax.ShapeDtypeStruct((g, 2, n, 1), F32)],
        compiler_params=_params("parallel"),
    )(lam_re.reshape(g, n, 1), lam_im.reshape(g, n, 1), lam_re.reshape(g, 1, n),
      lam_im.reshape(g, 1, n), log_dt.reshape(g, 1, 1),
      jnp.tile(b_re, (1, 1, S5_L)), jnp.tile(b_im, (1, 1, S5_L)), c_re, c_im,
      jnp.tile(c_re, (1, S5_L, 1)), jnp.tile(c_im, (1, S5_L, 1)),
      d_skip.reshape(g, S5_GROUP, 1))
    kk = kall.reshape(g, S5_GROUP, S5_L, S5_GROUP)
    t = jnp.arange(S5_L)
    lagm = t[:, None] - t[None, :]
    m = jnp.take(kk, jnp.clip(lagm, 0, S5_L - 1).reshape(-1), axis=2)
    m = m.reshape(g, S5_GROUP, S5_L, S5_L, S5_GROUP)
    m = jnp.where((lagm >= 0)[None, None, :, :, None], m, 0.0)
    m = m.transpose(0, 2, 1, 3, 4).reshape(g, k, k)
    return m.astype(BF16), e.astype(BF16), f.astype(BF16), al


def _s5_in_kernel(h_ref, g_ref, w_ref, ut_ref, u_s, *, seq):
    for r in range(seq // ROW_TILE):
        rows = slice(r * ROW_TILE, (r + 1) * ROW_TILE)
        xn = _rms(h_ref[rows, :], g_ref[...]).astype(BF16)
        u = _dot(xn, w_ref[...])
        for lt in range(D_MODEL // LANES):
            u_s[lt, rows, :] = u[:, lt * LANES:(lt + 1) * LANES]
    nck = seq // S5_L
    gpl = LANES // S5_GROUP
    for j in range(S5_L):
        for lt in range(D_MODEL // LANES):
            xj = u_s[lt, pl.ds(j, nck, stride=S5_L), :]
            ut_ref[gpl * lt:gpl * (lt + 1), S5_GROUP * j:S5_GROUP * (j + 1), :] = (
                xj.T.reshape(gpl, S5_GROUP, nck).astype(BF16))


def _s5_core_kernel(ut_ref, m_ref, e_ref, f_ref, al_ref, yt_ref, *, nck):
    n = S5_STATE
    u = ut_ref[...]
    y = _dot(m_ref[...], u)
    s = _dot(e_ref[...], u)
    sr, si = s[0:n, :], s[n:2 * n, :]
    ar, ai = al_ref[0], al_ref[1]
    chunk = lax.broadcasted_iota(jnp.int32, sr.shape, 1) & (nck - 1)
    shift = 1
    while shift < nck:
        pr = pltpu.roll(sr, shift, axis=1)
        pi = pltpu.roll(si, shift, axis=1)
        tr, ti = _cmul(ar, ai, pr, pi)
        keep = chunk >= shift
        sr = sr + jnp.where(keep, tr, 0.0)
        si = si + jnp.where(keep, ti, 0.0)
        ar, ai = _cmul(ar, ai, ar, ai)
        shift *= 2
    first = chunk == 0
    sr = jnp.where(first, 0.0, pltpu.roll(sr, 1, axis=1))
    si = jnp.where(first, 0.0, pltpu.roll(si, 1, axis=1))
    sp = jnp.concatenate([sr, si], axis=0).astype(BF16)
    yt_ref[...] = y + _dot(f_ref[...], sp)


def _s5_post_kernel(yt_ref, wg_ref, mix_ref, y_s, *, seq):
    nck = seq // S5_L
    gpl = LANES // S5_GROUP
    nlt = D_MODEL // LANES
    for t in range(S5_L):
        for lt in range(nlt):
            piece = yt_ref[gpl * lt:gpl * (lt + 1), S5_GROUP * t:S5_GROUP * (t + 1), :]
            y_s[lt, pl.ds(t, nck, stride=S5_L), :] = piece.reshape(LANES, nck).T
    for r in range(seq // ROW_TILE):
        rows = slice(r * ROW_TILE, (r + 1) * ROW_TILE)
        y = jnp.concatenate([y_s[lt, rows, :] for lt in range(nlt)], axis=1)
        a = jax.nn.gelu(y).astype(BF16)
        z = _dot(a, wg_ref[...])
        mix_ref[rows, :] = (z[:, :D_MODEL] * jax.nn.sigmoid(z[:, D_MODEL:])).astype(BF16)


def _s5_mixer(h, gm, w_in, mats, w_glu, batch, seq):
    m_t, e_t, f_t, al = mats
    nck = seq // S5_L
    lanes = batch * nck
    assert nck % LANES == 0 and nck & (nck - 1) == 0
    ut = pl.pallas_call(
        functools.partial(_s5_in_kernel, seq=seq),
        grid=(batch,),
        in_specs=[pl.BlockSpec((seq, D_MODEL), lambda b: (b, 0)),
                  _resident((1, D_MODEL)), _resident(w_in.shape)],
        out_specs=pl.BlockSpec((S5_GROUPS, S5_K, nck), lambda b: (0, 0, b)),
        out_shape=jax.ShapeDtypeStruct((S5_GROUPS, S5_K, lanes), BF16),
        scratch_shapes=[pltpu.VMEM((D_MODEL // LANES, seq, LANES), F32)],
        compiler_params=_params("parallel"),
    )(h, gm, w_in)
    per_g = lambda *shape: pl.BlockSpec((None,) + shape, lambda g: (g,) + (0,) * len(shape))
    yt = pl.pallas_call(
        functools.partial(_s5_core_kernel, nck=nck),
        grid=(S5_GROUPS,),
        in_specs=[per_g(S5_K, lanes), per_g(S5_K, S5_K), per_g(2 * S5_STATE, S5_K),
                  per_g(S5_K, 2 * S5_STATE), per_g(2, S5_STATE, 1)],
        out_specs=per_g(S5_K, lanes),
        out_shape=jax.ShapeDtypeStruct((S5_GROUPS, S5_K, lanes), F32),
        compiler_params=_params("parallel"),
    )(ut, m_t, e_t, f_t, al)
    return pl.pallas_call(
        functools.partial(_s5_post_kernel, seq=seq),
        grid=(batch,),
        in_specs=[pl.BlockSpec((S5_GROUPS, S5_K, nck), lambda b: (0, 0, b)),
                  _resident(w_glu.shape)],
        out_specs=pl.BlockSpec((seq, D_MODEL), lambda b: (b, 0)),
        out_shape=jax.ShapeDtypeStruct((batch * seq, D_MODEL), BF16),
        scratch_shapes=[pltpu.VMEM((D_MODEL // LANES, seq, LANES), F32)],
        compiler_params=_params("parallel"),
    )(yt, w_glu)


def _fox_in_kernel(h_ref, g_ref, wqkv_ref, wf_ref, bf_ref, q_ref, k_ref, v_ref, carry_s):
    tm = ROW_TILE

    @pl.when(pl.program_id(1) == 0)
    def _():
        carry_s[...] = jnp.zeros_like(carry_s)

    xn = _rms(h_ref[...], g_ref[...]).astype(BF16)
    proj = _dot(xn, wqkv_ref[...])
    log_f = _log_sigmoid(_dot(xn, wf_ref[...]) + bf_ref[...])
    row = lax.broadcasted_iota(jnp.int32, (tm, tm), 0)
    col = lax.broadcasted_iota(jnp.int32, (tm, tm), 1)
    tri = (row >= col).astype(BF16)
    hi, mid, lo = _split3(log_f)
    cum = _dot(tri, hi) + _dot(tri, mid) + _dot(tri, lo) + carry_s[...]
    carry_s[...] = cum[tm - 1:tm, :]
    hi, mid, lo = _split3(cum * LOG2E)
    hi, mid, lo = hi.astype(F32), mid.astype(F32), lo.astype(F32)
    lane = lax.broadcasted_iota(jnp.int32, (tm, LANES), 1)
    qscale = FOX_HEAD_DIM ** -0.5 * LOG2E
    for pair in range(FOX_HEADS // 2):
        qp = proj[:, pair * LANES:(pair + 1) * LANES] * qscale
        kp = proj[:, D_MODEL + pair * LANES:D_MODEL + (pair + 1) * LANES]
        for half in range(2):
            head = 2 * pair + half
            own = (lane < FOX_HEAD_DIM) if half == 0 else (lane >= FOX_HEAD_DIM)
            base = FOX_HEAD_DIM if half == 0 else 0
            c0, c1, c2 = hi[:, head:head + 1], mid[:, head:head + 1], lo[:, head:head + 1]
            ones = (lane >= base) & (lane < base + 3)
            ones_q = (lane >= base + 3) & (lane < base + 6)
            cq = jnp.where(lane == base, c0, jnp.where(lane == base + 1, c1,
                 jnp.where(lane == base + 2, c2, jnp.where(ones_q, 1.0, 0.0))))
            ck = jnp.where(lane == base + 3, -c0, jnp.where(lane == base + 4, -c1,
                 jnp.where(lane == base + 5, -c2, jnp.where(ones, 1.0, 0.0))))
            q_ref[head] = jnp.where(own, qp, cq).astype(BF16)
            k_ref[head] = jnp.where(own, kp, ck).astype(BF16)
    v_ref[...] = proj[:, 2 * D_MODEL:].astype(BF16)


def _fox_flash_kernel(q_ref, k_ref, v_ref, o_ref):
    tq = FOX_TQ
    qi = pl.program_id(2)
    row = lax.broadcasted_iota(jnp.int32, (tq, tq), 0)
    col = lax.broadcasted_iota(jnp.int32, (tq, tq), 1)
    causal = row >= col
    outs = []
    for half in range(2):
        q = q_ref[half]

        def step(j, carry, masked):
            m, l, acc = carry
            kb = k_ref[half, pl.ds(pl.multiple_of(j * tq, tq), tq), :]
            vb = v_ref[pl.ds(pl.multiple_of(j * tq, tq), tq), :]
            s = _dot_nt(q, kb)
            if masked:
                s = jnp.where(causal, s, NEG_INF)
            m_new = jnp.maximum(m, jnp.max(s, axis=-1, keepdims=True))
            alpha = jnp.exp2(m - m_new)
            p = jnp.exp2(s - m_new)
            l = alpha * l + jnp.sum(p, axis=-1, keepdims=True)
            acc = alpha * acc + _dot(p.astype(BF16), vb)
            return m_new, l, acc

        init = (jnp.full((tq, 1), NEG_INF, F32), jnp.zeros((tq, 1), F32),
                jnp.zeros((tq, LANES), F32))
        carry = lax.fori_loop(0, qi, functools.partial(step, masked=False), init)
        m, l, acc = step(qi, carry, masked=True)
        outs.append(acc / l)
    lane = lax.broadcasted_iota(jnp.int32, (tq, LANES), 1)
    o_ref[...] = jnp.where(lane < FOX_HEAD_DIM, outs[0], outs[1]).astype(BF16)


def _fox_mixer(h, gm, wqkv, wf, bf, batch, seq):
    tm = ROW_TILE
    nt = seq // tm
    q, k, v = pl.pallas_call(
        _fox_in_kernel,
        grid=(batch, nt),
        in_specs=[pl.BlockSpec((tm, D_MODEL), lambda b, i: (b * nt + i, 0)),
                  _resident((1, D_MODEL)), _resident(wqkv.shape), _resident(wf.shape),
                  _resident((1, LANES))],
        out_specs=[pl.BlockSpec((None, FOX_HEADS, tm, LANES), lambda b, i: (b, 0, i, 0)),
                   pl.BlockSpec((None, FOX_HEADS, tm, LANES), lambda b, i: (b, 0, i, 0)),
                   pl.BlockSpec((tm, D_MODEL), lambda b, i: (b * nt + i, 0))],
        out_shape=[jax.ShapeDtypeStruct((batch, FOX_HEADS, seq, LANES), BF16),
                   jax.ShapeDtypeStruct((batch, FOX_HEADS, seq, LANES), BF16),
                   jax.ShapeDtypeStruct((batch * seq, D_MODEL), BF16)],
        scratch_shapes=[pltpu.VMEM((1, LANES), F32)],
        compiler_params=_params("parallel", "arbitrary"),
    )(h, gm, wqkv, wf, bf)
    tq = FOX_TQ
    nq = seq // tq
    return pl.pallas_call(
        _fox_flash_kernel,
        grid=(batch, FOX_HEADS // 2, nq),
        in_specs=[pl.BlockSpec((None, 2, tq, LANES), lambda b, p, i: (b, p, i, 0)),
                  pl.BlockSpec((None, 2, seq, LANES), lambda b, p, i: (b, p, 0, 0)),
                  pl.BlockSpec((seq, LANES), lambda b, p, i: (b, p))],
        out_specs=pl.BlockSpec((tq, LANES), lambda b, p, i: (b * nq + i, p)),
        out_shape=jax.ShapeDtypeStruct((batch * seq, D_MODEL), BF16),
        compiler_params=_params("parallel", "parallel", "arbitrary"),
    )(q, k, v)


def _gla_in_kernel(h_ref, g_ref, w_ref, wg2_ref, bg_ref, qd_ref, ki_ref, ku_ref, v_ref, rs_ref,
                   dec_ref):
    tm = ROW_TILE
    xn = _rms(h_ref[...], g_ref[...]).astype(BF16)
    proj = _dot(xn, w_ref[...])
    q = proj[:, 0:GLA_KD]
    k = proj[:, GLA_KD:2 * GLA_KD]
    v = proj[:, 2 * GLA_KD:2 * GLA_KD + GLA_VD]
    r = proj[:, 2 * GLA_KD + GLA_VD:2 * GLA_KD + 2 * GLA_VD]
    g_lr = proj[:, 2 * GLA_KD + 2 * GLA_VD:].astype(BF16)
    log_a = _log_sigmoid(_dot(g_lr, wg2_ref[...]) + bg_ref[...]) / GLA_GATE_NORM
    row = lax.broadcasted_iota(jnp.int32, (tm, tm), 0)
    col = lax.broadcasted_iota(jnp.int32, (tm, tm), 1)
    same = (row >> 6) == (col >> 6)
    tri = (same & (row >= col)).astype(BF16)
    hi, mid, lo = _split3(log_a)
    bcum = _dot(tri, hi) + _dot(tri, mid) + _dot(tri, lo)
    nch = tm // GLA_CHUNK
    crow = lax.broadcasted_iota(jnp.int32, (nch, tm), 0)
    ccol = lax.broadcasted_iota(jnp.int32, (nch, tm), 1)
    pick = (crow == (ccol >> 6)).astype(BF16)
    b_last = _dot(pick, hi) + _dot(pick, mid) + _dot(pick, lo)
    dec_ref[...] = jnp.exp(b_last)
    spread = (same & ((col & (GLA_CHUNK - 1)) == GLA_CHUNK - 1)).astype(BF16)
    h2, m2, l2 = _split3(bcum)
    b_last_rows = _dot(spread, h2) + _dot(spread, m2) + _dot(spread, l2)
    qd_ref[...] = (q * GLA_DK ** -0.5 * jnp.exp(bcum)).astype(BF16)
    ki_ref[...] = (k * jnp.exp(-bcum)).astype(BF16)
    ku_ref[...] = (k * jnp.exp(b_last_rows - bcum)).astype(BF16)
    v_ref[...] = v.astype(BF16)
    rs_ref[...] = (r * jax.nn.sigmoid(r)).astype(BF16)


def _gla_core_kernel(qd_ref, ki_ref, ku_ref, v_ref, rs_ref, dec_ref, gn_ref, o_ref, *, seq):
    c = GLA_CHUNK
    row = lax.broadcasted_iota(jnp.int32, (c, c), 0)
    col = lax.broadcasted_iota(jnp.int32, (c, c), 1)
    causal = row >= col
    gn = gn_ref[...]

    def chunk(i, st):
        rows = pl.ds(pl.multiple_of(i * c, c), c)
        qd, ki, ku, v = qd_ref[rows, :], ki_ref[rows, :], ku_ref[rows, :], v_ref[rows, :]
        att = jnp.where(causal, _dot_nt(qd, ki), 0.0)
        o = _dot(att.astype(BF16), v) + _dot_nt(qd, st.astype(BF16))
        o = o * lax.rsqrt(jnp.mean(o * o, axis=-1, keepdims=True) + EPS)
        o_ref[rows, :] = (o * gn * rs_ref[rows, :].astype(F32)).astype(BF16)
        return st * dec_ref[pl.ds(i, 1), :] + _dot_tn(v, ku)

    lax.fori_loop(0, seq // c, chunk, jnp.zeros((GLA_DV, GLA_DK), F32))


def _gla_mixer(h, gm, w_cat, wg2, bg, gn, batch, seq):
    tm = ROW_TILE
    m = batch * seq
    nch = tm // GLA_CHUNK
    row = lambda i: (i, 0)
    tok = lambda width, dtype: jax.ShapeDtypeStruct((m, width), dtype)
    qd, ki, ku, v, rs, dec = pl.pallas_call(
        _gla_in_kernel,
        grid=(m // tm,),
        in_specs=[pl.BlockSpec((tm, D_MODEL), row), _resident((1, D_MODEL)),
                  _resident(w_cat.shape), _resident(wg2.shape), _resident((1, GLA_KD))],
        out_specs=[pl.BlockSpec((tm, GLA_KD), row), pl.BlockSpec((tm, GLA_KD), row),
                   pl.BlockSpec((tm, GLA_KD), row), pl.BlockSpec((tm, GLA_VD), row),
                   pl.BlockSpec((tm, GLA_VD), row), pl.BlockSpec((nch, GLA_KD), row)],
        out_shape=[tok(GLA_KD, BF16), tok(GLA_KD, BF16), tok(GLA_KD, BF16), tok(GLA_VD, BF16),
                   tok(GLA_VD, BF16), jax.ShapeDtypeStruct((m // GLA_CHUNK, GLA_KD), F32)],
        compiler_params=_params("parallel"),
    )(h, gm, w_cat, wg2, bg)
    blk = lambda rows, width: pl.BlockSpec((rows, width), lambda b, hd: (b, hd))
    return pl.pallas_call(
        functools.partial(_gla_core_kernel, seq=seq),
        grid=(batch, GLA_HEADS),
        in_specs=[blk(seq, GLA_DK), blk(seq, GLA_DK), blk(seq, GLA_DK), blk(seq, GLA_DV),
                  blk(seq, GLA_DV), blk(seq // GLA_CHUNK, GLA_DK),
                  pl.BlockSpec((1, GLA_DV), lambda b, hd: (0, hd))],
        out_specs=blk(seq, GLA_DV),
        out_shape=jax.ShapeDtypeStruct((m, GLA_VD), BF16),
        compiler_params=_params("parallel", "parallel"),
    )(qd, ki, ku, v, rs, dec, gn)


def _pad_lanes(w, width):
    return jnp.pad(w, ((0, 0), (0, width - w.shape[1])))


def kernel(x, p, norm_mix, norm_mlp, norm_ple, s5_w_in, s5_lam_re, s5_lam_im, s5_log_dt, s5_b_re,
           s5_b_im, s5_c_re, s5_c_im, s5_d, s5_w_glu, s5_w_out, fox_w_in, fox_b_f, fox_w_out,
           gla_w_in, gla_w_g2, gla_b_g, gla_norm, gla_w_out, mlp_w1, mlp_w2, ple_proj, ple_gate,
           final_norm):
    batch, seq, _ = x.shape
    depth = p.shape[0]
    m = batch * seq
    h = x.reshape(m, D_MODEL)
    p_all = p.reshape(depth, m, PLE_DIM)
    gf = final_norm.reshape(1, D_MODEL)
    for i in range(depth):
        mixer, j = i % 3, i // 3
        gm = norm_mix[i].reshape(1, D_MODEL)
        if mixer == 0:
            mats = _s5_prep(s5_lam_re[j], s5_lam_im[j], s5_log_dt[j], s5_b_re[j], s5_b_im[j],
                            s5_c_re[j], s5_c_im[j], s5_d[j])
            mix = _s5_mixer(h, gm, s5_w_in[j].astype(BF16), mats, s5_w_glu[j].astype(BF16),
                            batch, seq)
            wo = s5_w_out[j]
        elif mixer == 1:
            w = fox_w_in[j]
            wqkv = w[:, :3 * D_MODEL].astype(BF16)
            wf = _pad_lanes(w[:, 3 * D_MODEL:], LANES).astype(BF16)
            bf = _pad_lanes(fox_b_f[j].reshape(1, FOX_HEADS), LANES)
            mix = _fox_mixer(h, gm, wqkv, wf, bf, batch, seq)
            wo = fox_w_out[j]
        else:
            w_cat = _pad_lanes(gla_w_in[j], 2 * GLA_KD + 2 * GLA_VD + LANES).astype(BF16)
            wg2 = jnp.pad(gla_w_g2[j], ((0, LANES - GLA_RANK), (0, 0))).astype(BF16)
            mix = _gla_mixer(h, gm, w_cat, wg2, gla_b_g[j].reshape(1, GLA_KD),
                             gla_norm[j].reshape(1, GLA_VD), batch, seq)
            wo = gla_w_out[j]
        h = _tail(h, mix, wo.astype(BF16), norm_mlp[i].reshape(1, D_MODEL),
                  mlp_w1[i].astype(BF16), mlp_w2[i].astype(BF16),
                  norm_ple[i].reshape(1, D_MODEL), ple_gate[i].astype(BF16), p_all, i,
                  ple_proj[i].astype(BF16), gf, final=(i == depth - 1))
    return h.reshape(batch, seq, D_MODEL)
```

```python
import functools
import math

import jax
import jax.numpy as jnp
from jax import lax
from jax.experimental import pallas as pl
from jax.experimental.pallas import tpu as pltpu

F32 = jnp.float32
BF16 = jnp.bfloat16

D_MODEL = 1024
D_FF = 4 * D_MODEL
PLE_DIM = 256
EPS = 1e-6
NEG_INF = -1e30
LOG2E = 1.4426950408889634

V7X_VMEM_BYTES = 64 * 1024 * 1024
VMEM_LIMIT_BYTES = V7X_VMEM_BYTES - 8 * 1024 * 1024
LANES = 128

S5_GROUP = 16
S5_GROUPS = D_MODEL // S5_GROUP
S5_STATE = 64
S5_L = 16
S5_K = S5_L * S5_GROUP

FOX_HEADS = 16
FOX_HEAD_DIM = D_MODEL // FOX_HEADS
FOX_TQ = 512
FOX_BIAS_GROUPS = 6

GLA_HEADS = 4
GLA_KD = D_MODEL // 2
GLA_VD = D_MODEL
GLA_DK = GLA_KD // GLA_HEADS
GLA_DV = GLA_VD // GLA_HEADS
GLA_RANK = 16
GLA_GATE_NORM = 16.0
GLA_CHUNK = 64

ROW_TILE = 512


def _params(*sem):
    return pltpu.CompilerParams(dimension_semantics=sem, vmem_limit_bytes=VMEM_LIMIT_BYTES)


def _resident(shape):
    zeros = (0,) * len(shape)
    return pl.BlockSpec(shape, lambda *_: zeros, pipeline_mode=pl.Buffered(1))


def _rms(x, g):
    ms = jnp.mean(x * x, axis=-1, keepdims=True)
    return x * lax.rsqrt(ms + EPS) * g


def _log_sigmoid(x):
    return jnp.minimum(x, 0.0) - jnp.log1p(jnp.exp(-jnp.abs(x)))


def _dot(a, b):
    return jnp.dot(a, b, preferred_element_type=F32)


def _dot_nt(a, b):
    return lax.dot_general(a, b, (((1,), (1,)), ((), ())), preferred_element_type=F32)


def _dot_tn(a, b):
    return lax.dot_general(a, b, (((0,), (0,)), ((), ())), preferred_element_type=F32)


def _split3(x):
    hi = x.astype(BF16)
    r1 = x - hi.astype(F32)
    mid = r1.astype(BF16)
    lo = (r1 - mid.astype(F32)).astype(BF16)
    return hi, mid, lo


def _tail_kernel(h_ref, mix_ref, wo_ref, gm_ref, w1_ref, w2_ref, gp_ref, wpg_ref, p_ref,
                 wpp_ref, gf_ref, o_ref, *, final):
    h = h_ref[...] + _dot(mix_ref[...], wo_ref[...])
    xn = _rms(h, gm_ref[...]).astype(BF16)
    ff_chunk = D_FF // 4
    mlp = None
    for c in range(4):
        a = _dot(xn, w1_ref[:, c * ff_chunk:(c + 1) * ff_chunk])
        a = jnp.square(jnp.maximum(a, 0.0)).astype(BF16)
        part = _dot(a, w2_ref[c * ff_chunk:(c + 1) * ff_chunk, :])
        mlp = part if mlp is None else mlp + part
    h = h + mlp
    xg = _rms(h, gp_ref[...]).astype(BF16)
    gate = jax.nn.sigmoid(_dot(xg, wpg_ref[...]))
    pe = _dot(p_ref[...].astype(BF16), wpp_ref[...])
    h = h + pe * gate
    if final:
        h = _rms(h, gf_ref[...])
    o_ref[...] = h


def _tail(h, mix, wo, gm, w1, w2, gp, wpg, p_all, layer, wpp, gf, final):
    m = h.shape[0]
    tm = ROW_TILE
    row = lambda i: (i, 0)
    return pl.pallas_call(
        functools.partial(_tail_kernel, final=final),
        grid=(m // tm,),
        in_specs=[
            pl.BlockSpec((tm, D_MODEL), row),
            pl.BlockSpec((tm, mix.shape[1]), row),
            _resident(wo.shape),
            _resident((1, D_MODEL)),
            _resident(w1.shape),
            _resident(w2.shape),
            _resident((1, D_MODEL)),
            _resident(wpg.shape),
            pl.BlockSpec((None, tm, PLE_DIM), lambda i: (layer, i, 0)),
            _resident(wpp.shape),
            _resident((1, D_MODEL)),
        ],
        out_specs=pl.BlockSpec((tm, D_MODEL), row),
        out_shape=jax.ShapeDtypeStruct((m, D_MODEL), F32),
        compiler_params=_params("parallel"),
    )(h, mix, wo, gm, w1, w2, gp, wpg, p_all, wpp, gf)


def _cmul(ar, ai, br, bi):
    return ar * br - ai * bi, ar * bi + ai * br


def _cpow(ar, ai, n, shape):
    pr = jnp.ones(shape, F32)
    pi = jnp.zeros(shape, F32)
    br, bi = ar, ai
    for bit in range(5):
        tr, ti = _cmul(pr, pi, br, bi)
        take = ((n >> bit) & 1) == 1
        pr = jnp.where(take, tr, pr)
        pi = jnp.where(take, ti, pi)
        br, bi = _cmul(br, bi, br, bi)
    return pr, pi


def _s5_discretize(lr, li, ld):
    dt = jnp.exp(ld)
    mag = jnp.exp(lr * dt)
    ar = mag * jnp.cos(li * dt)
    ai = mag * jnp.sin(li * dt)
    den = lr * lr + li * li
    nr = ar - 1.0
    coef_re = (nr * lr + ai * li) / den
    coef_im = (ai * lr - nr * li) / den
    return ar, ai, coef_re, coef_im


def _s5_prep_kernel(lrc_ref, lic_ref, lrr_ref, lir_ref, ld_ref, brt_ref, bit_ref, cr_ref, ci_ref,
                    crt_ref, cit_ref, d_ref, kall_ref, e_ref, f_ref, al_ref):
    n, k = S5_STATE, S5_K
    ld = ld_ref[...]
    ar, ai, cfr, cfi = _s5_discretize(lrc_ref[...], lic_ref[...], ld)
    bbr = cfr * brt_ref[...] - cfi * bit_ref[...]
    bbi = cfr * bit_ref[...] + cfi * brt_ref[...]
    lag = lax.broadcasted_iota(jnp.int32, (n, k), 1) >> 4
    pr, pi = _cpow(ar, ai, lag, (n, k))
    wr, wi = _cmul(pr, pi, bbr, bbi)
    hp = lax.Precision.HIGHEST
    kall = (jnp.dot(cr_ref[...], wr, precision=hp, preferred_element_type=F32)
            - jnp.dot(ci_ref[...], wi, precision=hp, preferred_element_type=F32))
    row = lax.broadcasted_iota(jnp.int32, (S5_GROUP, k), 0)
    col = lax.broadcasted_iota(jnp.int32, (S5_GROUP, k), 1)
    kall_ref[...] = kall + jnp.where(row == col, d_ref[...], 0.0)
    pr, pi = _cpow(ar, ai, (S5_L - 1) - lag, (n, k))
    er, ei = _cmul(pr, pi, bbr, bbi)
    e_ref[0:n, :] = er
    e_ref[n:2 * n, :] = ei
    a2r, a2i = _cmul(ar, ai, ar, ai)
    a4r, a4i = _cmul(a2r, a2i, a2r, a2i)
    a8r, a8i = _cmul(a4r, a4i, a4r, a4i)
    a16r, a16i = _cmul(a8r, a8i, a8r, a8i)
    al_ref[0] = a16r
    al_ref[1] = a16i
    ar, ai, _, _ = _s5_discretize(lrr_ref[...], lir_ref[...], ld)
    step = (lax.broadcasted_iota(jnp.int32, (k, n), 0) >> 4) + 1
    pr, pi = _cpow(ar, ai, step, (k, n))
    f_ref[:, 0:n] = crt_ref[...] * pr - cit_ref[...] * pi
    f_ref[:, n:2 * n] = -(crt_ref[...] * pi + cit_ref[...] * pr)


def _s5_prep(lam_re, lam_im, log_dt, b_re, b_im, c_re, c_im, d_skip):
    g, n, k = S5_GROUPS, S5_STATE, S5_K
    per_g = lambda *shape: pl.BlockSpec((None,) + shape, lambda i: (i,) + (0,) * len(shape))
    kall, e, f, al = pl.pallas_call(
        _s5_prep_kernel,
        grid=(g,),
        in_specs=[per_g(n, 1), per_g(n, 1), per_g(1, n), per_g(1, n), per_g(1, 1),
                  per_g(n, k), per_g(n, k), per_g(S5_GROUP, n), per_g(S5_GROUP, n),
                  per_g(k, n), per_g(k, n), per_g(S5_GROUP, 1)],
        out_specs=[per_g(S5_GROUP, k), per_g(2 * n, k), per_g(k, 2 * n), per_g(2, n, 1)],
        out_shape=[jax.ShapeDtypeStruct((g, S5_GROUP, k), F32),
                   jax.ShapeDtypeStruct((g, 2 * n, k), F32),
                   jax.ShapeDtypeStruct((g, k, 2 * n), F32),
                   jax.ShapeDtypeStruct((g, 2, n, 1), F32)],
        compiler_params=_params("parallel"),
    )(lam_re.reshape(g, n, 1), lam_im.reshape(g, n, 1), lam_re.reshape(g, 1, n),
      lam_im.reshape(g, 1, n), log_dt.reshape(g, 1, 1),
      jnp.tile(b_re, (1, 1, S5_L)), jnp.tile(b_im, (1, 1, S5_L)), c_re, c_im,
      jnp.tile(c_re, (1, S5_L, 1)), jnp.tile(c_im, (1, S5_L, 1)),
      d_skip.reshape(g, S5_GROUP, 1))
    kk = kall.reshape(g, S5_GROUP, S5_L, S5_GROUP)
    t = jnp.arange(S5_L)
    lagm = t[:, None] - t[None, :]
    m = jnp.take(kk, jnp.clip(lagm, 0, S5_L - 1).reshape(-1), axis=2)
    m = m.reshape(g, S5_GROUP, S5_L, S5_L, S5_GROUP)
    m = jnp.where((lagm >= 0)[None, None, :, :, None], m, 0.0)
    m = m.transpose(0, 2, 1, 3, 4).reshape(g, k, k)
    return m.astype(BF16), e.astype(BF16), f.astype(BF16), al


def _s5_in_kernel(h_ref, g_ref, w_ref, ut_ref, u_s, *, seq):
    for r in range(seq // ROW_TILE):
        rows = slice(r * ROW_TILE, (r + 1) * ROW_TILE)
        xn = _rms(h_ref[rows, :], g_ref[...]).astype(BF16)
        u = _dot(xn, w_ref[...])
        for lt in range(D_MODEL // LANES):
            u_s[lt, rows, :] = u[:, lt * LANES:(lt + 1) * LANES]
    nck = seq // S5_L
    gpl = LANES // S5_GROUP
    for j in range(S5_L):
        for lt in range(D_MODEL // LANES):
            xj = u_s[lt, pl.ds(j, nck, stride=S5_L), :]
            ut_ref[gpl * lt:gpl * (lt + 1), S5_GROUP * j:S5_GROUP * (j + 1), :] = (
                xj.T.reshape(gpl, S5_GROUP, nck).astype(BF16))


def _s5_core_kernel(ut_ref, m_ref, e_ref, f_ref, al_ref, yt_ref, *, nck):
    n = S5_STATE
    u = ut_ref[...]
    y = _dot(m_ref[...], u)
    s = _dot(e_ref[...], u)
    sr, si = s[0:n, :], s[n:2 * n, :]
    ar, ai = al_ref[0], al_ref[1]
    chunk = lax.broadcasted_iota(jnp.int32, sr.shape, 1) & (nck - 1)
    shift = 1
    while shift < nck:
        pr = pltpu.roll(sr, shift, axis=1)
        pi = pltpu.roll(si, shift, axis=1)
        tr, ti = _cmul(ar, ai, pr, pi)
        keep = chunk >= shift
        sr = sr + jnp.where(keep, tr, 0.0)
        si = si + jnp.where(keep, ti, 0.0)
        ar, ai = _cmul(ar, ai, ar, ai)
        shift *= 2
    first = chunk == 0
    sr = jnp.where(first, 0.0, pltpu.roll(sr, 1, axis=1))
    si = jnp.where(first, 0.0, pltpu.roll(si, 1, axis=1))
    sp = jnp.concatenate([sr, si], axis=0).astype(BF16)
    yt_ref[...] = y + _dot(f_ref[...], sp)


def _s5_post_kernel(yt_ref, wg_ref, mix_ref, y_s, *, seq):
    nck = seq // S5_L
    gpl = LANES // S5_GROUP
    nlt = D_MODEL // LANES
    for t in range(S5_L):
        for lt in range(nlt):
            piece = yt_ref[gpl * lt:gpl * (lt + 1), S5_GROUP * t:S5_GROUP * (t + 1), :]
            y_s[lt, pl.ds(t, nck, stride=S5_L), :] = piece.reshape(LANES, nck).T
    for r in range(seq // ROW_TILE):
        rows = slice(r * ROW_TILE, (r + 1) * ROW_TILE)
        y = jnp.concatenate([y_s[lt, rows, :] for lt in range(nlt)], axis=1)
        a = jax.nn.gelu(y).astype(BF16)
        z = _dot(a, wg_ref[...])
        mix_ref[rows, :] = (z[:, :D_MODEL] * jax.nn.sigmoid(z[:, D_MODEL:])).astype(BF16)


def _s5_mixer(h, gm, w_in, mats, w_glu, batch, seq):
    m_t, e_t, f_t, al = mats
    nck = seq // S5_L
    lanes = batch * nck
    assert nck % LANES == 0 and nck & (nck - 1) == 0
    ut = pl.pallas_call(
        functools.partial(_s5_in_kernel, seq=seq),
        grid=(batch,),
        in_specs=[pl.BlockSpec((seq, D_MODEL), lambda b: (b, 0)),
                  _resident((1, D_MODEL)), _resident(w_in.shape)],
        out_specs=pl.BlockSpec((S5_GROUPS, S5_K, nck), lambda b: (0, 0, b)),
        out_shape=jax.ShapeDtypeStruct((S5_GROUPS, S5_K, lanes), BF16),
        scratch_shapes=[pltpu.VMEM((D_MODEL // LANES, seq, LANES), F32)],
        compiler_params=_params("parallel"),
    )(h, gm, w_in)
    per_g = lambda *shape: pl.BlockSpec((None,) + shape, lambda g: (g,) + (0,) * len(shape))
    yt = pl.pallas_call(
        functools.partial(_s5_core_kernel, nck=nck),
        grid=(S5_GROUPS,),
        in_specs=[per_g(S5_K, lanes), per_g(S5_K, S5_K), per_g(2 * S5_STATE, S5_K),
                  per_g(S5_K, 2 * S5_STATE), per_g(2, S5_STATE, 1)],
        out_specs=per_g(S5_K, lanes),
        out_shape=jax.ShapeDtypeStruct((S5_GROUPS, S5_K, lanes), F32),
        compiler_params=_params("parallel"),
    )(ut, m_t, e_t, f_t, al)
    return pl.pallas_call(
        functools.partial(_s5_post_kernel, seq=seq),
        grid=(batch,),
        in_specs=[pl.BlockSpec((S5_GROUPS, S5_K, nck), lambda b: (0, 0, b)),
                  _resident(w_glu.shape)],
        out_specs=pl.BlockSpec((seq, D_MODEL), lambda b: (b, 0)),
        out_shape=jax.ShapeDtypeStruct((batch * seq, D_MODEL), BF16),
        scratch_shapes=[pltpu.VMEM((D_MODEL // LANES, seq, LANES), F32)],
        compiler_params=_params("parallel"),
    )(yt, w_glu)


def _fox_in_kernel(h_ref, g_ref, wqkv_ref, wf_ref, bf_ref, q_ref, k_ref, v_ref, qc_ref, kc_ref,
                   carry_s):
    tm = ROW_TILE

    @pl.when(pl.program_id(1) == 0)
    def _():
        carry_s[...] = jnp.zeros_like(carry_s)

    xn = _rms(h_ref[...], g_ref[...]).astype(BF16)
    proj = _dot(xn, wqkv_ref[...])
    log_f = _log_sigmoid(_dot(xn, wf_ref[...]) + bf_ref[...])
    row = lax.broadcasted_iota(jnp.int32, (tm, tm), 0)
    col = lax.broadcasted_iota(jnp.int32, (tm, tm), 1)
    tri = (row >= col).astype(BF16)
    hi, mid, lo = _split3(log_f)
    cum = _dot(tri, hi) + _dot(tri, mid) + _dot(tri, lo) + carry_s[...]
    carry_s[...] = cum[tm - 1:tm, :]
    hi, mid, lo = _split3(cum * LOG2E)
    hi, mid, lo = hi.astype(F32), mid.astype(F32), lo.astype(F32)
    lane = lax.broadcasted_iota(jnp.int32, (tm, LANES), 1)
    grp = lane >> 4
    qc = jnp.where(grp == 0, hi, jnp.where(grp == 1, mid, jnp.where(grp == 2, lo,
         jnp.where(grp < FOX_BIAS_GROUPS, 1.0, 0.0))))
    kc = jnp.where(grp < 3, 1.0, jnp.where(grp == 3, -hi, jnp.where(grp == 4, -mid,
         jnp.where(grp == 5, -lo, 0.0))))
    qc_ref[...] = qc.astype(BF16)
    kc_ref[...] = kc.astype(BF16)
    q_ref[...] = (proj[:, 0:D_MODEL] * (FOX_HEAD_DIM ** -0.5 * LOG2E)).astype(BF16)
    k_ref[...] = proj[:, D_MODEL:2 * D_MODEL].astype(BF16)
    for pair in range(FOX_HEADS // 2):
        vp = proj[:, 2 * D_MODEL + pair * LANES:2 * D_MODEL + (pair + 1) * LANES]
        v_ref[2 * pair] = jnp.where(lane < FOX_HEAD_DIM, vp,
                                    jnp.where(lane == FOX_HEAD_DIM, 1.0, 0.0)).astype(BF16)
        v_ref[2 * pair + 1] = jnp.where(lane >= FOX_HEAD_DIM, vp,
                                        jnp.where(lane == 0, 1.0, 0.0)).astype(BF16)


def _fox_flash_kernel(q_ref, qc_ref, k_ref, kc_ref, v_ref, o_ref, *, seq):
    t = FOX_TQ
    pair = pl.program_id(1)
    lane = lax.broadcasted_iota(jnp.int32, (t, LANES), 1)
    row = lax.broadcasted_iota(jnp.int32, (t, t), 0)
    col = lax.broadcasted_iota(jnp.int32, (t, t), 1)
    causal = row >= col
    own = (lane < FOX_HEAD_DIM, lane >= FOX_HEAD_DIM)
    mine = [((lane & (FOX_HEADS - 1)) == 2 * pair + half) & (lane < FOX_HEADS * FOX_BIAS_GROUPS)
            for half in range(2)]
    for i in range(seq // t):
        qrows = slice(i * t, (i + 1) * t)
        q = q_ref[qrows, :].astype(F32)
        qc = qc_ref[qrows, :].astype(F32)
        accs = []
        for half in range(2):
            qcat = jnp.concatenate([jnp.where(own[half], q, 0.0), jnp.where(mine[half], qc, 0.0)],
                                   axis=1).astype(BF16)
            m = jnp.full((t, 1), NEG_INF, F32)
            acc = jnp.zeros((t, LANES), F32)
            for j in range(i + 1):
                krows = slice(j * t, (j + 1) * t)
                kcat = jnp.concatenate([k_ref[krows, :], kc_ref[krows, :]], axis=1)
                s = _dot_nt(qcat, kcat)
                if j == i:
                    s = jnp.where(causal, s, NEG_INF)
                m_new = jnp.maximum(m, jnp.max(s, axis=-1, keepdims=True))
                p = jnp.exp2(s - m_new)
                acc = jnp.exp2(m - m_new) * acc + _dot(p.astype(BF16), v_ref[half, krows, :])
                m = m_new
            accs.append(acc)
        out_a = accs[0] / accs[0][:, FOX_HEAD_DIM:FOX_HEAD_DIM + 1]
        out_b = accs[1] / accs[1][:, 0:1]
        o_ref[qrows, :] = jnp.where(lane < FOX_HEAD_DIM, out_a, out_b).astype(BF16)


def _fox_mixer(h, gm, wqkv, wf, bf, batch, seq):
    tm = ROW_TILE
    nt = seq // tm
    m = batch * seq
    tok = lambda b, i: (b * nt + i, 0)
    q, k, v, qc, kc = pl.pallas_call(
        _fox_in_kernel,
        grid=(batch, nt),
        in_specs=[pl.BlockSpec((tm, D_MODEL), tok),
                  _resident((1, D_MODEL)), _resident(wqkv.shape), _resident(wf.shape),
                  _resident((1, LANES))],
        out_specs=[pl.BlockSpec((tm, D_MODEL), tok), pl.BlockSpec((tm, D_MODEL), tok),
                   pl.BlockSpec((None, FOX_HEADS, tm, LANES), lambda b, i: (b, 0, i, 0)),
                   pl.BlockSpec((tm, LANES), tok), pl.BlockSpec((tm, LANES), tok)],
        out_shape=[jax.ShapeDtypeStruct((m, D_MODEL), BF16),
                   jax.ShapeDtypeStruct((m, D_MODEL), BF16),
                   jax.ShapeDtypeStruct((batch, FOX_HEADS, seq, LANES), BF16),
                   jax.ShapeDtypeStruct((m, LANES), BF16),
                   jax.ShapeDtypeStruct((m, LANES), BF16)],
        scratch_shapes=[pltpu.VMEM((1, LANES), F32)],
        compiler_params=_params("parallel", "arbitrary"),
    )(h, gm, wqkv, wf, bf)
    slab = pl.BlockSpec((seq, LANES), lambda b, p: (b, p))
    bias = pl.BlockSpec((seq, LANES), lambda b, p: (b, 0))
    return pl.pallas_call(
        functools.partial(_fox_flash_kernel, seq=seq),
        grid=(batch, FOX_HEADS // 2),
        in_specs=[slab, bias, slab, bias,
                  pl.BlockSpec((None, 2, seq, LANES), lambda b, p: (b, p, 0, 0))],
        out_specs=slab,
        out_shape=jax.ShapeDtypeStruct((m, D_MODEL), BF16),
        compiler_params=_params("parallel", "parallel"),
    )(q, qc, k, kc, v)


def _gla_in_kernel(h_ref, g_ref, w_ref, wg2_ref, bg_ref, qd_ref, ki_ref, ku_ref, v_ref, rs_ref,
                   dec_ref):
    tm = ROW_TILE
    xn = _rms(h_ref[...], g_ref[...]).astype(BF16)
    proj = _dot(xn, w_ref[...])
    q = proj[:, 0:GLA_KD]
    k = proj[:, GLA_KD:2 * GLA_KD]
    v = proj[:, 2 * GLA_KD:2 * GLA_KD + GLA_VD]
    r = proj[:, 2 * GLA_KD + GLA_VD:2 * GLA_KD + 2 * GLA_VD]
    g_lr = proj[:, 2 * GLA_KD + 2 * GLA_VD:].astype(BF16)
    log_a = _log_sigmoid(_dot(g_lr, wg2_ref[...]) + bg_ref[...]) / GLA_GATE_NORM
    tb = 256
    row = lax.broadcasted_iota(jnp.int32, (tb, tb), 0)
    col = lax.broadcasted_iota(jnp.int32, (tb, tb), 1)
    tri = (((row >> 6) == (col >> 6)) & (row >= col)).astype(BF16)
    hi = log_a.astype(BF16)
    lo = (log_a - hi.astype(F32)).astype(BF16)
    bcum = jnp.concatenate(
        [_dot(tri, hi[r:r + tb, :]) + _dot(tri, lo[r:r + tb, :]) for r in range(0, tm, tb)], axis=0)
    nch = tm // GLA_CHUNK
    crow = lax.broadcasted_iota(jnp.int32, (nch, tm), 0)
    ccol = lax.broadcasted_iota(jnp.int32, (nch, tm), 1)
    pick = (crow == (ccol >> 6)).astype(BF16)
    b_last = _dot(pick, hi) + _dot(pick, lo)
    dec_ref[...] = jnp.exp(b_last)
    b_last_rows = jnp.broadcast_to(b_last[:, None, :], (nch, GLA_CHUNK, GLA_KD)).reshape(tm, GLA_KD)
    qd_ref[...] = (q * GLA_DK ** -0.5 * jnp.exp(bcum)).astype(BF16)
    ki_ref[...] = (k * jnp.exp(-bcum)).astype(BF16)
    ku_ref[...] = (k * jnp.exp(b_last_rows - bcum)).astype(BF16)
    v_ref[...] = v.astype(BF16)
    rs_ref[...] = (r * jax.nn.sigmoid(r)).astype(BF16)


def _gla_core_kernel(qd_ref, ki_ref, ku_ref, v_ref, rs_ref, dec_ref, gn_ref, o_ref, *, seq):
    c = GLA_CHUNK
    row = lax.broadcasted_iota(jnp.int32, (c, c), 0)
    col = lax.broadcasted_iota(jnp.int32, (c, c), 1)
    causal = row >= col
    gn = gn_ref[...]

    def chunk(i, states):
        rows = pl.ds(pl.multiple_of(i * c, c), c)
        dec = dec_ref[pl.ds(i, 1), :]
        new = []
        for hd in range(GLA_HEADS):
            kk = slice(hd * GLA_DK, (hd + 1) * GLA_DK)
            vv = slice(hd * GLA_DV, (hd + 1) * GLA_DV)
            qd, ki, ku, v = qd_ref[rows, kk], ki_ref[rows, kk], ku_ref[rows, kk], v_ref[rows, vv]
            att = jnp.where(causal, _dot_nt(qd, ki), 0.0)
            o = _dot(att.astype(BF16), v) + _dot_nt(qd, states[hd].astype(BF16))
            o = o * lax.rsqrt(jnp.mean(o * o, axis=-1, keepdims=True) + EPS)
            o_ref[rows, vv] = (o * gn[:, vv] * rs_ref[rows, vv].astype(F32)).astype(BF16)
            new.append(states[hd] * dec[:, kk] + _dot_tn(v, ku))
        return tuple(new)

    lax.fori_loop(0, seq // c, chunk, (jnp.zeros((GLA_DV, GLA_DK), F32),) * GLA_HEADS)


def _gla_mixer(h, gm, w_cat, wg2, bg, gn, batch, seq):
    tm = ROW_TILE
    m = batch * seq
    nch = tm // GLA_CHUNK
    row = lambda i: (i, 0)
    tok = lambda width, dtype: jax.ShapeDtypeStruct((m, width), dtype)
    qd, ki, ku, v, rs, dec = pl.pallas_call(
        _gla_in_kernel,
        grid=(m // tm,),
        in_specs=[pl.BlockSpec((tm, D_MODEL), row), _resident((1, D_MODEL)),
                  _resident(w_cat.shape), _resident(wg2.shape), _resident((1, GLA_KD))],
        out_specs=[pl.BlockSpec((tm, GLA_KD), row), pl.BlockSpec((tm, GLA_KD), row),
                   pl.BlockSpec((tm, GLA_KD), row), pl.BlockSpec((tm, GLA_VD), row),
                   pl.BlockSpec((tm, GLA_VD), row), pl.BlockSpec((nch, GLA_KD), row)],
        out_shape=[tok(GLA_KD, BF16), tok(GLA_KD, BF16), tok(GLA_KD, BF16), tok(GLA_VD, BF16),
                   tok(GLA_VD, BF16), jax.ShapeDtypeStruct((m // GLA_CHUNK, GLA_KD), F32)],
        compiler_params=_params("parallel"),
    )(h, gm, w_cat, wg2, bg)
    blk = lambda rows, width: pl.BlockSpec((rows, width), lambda b: (b, 0))
    return pl.pallas_call(
        functools.partial(_gla_core_kernel, seq=seq),
        grid=(batch,),
        in_specs=[blk(seq, GLA_KD), blk(seq, GLA_KD), blk(seq, GLA_KD), blk(seq, GLA_VD),
                  blk(seq, GLA_VD), blk(seq // GLA_CHUNK, GLA_KD), _resident((1, GLA_VD))],
        out_specs=blk(seq, GLA_VD),
        out_shape=jax.ShapeDtypeStruct((m, GLA_VD), BF16),
        compiler_params=_params("parallel"),
    )(qd, ki, ku, v, rs, dec, gn)


def _pad_lanes(w, width):
    return jnp.pad(w, ((0, 0), (0, width - w.shape[1])))


def kernel(x, p, norm_mix, norm_mlp, norm_ple, s5_w_in, s5_lam_re, s5_lam_im, s5_log_dt, s5_b_re,
           s5_b_im, s5_c_re, s5_c_im, s5_d, s5_w_glu, s5_w_out, fox_w_in, fox_b_f, fox_w_out,
           gla_w_in, gla_w_g2, gla_b_g, gla_norm, gla_w_out, mlp_w1, mlp_w2, ple_proj, ple_gate,
           final_norm):
    batch, seq, _ = x.shape
    depth = p.shape[0]
    m = batch * seq
    h = x.reshape(m, D_MODEL)
    p_all = p.reshape(depth, m, PLE_DIM)
    gf = final_norm.reshape(1, D_MODEL)
    for i in range(depth):
        mixer, j = i % 3, i // 3
        gm = norm_mix[i].reshape(1, D_MODEL)
        if mixer == 0:
            mats = _s5_prep(s5_lam_re[j], s5_lam_im[j], s5_log_dt[j], s5_b_re[j], s5_b_im[j],
                            s5_c_re[j], s5_c_im[j], s5_d[j])
            mix = _s5_mixer(h, gm, s5_w_in[j].astype(BF16), mats, s5_w_glu[j].astype(BF16),
                            batch, seq)
            wo = s5_w_out[j]
        elif mixer == 1:
            w = fox_w_in[j]
            wqkv = w[:, :3 * D_MODEL].astype(BF16)
            wf = _pad_lanes(jnp.tile(w[:, 3 * D_MODEL:], (1, FOX_BIAS_GROUPS)), LANES).astype(BF16)
            bf = _pad_lanes(jnp.tile(fox_b_f[j].reshape(1, FOX_HEADS), (1, FOX_BIAS_GROUPS)), LANES)
            mix = _fox_mixer(h, gm, wqkv, wf, bf, batch, seq)
            wo = fox_w_out[j]
        else:
            w_cat = _pad_lanes(gla_w_in[j], 2 * GLA_KD + 2 * GLA_VD + LANES).astype(BF16)
            wg2 = jnp.pad(gla_w_g2[j], ((0, LANES - GLA_RANK), (0, 0))).astype(BF16)
            mix = _gla_mixer(h, gm, w_cat, wg2, gla_b_g[j].reshape(1, GLA_KD),
                             gla_norm[j].reshape(1, GLA_VD), batch, seq)
            wo = gla_w_out[j]
        h = _tail(h, mix, wo.astype(BF16), norm_mlp[i].reshape(1, D_MODEL),
                  mlp_w1[i].astype(BF16), mlp_w2[i].astype(BF16),
                  norm_ple[i].reshape(1, D_MODEL), ple_gate[i].astype(BF16), p_all, i,
                  ple_proj[i].astype(BF16), gf, final=(i == depth - 1))
    return h.reshape(batch, seq, D_MODEL)
```

```python
import functools

import jax
import jax.numpy as jnp
from jax import lax
from jax.experimental import pallas as pl
from jax.experimental.pallas import tpu as pltpu

F32 = jnp.float32
BF16 = jnp.bfloat16

D_MODEL = 1024
D_FF = 4 * D_MODEL
PLE_DIM = 256
EPS = 1e-6
NEG_INF = -1e30
LOG2E = 1.4426950408889634

V7X_VMEM_BYTES = 64 * 1024 * 1024
VMEM_LIMIT_BYTES = V7X_VMEM_BYTES - 8 * 1024 * 1024
LANES = 128

S5_GROUP = 16
S5_GROUPS = D_MODEL // S5_GROUP
S5_STATE = 64
S5_L = 16
S5_K = S5_L * S5_GROUP
S5_PREP_GROUPS = 16

FOX_HEADS = 16
FOX_HEAD_DIM = D_MODEL // FOX_HEADS
FOX_TQ = 512
FOX_BIAS_GROUPS = 6

GLA_HEADS = 4
GLA_KD = D_MODEL // 2
GLA_VD = D_MODEL
GLA_DK = GLA_KD // GLA_HEADS
GLA_DV = GLA_VD // GLA_HEADS
GLA_RANK = 16
GLA_GATE_NORM = 16.0
GLA_CHUNK = 64

ROW_TILE = 512
CAST_BLOCK_BYTES = 4 * 1024 * 1024


def _params(*sem):
    return pltpu.CompilerParams(dimension_semantics=sem, vmem_limit_bytes=VMEM_LIMIT_BYTES)


def _resident(shape):
    zeros = (0,) * len(shape)
    return pl.BlockSpec(shape, lambda *_: zeros, pipeline_mode=pl.Buffered(1))


def _layer(shape, layer):
    index = (layer,) + (0,) * len(shape)
    return pl.BlockSpec((None,) + tuple(shape), lambda *_: index, pipeline_mode=pl.Buffered(1))


def _cast_kernel(x_ref, o_ref):
    o_ref[...] = x_ref[...].astype(BF16)


def _to_bf16(w):
    layers, k, n = w.shape
    tk = max(8, min(k, CAST_BLOCK_BYTES // (4 * n)))
    assert k % tk == 0
    spec = pl.BlockSpec((None, tk, n), lambda l, i: (l, i, 0))
    return pl.pallas_call(
        _cast_kernel, grid=(layers, k // tk), in_specs=[spec], out_specs=spec,
        out_shape=jax.ShapeDtypeStruct(w.shape, BF16),
        compiler_params=_params("parallel", "parallel"),
    )(w)


def _rms(x, g):
    ms = jnp.mean(x * x, axis=-1, keepdims=True)
    return x * lax.rsqrt(ms + EPS) * g


def _log_sigmoid(x):
    return jnp.minimum(x, 0.0) - jnp.log1p(jnp.exp(-jnp.abs(x)))


def _dot(a, b):
    return jnp.dot(a, b, preferred_element_type=F32)


def _dot_nt(a, b):
    return lax.dot_general(a, b, (((1,), (1,)), ((), ())), preferred_element_type=F32)


def _dot_tn(a, b):
    return lax.dot_general(a, b, (((0,), (0,)), ((), ())), preferred_element_type=F32)


def _split3(x):
    hi = x.astype(BF16)
    r1 = x - hi.astype(F32)
    mid = r1.astype(BF16)
    lo = (r1 - mid.astype(F32)).astype(BF16)
    return hi, mid, lo


def _tail_kernel(h_ref, mix_ref, wo_ref, gm_ref, w1_ref, w2_ref, gp_ref, wpg_ref, p_ref,
                 wpp_ref, gf_ref, o_ref, *, final):
    h = h_ref[...] + _dot(mix_ref[...], wo_ref[...])
    xn = _rms(h, gm_ref[...]).astype(BF16)
    ff_chunk = D_FF // 4
    mlp = None
    for c in range(4):
        a = _dot(xn, w1_ref[:, c * ff_chunk:(c + 1) * ff_chunk])
        a = jnp.square(jnp.maximum(a, 0.0)).astype(BF16)
        part = _dot(a, w2_ref[c * ff_chunk:(c + 1) * ff_chunk, :])
        mlp = part if mlp is None else mlp + part
    h = h + mlp
    xg = _rms(h, gp_ref[...]).astype(BF16)
    gate = jax.nn.sigmoid(_dot(xg, wpg_ref[...]))
    pe = _dot(p_ref[...].astype(BF16), wpp_ref[...])
    h = h + pe * gate
    if final:
        h = _rms(h, gf_ref[...])
    o_ref[...] = h


def _tail(h, mix, wo, wo_layer, gm, w1, w2, gp, wpg, p_all, wpp, gf, layer, final):
    m = h.shape[0]
    tm = ROW_TILE
    row = lambda i: (i, 0)
    return pl.pallas_call(
        functools.partial(_tail_kernel, final=final),
        grid=(m // tm,),
        in_specs=[
            pl.BlockSpec((tm, D_MODEL), row),
            pl.BlockSpec((tm, mix.shape[1]), row),
            _layer(wo.shape[1:], wo_layer),
            _layer((1, D_MODEL), layer),
            _layer(w1.shape[1:], layer),
            _layer(w2.shape[1:], layer),
            _layer((1, D_MODEL), layer),
            _layer(wpg.shape[1:], layer),
            pl.BlockSpec((None, tm, PLE_DIM), lambda i: (layer, i, 0)),
            _layer(wpp.shape[1:], layer),
            _resident((1, D_MODEL)),
        ],
        out_specs=pl.BlockSpec((tm, D_MODEL), row),
        out_shape=jax.ShapeDtypeStruct((m, D_MODEL), F32),
        compiler_params=_params("parallel"),
    )(h, mix, wo, gm, w1, w2, gp, wpg, p_all, wpp, gf)


def _cmul(ar, ai, br, bi):
    return ar * br - ai * bi, ar * bi + ai * br


def _cpow(ar, ai, n, shape):
    pr = jnp.ones(shape, F32)
    pi = jnp.zeros(shape, F32)
    br, bi = ar, ai
    for bit in range(5):
        tr, ti = _cmul(pr, pi, br, bi)
        take = ((n >> bit) & 1) == 1
        pr = jnp.where(take, tr, pr)
        pi = jnp.where(take, ti, pi)
        br, bi = _cmul(br, bi, br, bi)
    return pr, pi


def _s5_discretize(lr, li, ld):
    dt = jnp.exp(ld)
    mag = jnp.exp(lr * dt)
    ar = mag * jnp.cos(li * dt)
    ai = mag * jnp.sin(li * dt)
    den = lr * lr + li * li
    nr = ar - 1.0
    coef_re = (nr * lr + ai * li) / den
    coef_im = (ai * lr - nr * li) / den
    return ar, ai, coef_re, coef_im


def _s5_prep_group(lrc, lic, lrr, lir, ld, brt, bit, cr, ci, crt, cit, d):
    n, k = S5_STATE, S5_K
    ar, ai, cfr, cfi = _s5_discretize(lrc, lic, ld)
    bbr = cfr * brt - cfi * bit
    bbi = cfr * bit + cfi * brt
    lag = lax.broadcasted_iota(jnp.int32, (n, k), 1) >> 4
    pr, pi = _cpow(ar, ai, lag, (n, k))
    wr, wi = _cmul(pr, pi, bbr, bbi)
    hp = lax.Precision.HIGHEST
    kall = (jnp.dot(cr, wr, precision=hp, preferred_element_type=F32)
            - jnp.dot(ci, wi, precision=hp, preferred_element_type=F32))
    row = lax.broadcasted_iota(jnp.int32, (S5_GROUP, k), 0)
    col = lax.broadcasted_iota(jnp.int32, (S5_GROUP, k), 1)
    kall = kall + jnp.where(row == col, d, 0.0)
    pr, pi = _cpow(ar, ai, (S5_L - 1) - lag, (n, k))
    er, ei = _cmul(pr, pi, bbr, bbi)
    a2r, a2i = _cmul(ar, ai, ar, ai)
    a4r, a4i = _cmul(a2r, a2i, a2r, a2i)
    a8r, a8i = _cmul(a4r, a4i, a4r, a4i)
    a16r, a16i = _cmul(a8r, a8i, a8r, a8i)
    ar, ai, _, _ = _s5_discretize(lrr, lir, ld)
    step = (lax.broadcasted_iota(jnp.int32, (k, n), 0) >> 4) + 1
    pr, pi = _cpow(ar, ai, step, (k, n))
    fr = crt * pr - cit * pi
    fi = -(crt * pi + cit * pr)
    return kall, er, ei, a16r, a16i, fr, fi


def _s5_prep_kernel(lrc_ref, lic_ref, lrr_ref, lir_ref, ld_ref, brt_ref, bit_ref, cr_ref, ci_ref,
                    crt_ref, cit_ref, d_ref, kall_ref, e_ref, f_ref, al_ref):
    n = S5_STATE
    for gi in range(S5_PREP_GROUPS):
        kall, er, ei, alr, ali, fr, fi = _s5_prep_group(
            lrc_ref[gi], lic_ref[gi], lrr_ref[gi], lir_ref[gi], ld_ref[gi], brt_ref[gi],
            bit_ref[gi], cr_ref[gi], ci_ref[gi], crt_ref[gi], cit_ref[gi], d_ref[gi])
        kall_ref[gi] = kall
        e_ref[gi, 0:n, :] = er
        e_ref[gi, n:2 * n, :] = ei
        al_ref[gi, 0] = alr
        al_ref[gi, 1] = ali
        f_ref[gi, :, 0:n] = fr
        f_ref[gi, :, n:2 * n] = fi


def _s5_prep(lam_re, lam_im, log_dt, b_re, b_im, c_re, c_im, d_skip):
    n, k = S5_STATE, S5_K
    g = lam_re.shape[0] * S5_GROUPS
    gb = S5_PREP_GROUPS
    blk = lambda *shape: pl.BlockSpec((gb,) + shape, lambda i: (i,) + (0,) * len(shape))
    b_re, b_im = b_re.reshape(g, n, S5_GROUP), b_im.reshape(g, n, S5_GROUP)
    c_re, c_im = c_re.reshape(g, S5_GROUP, n), c_im.reshape(g, S5_GROUP, n)
    kall, e, f, al = pl.pallas_call(
        _s5_prep_kernel,
        grid=(g // gb,),
        in_specs=[blk(n, 1), blk(n, 1), blk(1, n), blk(1, n), blk(1, 1),
                  blk(n, k), blk(n, k), blk(S5_GROUP, n), blk(S5_GROUP, n),
                  blk(k, n), blk(k, n), blk(S5_GROUP, 1)],
        out_specs=[blk(S5_GROUP, k), blk(2 * n, k), blk(k, 2 * n), blk(2, n, 1)],
        out_shape=[jax.ShapeDtypeStruct((g, S5_GROUP, k), F32),
                   jax.ShapeDtypeStruct((g, 2 * n, k), F32),
                   jax.ShapeDtypeStruct((g, k, 2 * n), F32),
                   jax.ShapeDtypeStruct((g, 2, n, 1), F32)],
        compiler_params=_params("parallel"),
    )(lam_re.reshape(g, n, 1), lam_im.reshape(g, n, 1), lam_re.reshape(g, 1, n),
      lam_im.reshape(g, 1, n), log_dt.reshape(g, 1, 1),
      jnp.tile(b_re, (1, 1, S5_L)), jnp.tile(b_im, (1, 1, S5_L)), c_re, c_im,
      jnp.tile(c_re, (1, S5_L, 1)), jnp.tile(c_im, (1, S5_L, 1)),
      d_skip.reshape(g, S5_GROUP, 1))
    kk = kall.reshape(g, S5_GROUP, S5_L, S5_GROUP)
    t = jnp.arange(S5_L)
    lagm = t[:, None] - t[None, :]
    m = jnp.take(kk, jnp.clip(lagm, 0, S5_L - 1).reshape(-1), axis=2)
    m = m.reshape(g, S5_GROUP, S5_L, S5_L, S5_GROUP)
    m = jnp.where((lagm >= 0)[None, None, :, :, None], m, 0.0)
    m = m.transpose(0, 2, 1, 3, 4).reshape(g, k, k)
    return m.astype(BF16), e.astype(BF16), f.astype(BF16), al


def _s5_in_kernel(h_ref, g_ref, w_ref, ut_ref, u_s, *, seq):
    for r in range(seq // ROW_TILE):
        rows = slice(r * ROW_TILE, (r + 1) * ROW_TILE)
        xn = _rms(h_ref[rows, :], g_ref[...]).astype(BF16)
        u = _dot(xn, w_ref[...])
        for lt in range(D_MODEL // LANES):
            u_s[lt, rows, :] = u[:, lt * LANES:(lt + 1) * LANES]
    nck = seq // S5_L
    gpl = LANES // S5_GROUP
    for j in range(S5_L):
        for lt in range(D_MODEL // LANES):
            xj = u_s[lt, pl.ds(j, nck, stride=S5_L), :]
            ut_ref[gpl * lt:gpl * (lt + 1), S5_GROUP * j:S5_GROUP * (j + 1), :] = (
                xj.T.reshape(gpl, S5_GROUP, nck).astype(BF16))


def _s5_core_kernel(ut_ref, m_ref, e_ref, f_ref, al_ref, yt_ref, *, nck):
    n = S5_STATE
    u = ut_ref[...]
    y = _dot(m_ref[...], u)
    s = _dot(e_ref[...], u)
    sr, si = s[0:n, :], s[n:2 * n, :]
    ar, ai = al_ref[0], al_ref[1]
    chunk = lax.broadcasted_iota(jnp.int32, sr.shape, 1) & (nck - 1)
    shift = 1
    while shift < nck:
        pr = pltpu.roll(sr, shift, axis=1)
        pi = pltpu.roll(si, shift, axis=1)
        tr, ti = _cmul(ar, ai, pr, pi)
        keep = chunk >= shift
        sr = sr + jnp.where(keep, tr, 0.0)
        si = si + jnp.where(keep, ti, 0.0)
        ar, ai = _cmul(ar, ai, ar, ai)
        shift *= 2
    first = chunk == 0
    sr = jnp.where(first, 0.0, pltpu.roll(sr, 1, axis=1))
    si = jnp.where(first, 0.0, pltpu.roll(si, 1, axis=1))
    sp = jnp.concatenate([sr, si], axis=0).astype(BF16)
    yt_ref[...] = y + _dot(f_ref[...], sp)


def _s5_post_kernel(yt_ref, wg_ref, mix_ref, y_s, *, seq):
    nck = seq // S5_L
    gpl = LANES // S5_GROUP
    nlt = D_MODEL // LANES
    for t in range(S5_L):
        for lt in range(nlt):
            piece = yt_ref[gpl * lt:gpl * (lt + 1), S5_GROUP * t:S5_GROUP * (t + 1), :]
            y_s[lt, pl.ds(t, nck, stride=S5_L), :] = piece.reshape(LANES, nck).T
    for r in range(seq // ROW_TILE):
        rows = slice(r * ROW_TILE, (r + 1) * ROW_TILE)
        y = jnp.concatenate([y_s[lt, rows, :] for lt in range(nlt)], axis=1)
        a = jax.nn.gelu(y).astype(BF16)
        z = _dot(a, wg_ref[...])
        mix_ref[rows, :] = (z[:, :D_MODEL] * jax.nn.sigmoid(z[:, D_MODEL:])).astype(BF16)


def _s5_mixer(h, gm, layer, w_in, mats, w_glu, s5_layer, batch, seq):
    m_t, e_t, f_t, al = mats
    nck = seq // S5_L
    lanes = batch * nck
    assert nck % LANES == 0 and nck & (nck - 1) == 0
    ut = pl.pallas_call(
        functools.partial(_s5_in_kernel, seq=seq),
        grid=(batch,),
        in_specs=[pl.BlockSpec((seq, D_MODEL), lambda b: (b, 0)),
                  _layer((1, D_MODEL), layer), _layer(w_in.shape[1:], s5_layer)],
        out_specs=pl.BlockSpec((S5_GROUPS, S5_K, nck), lambda b: (0, 0, b)),
        out_shape=jax.ShapeDtypeStruct((S5_GROUPS, S5_K, lanes), BF16),
        scratch_shapes=[pltpu.VMEM((D_MODEL // LANES, seq, LANES), F32)],
        compiler_params=_params("parallel"),
    )(h, gm, w_in)
    g0 = s5_layer * S5_GROUPS
    per_g = lambda *shape: pl.BlockSpec((None,) + shape, lambda g: (g,) + (0,) * len(shape))
    par_g = lambda *shape: pl.BlockSpec((None,) + shape, lambda g: (g0 + g,) + (0,) * len(shape))
    yt = pl.pallas_call(
        functools.partial(_s5_core_kernel, nck=nck),
        grid=(S5_GROUPS,),
        in_specs=[per_g(S5_K, lanes), par_g(S5_K, S5_K), par_g(2 * S5_STATE, S5_K),
                  par_g(S5_K, 2 * S5_STATE), par_g(2, S5_STATE, 1)],
        out_specs=per_g(S5_K, lanes),
        out_shape=jax.ShapeDtypeStruct((S5_GROUPS, S5_K, lanes), F32),
        compiler_params=_params("parallel"),
    )(ut, m_t, e_t, f_t, al)
    return pl.pallas_call(
        functools.partial(_s5_post_kernel, seq=seq),
        grid=(batch,),
        in_specs=[pl.BlockSpec((S5_GROUPS, S5_K, nck), lambda b: (0, 0, b)),
                  _layer(w_glu.shape[1:], s5_layer)],
        out_specs=pl.BlockSpec((seq, D_MODEL), lambda b: (b, 0)),
        out_shape=jax.ShapeDtypeStruct((batch * seq, D_MODEL), BF16),
        scratch_shapes=[pltpu.VMEM((D_MODEL // LANES, seq, LANES), F32)],
        compiler_params=_params("parallel"),
    )(yt, w_glu)


def _fox_in_kernel(h_ref, g_ref, wqkv_ref, wf_ref, bf_ref, q_ref, k_ref, v_ref, qc_ref, kc_ref,
                   carry_s):
    tm = ROW_TILE

    @pl.when(pl.program_id(1) == 0)
    def _():
        carry_s[...] = jnp.zeros_like(carry_s)

    xn = _rms(h_ref[...], g_ref[...]).astype(BF16)
    proj = _dot(xn, wqkv_ref[...])
    log_f = _log_sigmoid(_dot(xn, wf_ref[...]) + bf_ref[...])
    row = lax.broadcasted_iota(jnp.int32, (tm, tm), 0)
    col = lax.broadcasted_iota(jnp.int32, (tm, tm), 1)
    tri = (row >= col).astype(BF16)
    hi, mid, lo = _split3(log_f)
    cum = _dot(tri, hi) + _dot(tri, mid) + _dot(tri, lo) + carry_s[...]
    carry_s[...] = cum[tm - 1:tm, :]
    hi, mid, lo = _split3(cum * LOG2E)
    hi, mid, lo = hi.astype(F32), mid.astype(F32), lo.astype(F32)
    lane = lax.broadcasted_iota(jnp.int32, (tm, LANES), 1)
    grp = lane >> 4
    qc = jnp.where(grp == 0, hi, jnp.where(grp == 1, mid, jnp.where(grp == 2, lo,
         jnp.where(grp < FOX_BIAS_GROUPS, 1.0, 0.0))))
    kc = jnp.where(grp < 3, 1.0, jnp.where(grp == 3, -hi, jnp.where(grp == 4, -mid,
         jnp.where(grp == 5, -lo, 0.0))))
    qc_ref[...] = qc.astype(BF16)
    kc_ref[...] = kc.astype(BF16)
    q_ref[...] = (proj[:, 0:D_MODEL] * (FOX_HEAD_DIM ** -0.5 * LOG2E)).astype(BF16)
    k_ref[...] = proj[:, D_MODEL:2 * D_MODEL].astype(BF16)
    for pair in range(FOX_HEADS // 2):
        vp = proj[:, 2 * D_MODEL + pair * LANES:2 * D_MODEL + (pair + 1) * LANES]
        v_ref[2 * pair] = jnp.where(lane < FOX_HEAD_DIM, vp,
                                    jnp.where(lane == FOX_HEAD_DIM, 1.0, 0.0)).astype(BF16)
        v_ref[2 * pair + 1] = jnp.where(lane >= FOX_HEAD_DIM, vp,
                                        jnp.where(lane == 0, 1.0, 0.0)).astype(BF16)


def _fox_flash_kernel(q_ref, qc_ref, k_ref, kc_ref, v_ref, o_ref, *, seq):
    t = FOX_TQ
    pair = pl.program_id(1)
    lane = lax.broadcasted_iota(jnp.int32, (t, LANES), 1)
    row = lax.broadcasted_iota(jnp.int32, (t, t), 0)
    col = lax.broadcasted_iota(jnp.int32, (t, t), 1)
    causal = row >= col
    own = (lane < FOX_HEAD_DIM, lane >= FOX_HEAD_DIM)
    mine = [((lane & (FOX_HEADS - 1)) == 2 * pair + half) & (lane < FOX_HEADS * FOX_BIAS_GROUPS)
            for half in range(2)]
    for i in range(seq // t):
        qrows = slice(i * t, (i + 1) * t)
        q = q_ref[qrows, :].astype(F32)
        qc = qc_ref[qrows, :].astype(F32)
        accs = []
        for half in range(2):
            qcat = jnp.concatenate([jnp.where(own[half], q, 0.0), jnp.where(mine[half], qc, 0.0)],
                                   axis=1).astype(BF16)
            m = jnp.full((t, 1), NEG_INF, F32)
            acc = jnp.zeros((t, LANES), F32)
            for j in range(i + 1):
                krows = slice(j * t, (j + 1) * t)
                kcat = jnp.concatenate([k_ref[krows, :], kc_ref[krows, :]], axis=1)
                s = _dot_nt(qcat, kcat)
                if j == i:
                    s = jnp.where(causal, s, NEG_INF)
                m_new = jnp.maximum(m, jnp.max(s, axis=-1, keepdims=True))
                p = jnp.exp2(s - m_new)
                acc = jnp.exp2(m - m_new) * acc + _dot(p.astype(BF16), v_ref[half, krows, :])
                m = m_new
            accs.append(acc)
        out_a = accs[0] / accs[0][:, FOX_HEAD_DIM:FOX_HEAD_DIM + 1]
        out_b = accs[1] / accs[1][:, 0:1]
        o_ref[qrows, :] = jnp.where(lane < FOX_HEAD_DIM, out_a, out_b).astype(BF16)


def _fox_mixer(h, gm, layer, wqkv, wf, bf, batch, seq):
    tm = ROW_TILE
    nt = seq // tm
    m = batch * seq
    tok = lambda b, i: (b * nt + i, 0)
    q, k, v, qc, kc = pl.pallas_call(
        _fox_in_kernel,
        grid=(batch, nt),
        in_specs=[pl.BlockSpec((tm, D_MODEL), tok),
                  _layer((1, D_MODEL), layer), _resident(wqkv.shape), _resident(wf.shape),
                  _resident((1, LANES))],
        out_specs=[pl.BlockSpec((tm, D_MODEL), tok), pl.BlockSpec((tm, D_MODEL), tok),
                   pl.BlockSpec((None, FOX_HEADS, tm, LANES), lambda b, i: (b, 0, i, 0)),
                   pl.BlockSpec((tm, LANES), tok), pl.BlockSpec((tm, LANES), tok)],
        out_shape=[jax.ShapeDtypeStruct((m, D_MODEL), BF16),
                   jax.ShapeDtypeStruct((m, D_MODEL), BF16),
                   jax.ShapeDtypeStruct((batch, FOX_HEADS, seq, LANES), BF16),
                   jax.ShapeDtypeStruct((m, LANES), BF16),
                   jax.ShapeDtypeStruct((m, LANES), BF16)],
        scratch_shapes=[pltpu.VMEM((1, LANES), F32)],
        compiler_params=_params("parallel", "arbitrary"),
    )(h, gm, wqkv, wf, bf)
    slab = pl.BlockSpec((seq, LANES), lambda b, p: (b, p))
    bias = pl.BlockSpec((seq, LANES), lambda b, p: (b, 0))
    return pl.pallas_call(
        functools.partial(_fox_flash_kernel, seq=seq),
        grid=(batch, FOX_HEADS // 2),
        in_specs=[slab, bias, slab, bias,
                  pl.BlockSpec((None, 2, seq, LANES), lambda b, p: (b, p, 0, 0))],
        out_specs=slab,
        out_shape=jax.ShapeDtypeStruct((m, D_MODEL), BF16),
        compiler_params=_params("parallel", "parallel"),
    )(q, qc, k, kc, v)


def _gla_in_kernel(h_ref, g_ref, w_ref, wg2_ref, bg_ref, qd_ref, ki_ref, ku_ref, v_ref, rs_ref,
                   dec_ref):
    tm = ROW_TILE
    xn = _rms(h_ref[...], g_ref[...]).astype(BF16)
    proj = _dot(xn, w_ref[...])
    q = proj[:, 0:GLA_KD]
    k = proj[:, GLA_KD:2 * GLA_KD]
    v = proj[:, 2 * GLA_KD:2 * GLA_KD + GLA_VD]
    r = proj[:, 2 * GLA_KD + GLA_VD:2 * GLA_KD + 2 * GLA_VD]
    g_lr = proj[:, 2 * GLA_KD + 2 * GLA_VD:].astype(BF16)
    log_a = _log_sigmoid(_dot(g_lr, wg2_ref[...]) + bg_ref[...]) / GLA_GATE_NORM
    tb = 256
    row = lax.broadcasted_iota(jnp.int32, (tb, tb), 0)
    col = lax.broadcasted_iota(jnp.int32, (tb, tb), 1)
    tri = (((row >> 6) == (col >> 6)) & (row >= col)).astype(BF16)
    hi = log_a.astype(BF16)
    lo = (log_a - hi.astype(F32)).astype(BF16)
    bcum = jnp.concatenate(
        [_dot(tri, hi[r:r + tb, :]) + _dot(tri, lo[r:r + tb, :]) for r in range(0, tm, tb)], axis=0)
    nch = tm // GLA_CHUNK
    crow = lax.broadcasted_iota(jnp.int32, (nch, tm), 0)
    ccol = lax.broadcasted_iota(jnp.int32, (nch, tm), 1)
    pick = (crow == (ccol >> 6)).astype(BF16)
    b_last = _dot(pick, hi) + _dot(pick, lo)
    dec_ref[...] = jnp.exp(b_last)
    b_last_rows = jnp.broadcast_to(b_last[:, None, :], (nch, GLA_CHUNK, GLA_KD)).reshape(tm, GLA_KD)
    qd_ref[...] = (q * GLA_DK ** -0.5 * jnp.exp(bcum)).astype(BF16)
    ki_ref[...] = (k * jnp.exp(-bcum)).astype(BF16)
    ku_ref[...] = (k * jnp.exp(b_last_rows - bcum)).astype(BF16)
    v_ref[...] = v.astype(BF16)
    rs_ref[...] = (r * jax.nn.sigmoid(r)).astype(BF16)


def _gla_core_kernel(qd_ref, ki_ref, ku_ref, v_ref, rs_ref, dec_ref, gn_ref, o_ref, *, seq):
    c = GLA_CHUNK
    row = lax.broadcasted_iota(jnp.int32, (c, c), 0)
    col = lax.broadcasted_iota(jnp.int32, (c, c), 1)
    causal = row >= col
    gn = gn_ref[...]

    def chunk(i, states):
        rows = pl.ds(pl.multiple_of(i * c, c), c)
        dec = dec_ref[pl.ds(i, 1), :]
        new = []
        for hd in range(GLA_HEADS):
            kk = slice(hd * GLA_DK, (hd + 1) * GLA_DK)
            vv = slice(hd * GLA_DV, (hd + 1) * GLA_DV)
            qd, ki, ku, v = qd_ref[rows, kk], ki_ref[rows, kk], ku_ref[rows, kk], v_ref[rows, vv]
            att = jnp.where(causal, _dot_nt(qd, ki), 0.0)
            o = _dot(att.astype(BF16), v) + _dot_nt(qd, states[hd].astype(BF16))
            o = o * lax.rsqrt(jnp.mean(o * o, axis=-1, keepdims=True) + EPS)
            o_ref[rows, vv] = (o * gn[:, vv] * rs_ref[rows, vv].astype(F32)).astype(BF16)
            new.append(states[hd] * dec[:, kk] + _dot_tn(v, ku))
        return tuple(new)

    lax.fori_loop(0, seq // c, chunk, (jnp.zeros((GLA_DV, GLA_DK), F32),) * GLA_HEADS, unroll=2)


def _gla_mixer(h, gm, layer, w_cat, wg2, bg, gn, batch, seq):
    tm = ROW_TILE
    m = batch * seq
    nch = tm // GLA_CHUNK
    row = lambda i: (i, 0)
    tok = lambda width, dtype: jax.ShapeDtypeStruct((m, width), dtype)
    qd, ki, ku, v, rs, dec = pl.pallas_call(
        _gla_in_kernel,
        grid=(m // tm,),
        in_specs=[pl.BlockSpec((tm, D_MODEL), row), _layer((1, D_MODEL), layer),
                  _resident(w_cat.shape), _resident(wg2.shape), _resident((1, GLA_KD))],
        out_specs=[pl.BlockSpec((tm, GLA_KD), row), pl.BlockSpec((tm, GLA_KD), row),
                   pl.BlockSpec((tm, GLA_KD), row), pl.BlockSpec((tm, GLA_VD), row),
                   pl.BlockSpec((tm, GLA_VD), row), pl.BlockSpec((nch, GLA_KD), row)],
        out_shape=[tok(GLA_KD, BF16), tok(GLA_KD, BF16), tok(GLA_KD, BF16), tok(GLA_VD, BF16),
                   tok(GLA_VD, BF16), jax.ShapeDtypeStruct((m // GLA_CHUNK, GLA_KD), F32)],
        compiler_params=_params("parallel"),
    )(h, gm, w_cat, wg2, bg)
    blk = lambda rows, width: pl.BlockSpec((rows, width), lambda b: (b, 0))
    return pl.pallas_call(
        functools.partial(_gla_core_kernel, seq=seq),
        grid=(batch,),
        in_specs=[blk(seq, GLA_KD), blk(seq, GLA_KD), blk(seq, GLA_KD), blk(seq, GLA_VD),
                  blk(seq, GLA_VD), blk(seq // GLA_CHUNK, GLA_KD), _resident((1, GLA_VD))],
        out_specs=blk(seq, GLA_VD),
        out_shape=jax.ShapeDtypeStruct((m, GLA_VD), BF16),
        compiler_params=_params("parallel"),
    )(qd, ki, ku, v, rs, dec, gn)


def _pad_lanes(w, width):
    return jnp.pad(w, ((0, 0), (0, width - w.shape[1])))


def kernel(x, p, norm_mix, norm_mlp, norm_ple, s5_w_in, s5_lam_re, s5_lam_im, s5_log_dt, s5_b_re,
           s5_b_im, s5_c_re, s5_c_im, s5_d, s5_w_glu, s5_w_out, fox_w_in, fox_b_f, fox_w_out,
           gla_w_in, gla_w_g2, gla_b_g, gla_norm, gla_w_out, mlp_w1, mlp_w2, ple_proj, ple_gate,
           final_norm):
    batch, seq, _ = x.shape
    depth = p.shape[0]
    m = batch * seq
    h = x.reshape(m, D_MODEL)
    p_all = p.reshape(depth, m, PLE_DIM)
    gf = final_norm.reshape(1, D_MODEL)
    gm_mix = norm_mix.reshape(depth, 1, D_MODEL)
    gm_mlp = norm_mlp.reshape(depth, 1, D_MODEL)
    gm_ple = norm_ple.reshape(depth, 1, D_MODEL)
    w1, w2 = _to_bf16(mlp_w1), _to_bf16(mlp_w2)
    wpg, wpp = _to_bf16(ple_gate), _to_bf16(ple_proj)
    wo_s5, wo_fox, wo_gla = _to_bf16(s5_w_out), _to_bf16(fox_w_out), _to_bf16(gla_w_out)
    s5_in, s5_glu = _to_bf16(s5_w_in), _to_bf16(s5_w_glu)
    s5_mats = _s5_prep(s5_lam_re, s5_lam_im, s5_log_dt, s5_b_re, s5_b_im, s5_c_re, s5_c_im, s5_d)
    for i in range(depth):
        mixer, j = i % 3, i // 3
        if mixer == 0:
            mix = _s5_mixer(h, gm_mix, i, s5_in, s5_mats, s5_glu, j, batch, seq)
            wo = wo_s5
        elif mixer == 1:
            w = fox_w_in[j]
            wqkv = w[:, :3 * D_MODEL].astype(BF16)
            wf = _pad_lanes(jnp.tile(w[:, 3 * D_MODEL:], (1, FOX_BIAS_GROUPS)), LANES).astype(BF16)
            bf = _pad_lanes(jnp.tile(fox_b_f[j].reshape(1, FOX_HEADS), (1, FOX_BIAS_GROUPS)), LANES)
            mix = _fox_mixer(h, gm_mix, i, wqkv, wf, bf, batch, seq)
            wo = wo_fox
        else:
            w_cat = _pad_lanes(gla_w_in[j], 2 * GLA_KD + 2 * GLA_VD + LANES).astype(BF16)
            wg2 = jnp.pad(gla_w_g2[j], ((0, LANES - GLA_RANK), (0, 0))).astype(BF16)
            mix = _gla_mixer(h, gm_mix, i, w_cat, wg2, gla_b_g[j].reshape(1, GLA_KD),
                             gla_norm[j].reshape(1, GLA_VD), batch, seq)
            wo = wo_gla
        h = _tail(h, mix, wo, j, gm_mlp, w1, w2, gm_ple, wpg, p_all, wpp, gf, layer=i,
                  final=(i == depth - 1))
    return h.reshape(batch, seq, D_MODEL)
```

```python
import functools

import jax
import jax.numpy as jnp
from jax import lax
from jax.experimental import pallas as pl
from jax.experimental.pallas import tpu as pltpu

F32 = jnp.float32
BF16 = jnp.bfloat16

D_MODEL = 1024
D_FF = 4 * D_MODEL
PLE_DIM = 256
EPS = 1e-6
NEG_INF = -1e30
LOG2E = 1.4426950408889634

V7X_VMEM_BYTES = 64 * 1024 * 1024
VMEM_LIMIT_BYTES = V7X_VMEM_BYTES - 8 * 1024 * 1024
LANES = 128

S5_GROUP = 16
S5_GROUPS = D_MODEL // S5_GROUP
S5_STATE = 64
S5_L = 16
S5_K = S5_L * S5_GROUP
S5_PREP_GROUPS = 16

FOX_HEADS = 16
FOX_HEAD_DIM = D_MODEL // FOX_HEADS
FOX_TQ = 512
FOX_BIAS_GROUPS = 6

GLA_HEADS = 4
GLA_KD = D_MODEL // 2
GLA_VD = D_MODEL
GLA_DK = GLA_KD // GLA_HEADS
GLA_DV = GLA_VD // GLA_HEADS
GLA_RANK = 16
GLA_GATE_NORM = 16.0
GLA_CHUNK = 64

ROW_TILE = 512
TILES_PER_STEP = 2
CAST_BLOCK_BYTES = 4 * 1024 * 1024


def _params(*sem):
    return pltpu.CompilerParams(dimension_semantics=sem, vmem_limit_bytes=VMEM_LIMIT_BYTES)


def _resident(shape):
    zeros = (0,) * len(shape)
    return pl.BlockSpec(shape, lambda *_: zeros, pipeline_mode=pl.Buffered(1))


def _layer(shape, layer):
    index = (layer,) + (0,) * len(shape)
    return pl.BlockSpec((None,) + tuple(shape), lambda *_: index, pipeline_mode=pl.Buffered(1))


def _cast_kernel(x_ref, o_ref):
    o_ref[...] = x_ref[...].astype(BF16)


def _to_bf16(w):
    layers, k, n = w.shape
    tk = max(8, min(k, CAST_BLOCK_BYTES // (4 * n)))
    assert k % tk == 0
    spec = pl.BlockSpec((None, tk, n), lambda l, i: (l, i, 0))
    return pl.pallas_call(
        _cast_kernel, grid=(layers, k // tk), in_specs=[spec], out_specs=spec,
        out_shape=jax.ShapeDtypeStruct(w.shape, BF16),
        compiler_params=_params("parallel", "parallel"),
    )(w)


def _rms(x, g):
    ms = jnp.mean(x * x, axis=-1, keepdims=True)
    return x * lax.rsqrt(ms + EPS) * g


def _log_sigmoid(x):
    return jnp.minimum(x, 0.0) - jnp.log1p(jnp.exp(-jnp.abs(x)))


def _dot(a, b):
    return jnp.dot(a, b, preferred_element_type=F32)


def _dot_nt(a, b):
    return lax.dot_general(a, b, (((1,), (1,)), ((), ())), preferred_element_type=F32)


def _dot_tn(a, b):
    return lax.dot_general(a, b, (((0,), (0,)), ((), ())), preferred_element_type=F32)


def _split3(x):
    hi = x.astype(BF16)
    r1 = x - hi.astype(F32)
    mid = r1.astype(BF16)
    lo = (r1 - mid.astype(F32)).astype(BF16)
    return hi, mid, lo


def _tail_kernel(h_ref, mix_ref, wo_ref, gm_ref, w1_ref, w2_ref, gp_ref, wpg_ref, p_ref,
                 wpp_ref, gf_ref, o_ref, *, final):
    h = h_ref[...] + _dot(mix_ref[...], wo_ref[...])
    xn = _rms(h, gm_ref[...]).astype(BF16)
    ff_chunk = D_FF // 4
    mlp = None
    for c in range(4):
        a = _dot(xn, w1_ref[:, c * ff_chunk:(c + 1) * ff_chunk])
        a = jnp.square(jnp.maximum(a, 0.0)).astype(BF16)
        part = _dot(a, w2_ref[c * ff_chunk:(c + 1) * ff_chunk, :])
        mlp = part if mlp is None else mlp + part
    h = h + mlp
    xg = _rms(h, gp_ref[...]).astype(BF16)
    gate = jax.nn.sigmoid(_dot(xg, wpg_ref[...]))
    pe = _dot(p_ref[...].astype(BF16), wpp_ref[...])
    h = h + pe * gate
    if final:
        h = _rms(h, gf_ref[...])
    o_ref[...] = h


def _tail(h, mix, wo, wo_layer, gm, w1, w2, gp, wpg, p_all, wpp, gf, layer, final):
    m = h.shape[0]
    tm = ROW_TILE
    row = lambda i: (i, 0)
    return pl.pallas_call(
        functools.partial(_tail_kernel, final=final),
        grid=(m // tm,),
        in_specs=[
            pl.BlockSpec((tm, D_MODEL), row),
            pl.BlockSpec((tm, mix.shape[1]), row),
            _layer(wo.shape[1:], wo_layer),
            _layer((1, D_MODEL), layer),
            _layer(w1.shape[1:], layer),
            _layer(w2.shape[1:], layer),
            _layer((1, D_MODEL), layer),
            _layer(wpg.shape[1:], layer),
            pl.BlockSpec((None, tm, PLE_DIM), lambda i: (layer, i, 0)),
            _layer(wpp.shape[1:], layer),
            _resident((1, D_MODEL)),
        ],
        out_specs=pl.BlockSpec((tm, D_MODEL), row),
        out_shape=jax.ShapeDtypeStruct((m, D_MODEL), F32),
        compiler_params=_params("parallel"),
    )(h, mix, wo, gm, w1, w2, gp, wpg, p_all, wpp, gf)


def _cmul(ar, ai, br, bi):
    return ar * br - ai * bi, ar * bi + ai * br


def _cpow(ar, ai, n, shape):
    pr = jnp.ones(shape, F32)
    pi = jnp.zeros(shape, F32)
    br, bi = ar, ai
    for bit in range(5):
        tr, ti = _cmul(pr, pi, br, bi)
        take = ((n >> bit) & 1) == 1
        pr = jnp.where(take, tr, pr)
        pi = jnp.where(take, ti, pi)
        br, bi = _cmul(br, bi, br, bi)
    return pr, pi


def _s5_discretize(lr, li, ld):
    dt = jnp.exp(ld)
    mag = jnp.exp(lr * dt)
    ar = mag * jnp.cos(li * dt)
    ai = mag * jnp.sin(li * dt)
    den = lr * lr + li * li
    nr = ar - 1.0
    coef_re = (nr * lr + ai * li) / den
    coef_im = (ai * lr - nr * li) / den
    return ar, ai, coef_re, coef_im


def _s5_prep_group(lrc, lic, lrr, lir, ld, brt, bit, cr, ci, crt, cit, d):
    n, k = S5_STATE, S5_K
    ar, ai, cfr, cfi = _s5_discretize(lrc, lic, ld)
    bbr = cfr * brt - cfi * bit
    bbi = cfr * bit + cfi * brt
    lag = lax.broadcasted_iota(jnp.int32, (n, k), 1) >> 4
    pr, pi = _cpow(ar, ai, lag, (n, k))
    wr, wi = _cmul(pr, pi, bbr, bbi)
    hp = lax.Precision.HIGHEST
    kall = (jnp.dot(cr, wr, precision=hp, preferred_element_type=F32)
            - jnp.dot(ci, wi, precision=hp, preferred_element_type=F32))
    row = lax.broadcasted_iota(jnp.int32, (S5_GROUP, k), 0)
    col = lax.broadcasted_iota(jnp.int32, (S5_GROUP, k), 1)
    kall = kall + jnp.where(row == col, d, 0.0)
    pr, pi = _cpow(ar, ai, (S5_L - 1) - lag, (n, k))
    er, ei = _cmul(pr, pi, bbr, bbi)
    a2r, a2i = _cmul(ar, ai, ar, ai)
    a4r, a4i = _cmul(a2r, a2i, a2r, a2i)
    a8r, a8i = _cmul(a4r, a4i, a4r, a4i)
    a16r, a16i = _cmul(a8r, a8i, a8r, a8i)
    ar, ai, _, _ = _s5_discretize(lrr, lir, ld)
    step = (lax.broadcasted_iota(jnp.int32, (k, n), 0) >> 4) + 1
    pr, pi = _cpow(ar, ai, step, (k, n))
    fr = crt * pr - cit * pi
    fi = -(crt * pi + cit * pr)
    return kall, er, ei, a16r, a16i, fr, fi


def _s5_prep_kernel(lrc_ref, lic_ref, lrr_ref, lir_ref, ld_ref, brt_ref, bit_ref, cr_ref, ci_ref,
                    crt_ref, cit_ref, d_ref, kall_ref, e_ref, f_ref, al_ref):
    n = S5_STATE
    for gi in range(S5_PREP_GROUPS):
        kall, er, ei, alr, ali, fr, fi = _s5_prep_group(
            lrc_ref[gi], lic_ref[gi], lrr_ref[gi], lir_ref[gi], ld_ref[gi], brt_ref[gi],
            bit_ref[gi], cr_ref[gi], ci_ref[gi], crt_ref[gi], cit_ref[gi], d_ref[gi])
        kall_ref[gi] = kall
        e_ref[gi, 0:n, :] = er
        e_ref[gi, n:2 * n, :] = ei
        al_ref[gi, 0] = alr
        al_ref[gi, 1] = ali
        f_ref[gi, :, 0:n] = fr
        f_ref[gi, :, n:2 * n] = fi


def _s5_prep(lam_re, lam_im, log_dt, b_re, b_im, c_re, c_im, d_skip):
    n, k = S5_STATE, S5_K
    g = lam_re.shape[0] * S5_GROUPS
    gb = S5_PREP_GROUPS
    blk = lambda *shape: pl.BlockSpec((gb,) + shape, lambda i: (i,) + (0,) * len(shape))
    b_re, b_im = b_re.reshape(g, n, S5_GROUP), b_im.reshape(g, n, S5_GROUP)
    c_re, c_im = c_re.reshape(g, S5_GROUP, n), c_im.reshape(g, S5_GROUP, n)
    kall, e, f, al = pl.pallas_call(
        _s5_prep_kernel,
        grid=(g // gb,),
        in_specs=[blk(n, 1), blk(n, 1), blk(1, n), blk(1, n), blk(1, 1),
                  blk(n, k), blk(n, k), blk(S5_GROUP, n), blk(S5_GROUP, n),
                  blk(k, n), blk(k, n), blk(S5_GROUP, 1)],
        out_specs=[blk(S5_GROUP, k), blk(2 * n, k), blk(k, 2 * n), blk(2, n, 1)],
        out_shape=[jax.ShapeDtypeStruct((g, S5_GROUP, k), F32),
                   jax.ShapeDtypeStruct((g, 2 * n, k), F32),
                   jax.ShapeDtypeStruct((g, k, 2 * n), F32),
                   jax.ShapeDtypeStruct((g, 2, n, 1), F32)],
        compiler_params=_params("parallel"),
    )(lam_re.reshape(g, n, 1), lam_im.reshape(g, n, 1), lam_re.reshape(g, 1, n),
      lam_im.reshape(g, 1, n), log_dt.reshape(g, 1, 1),
      jnp.tile(b_re, (1, 1, S5_L)), jnp.tile(b_im, (1, 1, S5_L)), c_re, c_im,
      jnp.tile(c_re, (1, S5_L, 1)), jnp.tile(c_im, (1, S5_L, 1)),
      d_skip.reshape(g, S5_GROUP, 1))
    kk = kall.reshape(g, S5_GROUP, S5_L, S5_GROUP)
    t = jnp.arange(S5_L)
    lagm = t[:, None] - t[None, :]
    m = jnp.take(kk, jnp.clip(lagm, 0, S5_L - 1).reshape(-1), axis=2)
    m = m.reshape(g, S5_GROUP, S5_L, S5_L, S5_GROUP)
    m = jnp.where((lagm >= 0)[None, None, :, :, None], m, 0.0)
    m = m.transpose(0, 2, 1, 3, 4).reshape(g, k, k)
    al = al.reshape(g // 2, 2, 2, n).transpose(0, 2, 1, 3).reshape(g // 2, 2, 1, 2 * n)
    return m.astype(BF16), e.astype(BF16), f.astype(BF16), al


def _s5_in_kernel(h_ref, g_ref, w_ref, ut_ref, u_s, *, seq):
    for r in range(seq // ROW_TILE):
        rows = slice(r * ROW_TILE, (r + 1) * ROW_TILE)
        xn = _rms(h_ref[rows, :], g_ref[...]).astype(BF16)
        u = _dot(xn, w_ref[...])
        for lt in range(D_MODEL // LANES):
            u_s[lt, rows, :] = u[:, lt * LANES:(lt + 1) * LANES]
    nck = seq // S5_L
    gpl = LANES // S5_GROUP
    for j in range(S5_L):
        for lt in range(D_MODEL // LANES):
            xj = u_s[lt, pl.ds(j, nck, stride=S5_L), :]
            ut_ref[gpl * lt:gpl * (lt + 1), S5_GROUP * j:S5_GROUP * (j + 1), :] = (
                xj.T.reshape(gpl, S5_GROUP, nck).astype(BF16))


def _s5_core_kernel(ut_ref, m_ref, e_ref, f_ref, al_ref, yt_ref, sr_s, si_s, *, nck, batch):
    n = S5_STATE
    u = [ut_ref[g] for g in range(2)]
    s = [_dot(e_ref[g], u[g]) for g in range(2)]
    s_re = jnp.concatenate([s[0][0:n, :], s[1][0:n, :]], axis=0)
    s_im = jnp.concatenate([s[0][n:2 * n, :], s[1][n:2 * n, :]], axis=0)
    for b in range(batch):
        cols = slice(b * nck, (b + 1) * nck)
        sr_s[pl.ds(b, nck, stride=batch), :] = s_re[:, cols].T
        si_s[pl.ds(b, nck, stride=batch), :] = s_im[:, cols].T
    ar, ai = al_ref[0], al_ref[1]

    def step(k, prev):
        pr, pi = prev
        rows = pl.ds(pl.multiple_of(k * batch, batch), batch)
        cr, ci = sr_s[rows, :], si_s[rows, :]
        sr_s[rows, :] = pr
        si_s[rows, :] = pi
        return ar * pr - ai * pi + cr, ar * pi + ai * pr + ci

    zero = jnp.zeros((batch, 2 * n), F32)
    lax.fori_loop(0, nck, step, (zero, zero), unroll=8)
    p_re = jnp.concatenate([sr_s[pl.ds(b, nck, stride=batch), :].T for b in range(batch)], axis=1)
    p_im = jnp.concatenate([si_s[pl.ds(b, nck, stride=batch), :].T for b in range(batch)], axis=1)
    for g in range(2):
        sp = jnp.concatenate([p_re[g * n:(g + 1) * n, :], p_im[g * n:(g + 1) * n, :]], axis=0)
        yt_ref[g] = _dot(m_ref[g], u[g]) + _dot(f_ref[g], sp.astype(BF16))


def _s5_post_kernel(yt_ref, wg_ref, mix_ref, y_s, *, seq):
    nck = seq // S5_L
    gpl = LANES // S5_GROUP
    nlt = D_MODEL // LANES
    for t in range(S5_L):
        for lt in range(nlt):
            piece = yt_ref[gpl * lt:gpl * (lt + 1), S5_GROUP * t:S5_GROUP * (t + 1), :]
            y_s[lt, pl.ds(t, nck, stride=S5_L), :] = piece.reshape(LANES, nck).T
    for r in range(seq // ROW_TILE):
        rows = slice(r * ROW_TILE, (r + 1) * ROW_TILE)
        y = jnp.concatenate([y_s[lt, rows, :] for lt in range(nlt)], axis=1)
        a = jax.nn.gelu(y).astype(BF16)
        z = _dot(a, wg_ref[...])
        mix_ref[rows, :] = (z[:, :D_MODEL] * jax.nn.sigmoid(z[:, D_MODEL:])).astype(BF16)


def _s5_mixer(h, gm, layer, w_in, mats, w_glu, s5_layer, batch, seq):
    m_t, e_t, f_t, al = mats
    nck = seq // S5_L
    lanes = batch * nck
    assert nck % LANES == 0 and nck & (nck - 1) == 0
    ut = pl.pallas_call(
        functools.partial(_s5_in_kernel, seq=seq),
        grid=(batch,),
        in_specs=[pl.BlockSpec((seq, D_MODEL), lambda b: (b, 0)),
                  _layer((1, D_MODEL), layer), _layer(w_in.shape[1:], s5_layer)],
        out_specs=pl.BlockSpec((S5_GROUPS, S5_K, nck), lambda b: (0, 0, b)),
        out_shape=jax.ShapeDtypeStruct((S5_GROUPS, S5_K, lanes), BF16),
        scratch_shapes=[pltpu.VMEM((D_MODEL // LANES, seq, LANES), F32)],
        compiler_params=_params("parallel"),
    )(h, gm, w_in)
    p0 = s5_layer * (S5_GROUPS // 2)
    act = lambda *shape: pl.BlockSpec((2,) + shape, lambda g: (g,) + (0,) * len(shape))
    par = lambda *shape: pl.BlockSpec((2,) + shape, lambda g: (p0 + g,) + (0,) * len(shape))
    yt = pl.pallas_call(
        functools.partial(_s5_core_kernel, nck=nck, batch=batch),
        grid=(S5_GROUPS // 2,),
        in_specs=[act(S5_K, lanes), par(S5_K, S5_K), par(2 * S5_STATE, S5_K),
                  par(S5_K, 2 * S5_STATE),
                  pl.BlockSpec((None, 2, 1, 2 * S5_STATE), lambda g: (p0 + g, 0, 0, 0))],
        out_specs=act(S5_K, lanes),
        out_shape=jax.ShapeDtypeStruct((S5_GROUPS, S5_K, lanes), F32),
        scratch_shapes=[pltpu.VMEM((lanes, 2 * S5_STATE), F32),
                        pltpu.VMEM((lanes, 2 * S5_STATE), F32)],
        compiler_params=_params("parallel"),
    )(ut, m_t, e_t, f_t, al)
    return pl.pallas_call(
        functools.partial(_s5_post_kernel, seq=seq),
        grid=(batch,),
        in_specs=[pl.BlockSpec((S5_GROUPS, S5_K, nck), lambda b: (0, 0, b)),
                  _layer(w_glu.shape[1:], s5_layer)],
        out_specs=pl.BlockSpec((seq, D_MODEL), lambda b: (b, 0)),
        out_shape=jax.ShapeDtypeStruct((batch * seq, D_MODEL), BF16),
        scratch_shapes=[pltpu.VMEM((D_MODEL // LANES, seq, LANES), F32)],
        compiler_params=_params("parallel"),
    )(yt, w_glu)


def _fox_in_kernel(h_ref, g_ref, wqkv_ref, wf_ref, bf_ref, q_ref, k_ref, v_ref, qc_ref, kc_ref,
                   carry_s):
    tm = ROW_TILE

    @pl.when(pl.program_id(1) == 0)
    def _():
        carry_s[...] = jnp.zeros_like(carry_s)

    row = lax.broadcasted_iota(jnp.int32, (tm, tm), 0)
    col = lax.broadcasted_iota(jnp.int32, (tm, tm), 1)
    tri = (row >= col).astype(BF16)
    lane = lax.broadcasted_iota(jnp.int32, (tm, LANES), 1)
    grp = lane >> 4
    carry = carry_s[...]
    for t in range(TILES_PER_STEP):
        rows = slice(t * tm, (t + 1) * tm)
        xn = _rms(h_ref[rows, :], g_ref[...]).astype(BF16)
        proj = _dot(xn, wqkv_ref[...])
        log_f = _log_sigmoid(_dot(xn, wf_ref[...]) + bf_ref[...])
        hi, mid, lo = _split3(log_f)
        cum = _dot(tri, hi) + _dot(tri, mid) + _dot(tri, lo) + carry
        carry = cum[tm - 1:tm, :]
        hi, mid, lo = _split3(cum * LOG2E)
        hi, mid, lo = hi.astype(F32), mid.astype(F32), lo.astype(F32)
        qc = jnp.where(grp == 0, hi, jnp.where(grp == 1, mid, jnp.where(grp == 2, lo,
             jnp.where(grp < FOX_BIAS_GROUPS, 1.0, 0.0))))
        kc = jnp.where(grp < 3, 1.0, jnp.where(grp == 3, -hi, jnp.where(grp == 4, -mid,
             jnp.where(grp == 5, -lo, 0.0))))
        qc_ref[rows, :] = qc.astype(BF16)
        kc_ref[rows, :] = kc.astype(BF16)
        q_ref[rows, :] = (proj[:, 0:D_MODEL] * (FOX_HEAD_DIM ** -0.5 * LOG2E)).astype(BF16)
        k_ref[rows, :] = proj[:, D_MODEL:2 * D_MODEL].astype(BF16)
        for pair in range(FOX_HEADS // 2):
            vp = proj[:, 2 * D_MODEL + pair * LANES:2 * D_MODEL + (pair + 1) * LANES]
            v_ref[2 * pair, rows, :] = jnp.where(
                lane < FOX_HEAD_DIM, vp, jnp.where(lane == FOX_HEAD_DIM, 1.0, 0.0)).astype(BF16)
            v_ref[2 * pair + 1, rows, :] = jnp.where(
                lane >= FOX_HEAD_DIM, vp, jnp.where(lane == 0, 1.0, 0.0)).astype(BF16)
    carry_s[...] = carry


def _fox_flash_kernel(q_ref, qc_ref, k_ref, kc_ref, v_ref, o_ref, *, seq):
    t = FOX_TQ
    pair = pl.program_id(1)
    lane = lax.broadcasted_iota(jnp.int32, (t, LANES), 1)
    row = lax.broadcasted_iota(jnp.int32, (t, t), 0)
    col = lax.broadcasted_iota(jnp.int32, (t, t), 1)
    causal = row >= col
    own = (lane < FOX_HEAD_DIM, lane >= FOX_HEAD_DIM)
    mine = [((lane & (FOX_HEADS - 1)) == 2 * pair + half) & (lane < FOX_HEADS * FOX_BIAS_GROUPS)
            for half in range(2)]
    for i in range(seq // t):
        qrows = slice(i * t, (i + 1) * t)
        q = q_ref[qrows, :].astype(F32)
        qc = qc_ref[qrows, :].astype(F32)
        accs = []
        for half in range(2):
            qcat = jnp.concatenate([jnp.where(own[half], q, 0.0), jnp.where(mine[half], qc, 0.0)],
                                   axis=1).astype(BF16)
            m = jnp.full((t, 1), NEG_INF, F32)
            acc = jnp.zeros((t, LANES), F32)
            for j in range(i + 1):
                krows = slice(j * t, (j + 1) * t)
                kcat = jnp.concatenate([k_ref[krows, :], kc_ref[krows, :]], axis=1)
                s = _dot_nt(qcat, kcat)
                if j == i:
                    s = jnp.where(causal, s, NEG_INF)
                m_new = jnp.maximum(m, jnp.max(s, axis=-1, keepdims=True))
                p = jnp.exp2(s - m_new)
                acc = jnp.exp2(m - m_new) * acc + _dot(p.astype(BF16), v_ref[half, krows, :])
                m = m_new
            accs.append(acc)
        out_a = accs[0] / accs[0][:, FOX_HEAD_DIM:FOX_HEAD_DIM + 1]
        out_b = accs[1] / accs[1][:, 0:1]
        o_ref[qrows, :] = jnp.where(lane < FOX_HEAD_DIM, out_a, out_b).astype(BF16)


def _fox_mixer(h, gm, layer, wqkv, wf, bf, batch, seq):
    tm = ROW_TILE * TILES_PER_STEP
    nt = seq // tm
    m = batch * seq
    tok = lambda b, i: (b * nt + i, 0)
    q, k, v, qc, kc = pl.pallas_call(
        _fox_in_kernel,
        grid=(batch, nt),
        in_specs=[pl.BlockSpec((tm, D_MODEL), tok),
                  _layer((1, D_MODEL), layer), _resident(wqkv.shape), _resident(wf.shape),
                  _resident((1, LANES))],
        out_specs=[pl.BlockSpec((tm, D_MODEL), tok), pl.BlockSpec((tm, D_MODEL), tok),
                   pl.BlockSpec((None, FOX_HEADS, tm, LANES), lambda b, i: (b, 0, i, 0)),
                   pl.BlockSpec((tm, LANES), tok), pl.BlockSpec((tm, LANES), tok)],
        out_shape=[jax.ShapeDtypeStruct((m, D_MODEL), BF16),
                   jax.ShapeDtypeStruct((m, D_MODEL), BF16),
                   jax.ShapeDtypeStruct((batch, FOX_HEADS, seq, LANES), BF16),
                   jax.ShapeDtypeStruct((m, LANES), BF16),
                   jax.ShapeDtypeStruct((m, LANES), BF16)],
        scratch_shapes=[pltpu.VMEM((1, LANES), F32)],
        compiler_params=_params("parallel", "arbitrary"),
    )(h, gm, wqkv, wf, bf)
    slab = pl.BlockSpec((seq, LANES), lambda b, p: (b, p))
    bias = pl.BlockSpec((seq, LANES), lambda b, p: (b, 0))
    return pl.pallas_call(
        functools.partial(_fox_flash_kernel, seq=seq),
        grid=(batch, FOX_HEADS // 2),
        in_specs=[slab, bias, slab, bias,
                  pl.BlockSpec((None, 2, seq, LANES), lambda b, p: (b, p, 0, 0))],
        out_specs=slab,
        out_shape=jax.ShapeDtypeStruct((m, D_MODEL), BF16),
        compiler_params=_params("parallel", "parallel"),
    )(q, qc, k, kc, v)


def _gla_in_kernel(h_ref, g_ref, w_ref, wg2_ref, bg_ref, qd_ref, ki_ref, ku_ref, v_ref, rs_ref,
                   dec_ref):
    tm = ROW_TILE
    nch = tm // GLA_CHUNK
    tb = 256
    row = lax.broadcasted_iota(jnp.int32, (tb, tb), 0)
    col = lax.broadcasted_iota(jnp.int32, (tb, tb), 1)
    tri = (((row >> 6) == (col >> 6)) & (row >= col)).astype(BF16)
    crow = lax.broadcasted_iota(jnp.int32, (nch, tm), 0)
    ccol = lax.broadcasted_iota(jnp.int32, (nch, tm), 1)
    pick = (crow == (ccol >> 6)).astype(BF16)
    for t in range(TILES_PER_STEP):
        rows = slice(t * tm, (t + 1) * tm)
        xn = _rms(h_ref[rows, :], g_ref[...]).astype(BF16)
        proj = _dot(xn, w_ref[...])
        q = proj[:, 0:GLA_KD]
        k = proj[:, GLA_KD:2 * GLA_KD]
        v = proj[:, 2 * GLA_KD:2 * GLA_KD + GLA_VD]
        r = proj[:, 2 * GLA_KD + GLA_VD:2 * GLA_KD + 2 * GLA_VD]
        g_lr = proj[:, 2 * GLA_KD + 2 * GLA_VD:].astype(BF16)
        log_a = _log_sigmoid(_dot(g_lr, wg2_ref[...]) + bg_ref[...]) / GLA_GATE_NORM
        hi = log_a.astype(BF16)
        lo = (log_a - hi.astype(F32)).astype(BF16)
        bcum = jnp.concatenate(
            [_dot(tri, hi[r0:r0 + tb, :]) + _dot(tri, lo[r0:r0 + tb, :]) for r0 in range(0, tm, tb)],
            axis=0)
        b_last = _dot(pick, hi) + _dot(pick, lo)
        dec_ref[t * nch:(t + 1) * nch, :] = jnp.exp(b_last)
        b_last_rows = jnp.broadcast_to(
            b_last[:, None, :], (nch, GLA_CHUNK, GLA_KD)).reshape(tm, GLA_KD)
        qd_ref[rows, :] = (q * GLA_DK ** -0.5 * jnp.exp(bcum)).astype(BF16)
        ki_ref[rows, :] = (k * jnp.exp(-bcum)).astype(BF16)
        ku_ref[rows, :] = (k * jnp.exp(b_last_rows - bcum)).astype(BF16)
        v_ref[rows, :] = v.astype(BF16)
        rs_ref[rows, :] = (r * jax.nn.sigmoid(r)).astype(BF16)


def _gla_core_kernel(qd_ref, ki_ref, ku_ref, v_ref, rs_ref, dec_ref, gn_ref, o_ref, *, seq):
    c = GLA_CHUNK
    row = lax.broadcasted_iota(jnp.int32, (c, c), 0)
    col = lax.broadcasted_iota(jnp.int32, (c, c), 1)
    causal = row >= col
    gn = gn_ref[...]

    def chunk(i, states):
        rows = pl.ds(pl.multiple_of(i * c, c), c)
        dec = dec_ref[pl.ds(i, 1), :]
        new = []
        for hd in range(GLA_HEADS):
            kk = slice(hd * GLA_DK, (hd + 1) * GLA_DK)
            vv = slice(hd * GLA_DV, (hd + 1) * GLA_DV)
            qd, ki, ku, v = qd_ref[rows, kk], ki_ref[rows, kk], ku_ref[rows, kk], v_ref[rows, vv]
            att = jnp.where(causal, _dot_nt(qd, ki), 0.0)
            o = _dot(att.astype(BF16), v) + _dot_nt(qd, states[hd].astype(BF16))
            o = o * lax.rsqrt(jnp.mean(o * o, axis=-1, keepdims=True) + EPS)
            o_ref[rows, vv] = (o * gn[:, vv] * rs_ref[rows, vv].astype(F32)).astype(BF16)
            new.append(states[hd] * dec[:, kk] + _dot_tn(v, ku))
        return tuple(new)

    lax.fori_loop(0, seq // c, chunk, (jnp.zeros((GLA_DV, GLA_DK), F32),) * GLA_HEADS, unroll=2)


def _gla_mixer(h, gm, layer, w_cat, wg2, bg, gn, batch, seq):
    tm = ROW_TILE * TILES_PER_STEP
    m = batch * seq
    nch = tm // GLA_CHUNK
    row = lambda i: (i, 0)
    tok = lambda width, dtype: jax.ShapeDtypeStruct((m, width), dtype)
    qd, ki, ku, v, rs, dec = pl.pallas_call(
        _gla_in_kernel,
        grid=(m // tm,),
        in_specs=[pl.BlockSpec((tm, D_MODEL), row), _layer((1, D_MODEL), layer),
                  _resident(w_cat.shape), _resident(wg2.shape), _resident((1, GLA_KD))],
        out_specs=[pl.BlockSpec((tm, GLA_KD), row), pl.BlockSpec((tm, GLA_KD), row),
                   pl.BlockSpec((tm, GLA_KD), row), pl.BlockSpec((tm, GLA_VD), row),
                   pl.BlockSpec((tm, GLA_VD), row), pl.BlockSpec((nch, GLA_KD), row)],
        out_shape=[tok(GLA_KD, BF16), tok(GLA_KD, BF16), tok(GLA_KD, BF16), tok(GLA_VD, BF16),
                   tok(GLA_VD, BF16), jax.ShapeDtypeStruct((m // GLA_CHUNK, GLA_KD), F32)],
        compiler_params=_params("parallel"),
    )(h, gm, w_cat, wg2, bg)
    blk = lambda rows, width: pl.BlockSpec((rows, width), lambda b: (b, 0))
    return pl.pallas_call(
        functools.partial(_gla_core_kernel, seq=seq),
        grid=(batch,),
        in_specs=[blk(seq, GLA_KD), blk(seq, GLA_KD), blk(seq, GLA_KD), blk(seq, GLA_VD),
                  blk(seq, GLA_VD), blk(seq // GLA_CHUNK, GLA_KD), _resident((1, GLA_VD))],
        out_specs=blk(seq, GLA_VD),
        out_shape=jax.ShapeDtypeStruct((m, GLA_VD), BF16),
        compiler_params=_params("parallel"),
    )(qd, ki, ku, v, rs, dec, gn)


def _pad_lanes(w, width):
    return jnp.pad(w, ((0, 0), (0, width - w.shape[1])))


def kernel(x, p, norm_mix, norm_mlp, norm_ple, s5_w_in, s5_lam_re, s5_lam_im, s5_log_dt, s5_b_re,
           s5_b_im, s5_c_re, s5_c_im, s5_d, s5_w_glu, s5_w_out, fox_w_in, fox_b_f, fox_w_out,
           gla_w_in, gla_w_g2, gla_b_g, gla_norm, gla_w_out, mlp_w1, mlp_w2, ple_proj, ple_gate,
           final_norm):
    batch, seq, _ = x.shape
    depth = p.shape[0]
    m = batch * seq
    h = x.reshape(m, D_MODEL)
    p_all = p.reshape(depth, m, PLE_DIM)
    gf = final_norm.reshape(1, D_MODEL)
    gm_mix = norm_mix.reshape(depth, 1, D_MODEL)
    gm_mlp = norm_mlp.reshape(depth, 1, D_MODEL)
    gm_ple = norm_ple.reshape(depth, 1, D_MODEL)
    w1, w2 = _to_bf16(mlp_w1), _to_bf16(mlp_w2)
    wpg, wpp = _to_bf16(ple_gate), _to_bf16(ple_proj)
    wo_s5, wo_fox, wo_gla = _to_bf16(s5_w_out), _to_bf16(fox_w_out), _to_bf16(gla_w_out)
    s5_in, s5_glu = _to_bf16(s5_w_in), _to_bf16(s5_w_glu)
    s5_mats = _s5_prep(s5_lam_re, s5_lam_im, s5_log_dt, s5_b_re, s5_b_im, s5_c_re, s5_c_im, s5_d)
    for i in range(depth):
        mixer, j = i % 3, i // 3
        if mixer == 0:
            mix = _s5_mixer(h, gm_mix, i, s5_in, s5_mats, s5_glu, j, batch, seq)
            wo = wo_s5
        elif mixer == 1:
            w = fox_w_in[j]
            wqkv = w[:, :3 * D_MODEL].astype(BF16)
            wf = _pad_lanes(jnp.tile(w[:, 3 * D_MODEL:], (1, FOX_BIAS_GROUPS)), LANES).astype(BF16)
            bf = _pad_lanes(jnp.tile(fox_b_f[j].reshape(1, FOX_HEADS), (1, FOX_BIAS_GROUPS)), LANES)
            mix = _fox_mixer(h, gm_mix, i, wqkv, wf, bf, batch, seq)
            wo = wo_fox
        else:
            w_cat = _pad_lanes(gla_w_in[j], 2 * GLA_KD + 2 * GLA_VD + LANES).astype(BF16)
            wg2 = jnp.pad(gla_w_g2[j], ((0, LANES - GLA_RANK), (0, 0))).astype(BF16)
            mix = _gla_mixer(h, gm_mix, i, w_cat, wg2, gla_b_g[j].reshape(1, GLA_KD),
                             gla_norm[j].reshape(1, GLA_VD), batch, seq)
            wo = wo_gla
        h = _tail(h, mix, wo, j, gm_mlp, w1, w2, gm_ple, wpg, p_all, wpp, gf, layer=i,
                  final=(i == depth - 1))
    return h.reshape(batch, seq, D_MODEL)
```

```python
import functools

import jax
import jax.numpy as jnp
from jax import lax
from jax.experimental import pallas as pl
from jax.experimental.pallas import tpu as pltpu

F32 = jnp.float32
BF16 = jnp.bfloat16

D_MODEL = 1024
D_FF = 4 * D_MODEL
PLE_DIM = 256
EPS = 1e-6
NEG_INF = -1e30
LOG2E = 1.4426950408889634

V7X_VMEM_BYTES = 64 * 1024 * 1024
VMEM_LIMIT_BYTES = V7X_VMEM_BYTES - 8 * 1024 * 1024
LANES = 128

S5_GROUP = 16
S5_GROUPS = D_MODEL // S5_GROUP
S5_STATE = 64
S5_L = 16
S5_K = S5_L * S5_GROUP
S5_PREP_GROUPS = 16

FOX_HEADS = 16
FOX_HEAD_DIM = D_MODEL // FOX_HEADS
FOX_TQ = 512
FOX_BIAS_GROUPS = 6

GLA_HEADS = 4
GLA_KD = D_MODEL // 2
GLA_VD = D_MODEL
GLA_DK = GLA_KD // GLA_HEADS
GLA_DV = GLA_VD // GLA_HEADS
GLA_RANK = 16
GLA_GATE_NORM = 16.0
GLA_CHUNK = 64
GLA_BLOCK_CHUNKS = 4

ROW_TILE = 512
TILES_PER_STEP = 2
CAST_BLOCK_BYTES = 4 * 1024 * 1024


def _params(*sem):
    return pltpu.CompilerParams(dimension_semantics=sem, vmem_limit_bytes=VMEM_LIMIT_BYTES)


def _resident(shape):
    zeros = (0,) * len(shape)
    return pl.BlockSpec(shape, lambda *_: zeros, pipeline_mode=pl.Buffered(1))


def _layer(shape, layer):
    index = (layer,) + (0,) * len(shape)
    return pl.BlockSpec((None,) + tuple(shape), lambda *_: index, pipeline_mode=pl.Buffered(1))


def _cast_kernel(x_ref, o_ref):
    o_ref[...] = x_ref[...].astype(BF16)


def _to_bf16(w):
    layers, k, n = w.shape
    tk = max(8, min(k, CAST_BLOCK_BYTES // (4 * n)))
    assert k % tk == 0
    spec = pl.BlockSpec((None, tk, n), lambda l, i: (l, i, 0))
    return pl.pallas_call(
        _cast_kernel, grid=(layers, k // tk), in_specs=[spec], out_specs=spec,
        out_shape=jax.ShapeDtypeStruct(w.shape, BF16),
        compiler_params=_params("parallel", "parallel"),
    )(w)


def _rms(x, g):
    ms = jnp.mean(x * x, axis=-1, keepdims=True)
    return x * lax.rsqrt(ms + EPS) * g


def _log_sigmoid(x):
    return jnp.minimum(x, 0.0) - jnp.log1p(jnp.exp(-jnp.abs(x)))


def _dot(a, b):
    return jnp.dot(a, b, preferred_element_type=F32)


def _dot_nt(a, b):
    return lax.dot_general(a, b, (((1,), (1,)), ((), ())), preferred_element_type=F32)


def _dot_tn(a, b):
    return lax.dot_general(a, b, (((0,), (0,)), ((), ())), preferred_element_type=F32)


def _split3(x):
    hi = x.astype(BF16)
    r1 = x - hi.astype(F32)
    mid = r1.astype(BF16)
    lo = (r1 - mid.astype(F32)).astype(BF16)
    return hi, mid, lo


def _tail_kernel(h_ref, mix_ref, wo_ref, gm_ref, w1_ref, w2_ref, gp_ref, wpg_ref, p_ref,
                 wpp_ref, gf_ref, o_ref, *, final):
    h = h_ref[...] + _dot(mix_ref[...], wo_ref[...])
    xn = _rms(h, gm_ref[...]).astype(BF16)
    ff_chunk = D_FF // 4
    mlp = None
    for c in range(4):
        a = _dot(xn, w1_ref[:, c * ff_chunk:(c + 1) * ff_chunk])
        a = jnp.square(jnp.maximum(a, 0.0)).astype(BF16)
        part = _dot(a, w2_ref[c * ff_chunk:(c + 1) * ff_chunk, :])
        mlp = part if mlp is None else mlp + part
    h = h + mlp
    xg = _rms(h, gp_ref[...]).astype(BF16)
    gate = jax.nn.sigmoid(_dot(xg, wpg_ref[...]))
    pe = _dot(p_ref[...].astype(BF16), wpp_ref[...])
    h = h + pe * gate
    if final:
        h = _rms(h, gf_ref[...])
    o_ref[...] = h


def _tail(h, mix, wo, wo_layer, gm, w1, w2, gp, wpg, p_all, wpp, gf, layer, final):
    m = h.shape[0]
    tm = ROW_TILE
    row = lambda i: (i, 0)
    return pl.pallas_call(
        functools.partial(_tail_kernel, final=final),
        grid=(m // tm,),
        in_specs=[
            pl.BlockSpec((tm, D_MODEL), row),
            pl.BlockSpec((tm, mix.shape[1]), row),
            _layer(wo.shape[1:], wo_layer),
            _layer((1, D_MODEL), layer),
            _layer(w1.shape[1:], layer),
            _layer(w2.shape[1:], layer),
            _layer((1, D_MODEL), layer),
            _layer(wpg.shape[1:], layer),
            pl.BlockSpec((None, tm, PLE_DIM), lambda i: (layer, i, 0)),
            _layer(wpp.shape[1:], layer),
            _resident((1, D_MODEL)),
        ],
        out_specs=pl.BlockSpec((tm, D_MODEL), row),
        out_shape=jax.ShapeDtypeStruct((m, D_MODEL), F32),
        compiler_params=_params("parallel"),
    )(h, mix, wo, gm, w1, w2, gp, wpg, p_all, wpp, gf)


def _cmul(ar, ai, br, bi):
    return ar * br - ai * bi, ar * bi + ai * br


def _cpow(ar, ai, n, shape):
    pr = jnp.ones(shape, F32)
    pi = jnp.zeros(shape, F32)
    br, bi = ar, ai
    for bit in range(5):
        tr, ti = _cmul(pr, pi, br, bi)
        take = ((n >> bit) & 1) == 1
        pr = jnp.where(take, tr, pr)
        pi = jnp.where(take, ti, pi)
        br, bi = _cmul(br, bi, br, bi)
    return pr, pi


def _s5_discretize(lr, li, ld):
    dt = jnp.exp(ld)
    mag = jnp.exp(lr * dt)
    ar = mag * jnp.cos(li * dt)
    ai = mag * jnp.sin(li * dt)
    den = lr * lr + li * li
    nr = ar - 1.0
    coef_re = (nr * lr + ai * li) / den
    coef_im = (ai * lr - nr * li) / den
    return ar, ai, coef_re, coef_im


def _s5_prep_group(lrc, lic, lrr, lir, ld, brt, bit, cr, ci, crt, cit, d):
    n, k = S5_STATE, S5_K
    ar, ai, cfr, cfi = _s5_discretize(lrc, lic, ld)
    bbr = cfr * brt - cfi * bit
    bbi = cfr * bit + cfi * brt
    lag = lax.broadcasted_iota(jnp.int32, (n, k), 1) >> 4
    pr, pi = _cpow(ar, ai, lag, (n, k))
    wr, wi = _cmul(pr, pi, bbr, bbi)
    hp = lax.Precision.HIGHEST
    kall = (jnp.dot(cr, wr, precision=hp, preferred_element_type=F32)
            - jnp.dot(ci, wi, precision=hp, preferred_element_type=F32))
    row = lax.broadcasted_iota(jnp.int32, (S5_GROUP, k), 0)
    col = lax.broadcasted_iota(jnp.int32, (S5_GROUP, k), 1)
    kall = kall + jnp.where(row == col, d, 0.0)
    pr, pi = _cpow(ar, ai, (S5_L - 1) - lag, (n, k))
    er, ei = _cmul(pr, pi, bbr, bbi)
    a2r, a2i = _cmul(ar, ai, ar, ai)
    a4r, a4i = _cmul(a2r, a2i, a2r, a2i)
    a8r, a8i = _cmul(a4r, a4i, a4r, a4i)
    a16r, a16i = _cmul(a8r, a8i, a8r, a8i)
    ar, ai, _, _ = _s5_discretize(lrr, lir, ld)
    step = (lax.broadcasted_iota(jnp.int32, (k, n), 0) >> 4) + 1
    pr, pi = _cpow(ar, ai, step, (k, n))
    fr = crt * pr - cit * pi
    fi = -(crt * pi + cit * pr)
    return kall, er, ei, a16r, a16i, fr, fi


def _s5_prep_kernel(lrc_ref, lic_ref, lrr_ref, lir_ref, ld_ref, brt_ref, bit_ref, cr_ref, ci_ref,
                    crt_ref, cit_ref, d_ref, kall_ref, e_ref, f_ref, al_ref):
    n = S5_STATE
    for gi in range(S5_PREP_GROUPS):
        kall, er, ei, alr, ali, fr, fi = _s5_prep_group(
            lrc_ref[gi], lic_ref[gi], lrr_ref[gi], lir_ref[gi], ld_ref[gi], brt_ref[gi],
            bit_ref[gi], cr_ref[gi], ci_ref[gi], crt_ref[gi], cit_ref[gi], d_ref[gi])
        kall_ref[gi] = kall
        e_ref[gi, 0:n, :] = er
        e_ref[gi, n:2 * n, :] = ei
        al_ref[gi, 0] = alr
        al_ref[gi, 1] = ali
        f_ref[gi, :, 0:n] = fr
        f_ref[gi, :, n:2 * n] = fi


def _s5_prep(lam_re, lam_im, log_dt, b_re, b_im, c_re, c_im, d_skip):
    n, k = S5_STATE, S5_K
    g = lam_re.shape[0] * S5_GROUPS
    gb = S5_PREP_GROUPS
    blk = lambda *shape: pl.BlockSpec((gb,) + shape, lambda i: (i,) + (0,) * len(shape))
    b_re, b_im = b_re.reshape(g, n, S5_GROUP), b_im.reshape(g, n, S5_GROUP)
    c_re, c_im = c_re.reshape(g, S5_GROUP, n), c_im.reshape(g, S5_GROUP, n)
    kall, e, f, al = pl.pallas_call(
        _s5_prep_kernel,
        grid=(g // gb,),
        in_specs=[blk(n, 1), blk(n, 1), blk(1, n), blk(1, n), blk(1, 1),
                  blk(n, k), blk(n, k), blk(S5_GROUP, n), blk(S5_GROUP, n),
                  blk(k, n), blk(k, n), blk(S5_GROUP, 1)],
        out_specs=[blk(S5_GROUP, k), blk(2 * n, k), blk(k, 2 * n), blk(2, n, 1)],
        out_shape=[jax.ShapeDtypeStruct((g, S5_GROUP, k), F32),
                   jax.ShapeDtypeStruct((g, 2 * n, k), F32),
                   jax.ShapeDtypeStruct((g, k, 2 * n), F32),
                   jax.ShapeDtypeStruct((g, 2, n, 1), F32)],
        compiler_params=_params("parallel"),
    )(lam_re.reshape(g, n, 1), lam_im.reshape(g, n, 1), lam_re.reshape(g, 1, n),
      lam_im.reshape(g, 1, n), log_dt.reshape(g, 1, 1),
      jnp.tile(b_re, (1, 1, S5_L)), jnp.tile(b_im, (1, 1, S5_L)), c_re, c_im,
      jnp.tile(c_re, (1, S5_L, 1)), jnp.tile(c_im, (1, S5_L, 1)),
      d_skip.reshape(g, S5_GROUP, 1))
    kk = kall.reshape(g, S5_GROUP, S5_L, S5_GROUP)
    t = jnp.arange(S5_L)
    lagm = t[:, None] - t[None, :]
    m = jnp.take(kk, jnp.clip(lagm, 0, S5_L - 1).reshape(-1), axis=2)
    m = m.reshape(g, S5_GROUP, S5_L, S5_L, S5_GROUP)
    m = jnp.where((lagm >= 0)[None, None, :, :, None], m, 0.0)
    m = m.transpose(0, 2, 1, 3, 4).reshape(g, k, k)
    al = al.reshape(g // 2, 2, 2, n).transpose(0, 2, 1, 3).reshape(g // 2, 2, 1, 2 * n)
    return m.astype(BF16), e.astype(BF16), f.astype(BF16), al


def _s5_in_kernel(h_ref, g_ref, w_ref, ut_ref, u_s, *, seq):
    for r in range(seq // ROW_TILE):
        rows = slice(r * ROW_TILE, (r + 1) * ROW_TILE)
        xn = _rms(h_ref[rows, :], g_ref[...]).astype(BF16)
        u = _dot(xn, w_ref[...])
        for lt in range(D_MODEL // LANES):
            u_s[lt, rows, :] = u[:, lt * LANES:(lt + 1) * LANES]
    nck = seq // S5_L
    gpl = LANES // S5_GROUP
    for j in range(S5_L):
        for lt in range(D_MODEL // LANES):
            xj = u_s[lt, pl.ds(j, nck, stride=S5_L), :]
            ut_ref[gpl * lt:gpl * (lt + 1), S5_GROUP * j:S5_GROUP * (j + 1), :] = (
                xj.T.reshape(gpl, S5_GROUP, nck).astype(BF16))


def _s5_core_kernel(ut_ref, m_ref, e_ref, f_ref, al_ref, yt_ref, sr_s, si_s, *, nck, batch):
    n = S5_STATE
    u = [ut_ref[g] for g in range(2)]
    s = [_dot(e_ref[g], u[g]) for g in range(2)]
    s_re = jnp.concatenate([s[0][0:n, :], s[1][0:n, :]], axis=0)
    s_im = jnp.concatenate([s[0][n:2 * n, :], s[1][n:2 * n, :]], axis=0)
    for b in range(batch):
        cols = slice(b * nck, (b + 1) * nck)
        sr_s[pl.ds(b, nck, stride=batch), :] = s_re[:, cols].T
        si_s[pl.ds(b, nck, stride=batch), :] = s_im[:, cols].T
    ar, ai = al_ref[0], al_ref[1]

    def step(k, prev):
        pr, pi = prev
        rows = pl.ds(pl.multiple_of(k * batch, batch), batch)
        cr, ci = sr_s[rows, :], si_s[rows, :]
        sr_s[rows, :] = pr
        si_s[rows, :] = pi
        return ar * pr - ai * pi + cr, ar * pi + ai * pr + ci

    zero = jnp.zeros((batch, 2 * n), F32)
    lax.fori_loop(0, nck, step, (zero, zero), unroll=8)
    p_re = jnp.concatenate([sr_s[pl.ds(b, nck, stride=batch), :].T for b in range(batch)], axis=1)
    p_im = jnp.concatenate([si_s[pl.ds(b, nck, stride=batch), :].T for b in range(batch)], axis=1)
    for g in range(2):
        sp = jnp.concatenate([p_re[g * n:(g + 1) * n, :], p_im[g * n:(g + 1) * n, :]], axis=0)
        yt_ref[g] = _dot(m_ref[g], u[g]) + _dot(f_ref[g], sp.astype(BF16))


def _s5_post_kernel(yt_ref, wg_ref, mix_ref, y_s, *, seq):
    nck = seq // S5_L
    gpl = LANES // S5_GROUP
    nlt = D_MODEL // LANES
    for t in range(S5_L):
        for lt in range(nlt):
            piece = yt_ref[gpl * lt:gpl * (lt + 1), S5_GROUP * t:S5_GROUP * (t + 1), :]
            y_s[lt, pl.ds(t, nck, stride=S5_L), :] = piece.reshape(LANES, nck).T
    for r in range(seq // ROW_TILE):
        rows = slice(r * ROW_TILE, (r + 1) * ROW_TILE)
        y = jnp.concatenate([y_s[lt, rows, :] for lt in range(nlt)], axis=1)
        a = jax.nn.gelu(y).astype(BF16)
        z = _dot(a, wg_ref[...])
        mix_ref[rows, :] = (z[:, :D_MODEL] * jax.nn.sigmoid(z[:, D_MODEL:])).astype(BF16)


def _s5_mixer(h, gm, layer, w_in, mats, w_glu, s5_layer, batch, seq):
    m_t, e_t, f_t, al = mats
    nck = seq // S5_L
    lanes = batch * nck
    assert nck % LANES == 0 and nck & (nck - 1) == 0
    ut = pl.pallas_call(
        functools.partial(_s5_in_kernel, seq=seq),
        grid=(batch,),
        in_specs=[pl.BlockSpec((seq, D_MODEL), lambda b: (b, 0)),
                  _layer((1, D_MODEL), layer), _layer(w_in.shape[1:], s5_layer)],
        out_specs=pl.BlockSpec((S5_GROUPS, S5_K, nck), lambda b: (0, 0, b)),
        out_shape=jax.ShapeDtypeStruct((S5_GROUPS, S5_K, lanes), BF16),
        scratch_shapes=[pltpu.VMEM((D_MODEL // LANES, seq, LANES), F32)],
        compiler_params=_params("parallel"),
    )(h, gm, w_in)
    p0 = s5_layer * (S5_GROUPS // 2)
    act = lambda *shape: pl.BlockSpec((2,) + shape, lambda g: (g,) + (0,) * len(shape))
    par = lambda *shape: pl.BlockSpec((2,) + shape, lambda g: (p0 + g,) + (0,) * len(shape))
    yt = pl.pallas_call(
        functools.partial(_s5_core_kernel, nck=nck, batch=batch),
        grid=(S5_GROUPS // 2,),
        in_specs=[act(S5_K, lanes), par(S5_K, S5_K), par(2 * S5_STATE, S5_K),
                  par(S5_K, 2 * S5_STATE),
                  pl.BlockSpec((None, 2, 1, 2 * S5_STATE), lambda g: (p0 + g, 0, 0, 0))],
        out_specs=act(S5_K, lanes),
        out_shape=jax.ShapeDtypeStruct((S5_GROUPS, S5_K, lanes), F32),
        scratch_shapes=[pltpu.VMEM((lanes, 2 * S5_STATE), F32),
                        pltpu.VMEM((lanes, 2 * S5_STATE), F32)],
        compiler_params=_params("parallel"),
    )(ut, m_t, e_t, f_t, al)
    return pl.pallas_call(
        functools.partial(_s5_post_kernel, seq=seq),
        grid=(batch,),
        in_specs=[pl.BlockSpec((S5_GROUPS, S5_K, nck), lambda b: (0, 0, b)),
                  _layer(w_glu.shape[1:], s5_layer)],
        out_specs=pl.BlockSpec((seq, D_MODEL), lambda b: (b, 0)),
        out_shape=jax.ShapeDtypeStruct((batch * seq, D_MODEL), BF16),
        scratch_shapes=[pltpu.VMEM((D_MODEL // LANES, seq, LANES), F32)],
        compiler_params=_params("parallel"),
    )(yt, w_glu)


def _fox_in_kernel(h_ref, g_ref, wqkv_ref, wf_ref, bf_ref, q_ref, k_ref, v_ref, qc_ref, kc_ref,
                   carry_s):
    tm = ROW_TILE

    @pl.when(pl.program_id(1) == 0)
    def _():
        carry_s[...] = jnp.zeros_like(carry_s)

    row = lax.broadcasted_iota(jnp.int32, (tm, tm), 0)
    col = lax.broadcasted_iota(jnp.int32, (tm, tm), 1)
    tri = (row >= col).astype(BF16)
    lane = lax.broadcasted_iota(jnp.int32, (tm, LANES), 1)
    grp = lane >> 4
    carry = carry_s[...]
    for t in range(TILES_PER_STEP):
        rows = slice(t * tm, (t + 1) * tm)
        xn = _rms(h_ref[rows, :], g_ref[...]).astype(BF16)
        proj = _dot(xn, wqkv_ref[...])
        log_f = _log_sigmoid(_dot(xn, wf_ref[...]) + bf_ref[...])
        hi, mid, lo = _split3(log_f)
        cum = _dot(tri, hi) + _dot(tri, mid) + _dot(tri, lo) + carry
        carry = cum[tm - 1:tm, :]
        hi, mid, lo = _split3(cum * LOG2E)
        hi, mid, lo = hi.astype(F32), mid.astype(F32), lo.astype(F32)
        qc = jnp.where(grp == 0, hi, jnp.where(grp == 1, mid, jnp.where(grp == 2, lo,
             jnp.where(grp < FOX_BIAS_GROUPS, 1.0, 0.0))))
        kc = jnp.where(grp < 3, 1.0, jnp.where(grp == 3, -hi, jnp.where(grp == 4, -mid,
             jnp.where(grp == 5, -lo, 0.0))))
        qc_ref[rows, :] = qc.astype(BF16)
        kc_ref[rows, :] = kc.astype(BF16)
        q_ref[rows, :] = (proj[:, 0:D_MODEL] * (FOX_HEAD_DIM ** -0.5 * LOG2E)).astype(BF16)
        k_ref[rows, :] = proj[:, D_MODEL:2 * D_MODEL].astype(BF16)
        for pair in range(FOX_HEADS // 2):
            vp = proj[:, 2 * D_MODEL + pair * LANES:2 * D_MODEL + (pair + 1) * LANES]
            v_ref[2 * pair, rows, :] = jnp.where(
                lane < FOX_HEAD_DIM, vp, jnp.where(lane == FOX_HEAD_DIM, 1.0, 0.0)).astype(BF16)
            v_ref[2 * pair + 1, rows, :] = jnp.where(
                lane >= FOX_HEAD_DIM, vp, jnp.where(lane == 0, 1.0, 0.0)).astype(BF16)
    carry_s[...] = carry


def _fox_flash_kernel(q_ref, qc_ref, k_ref, kc_ref, v_ref, o_ref, *, seq):
    t = FOX_TQ
    pair = pl.program_id(1)
    lane = lax.broadcasted_iota(jnp.int32, (t, LANES), 1)
    row = lax.broadcasted_iota(jnp.int32, (t, t), 0)
    col = lax.broadcasted_iota(jnp.int32, (t, t), 1)
    causal = row >= col
    own = (lane < FOX_HEAD_DIM, lane >= FOX_HEAD_DIM)
    mine = [((lane & (FOX_HEADS - 1)) == 2 * pair + half) & (lane < FOX_HEADS * FOX_BIAS_GROUPS)
            for half in range(2)]
    causal2 = jnp.concatenate([causal, causal], axis=0)
    for i in range(seq // t):
        qrows = slice(i * t, (i + 1) * t)
        q = q_ref[qrows, :].astype(F32)
        qc = qc_ref[qrows, :].astype(F32)
        qcat = jnp.concatenate(
            [jnp.concatenate([jnp.where(own[half], q, 0.0), jnp.where(mine[half], qc, 0.0)], axis=1)
             for half in range(2)], axis=0).astype(BF16)
        m = jnp.full((2 * t, 1), NEG_INF, F32)
        acc = jnp.zeros((2 * t, LANES), F32)
        for j in range(i + 1):
            krows = slice(j * t, (j + 1) * t)
            kcat = jnp.concatenate([k_ref[krows, :], kc_ref[krows, :]], axis=1)
            vcat = jnp.concatenate([v_ref[0, krows, :], v_ref[1, krows, :]], axis=1)
            s = _dot_nt(qcat, kcat)
            if j == i:
                s = jnp.where(causal2, s, NEG_INF)
            m_new = jnp.maximum(m, jnp.max(s, axis=-1, keepdims=True))
            p = jnp.exp2(s - m_new)
            pv = _dot(p.astype(BF16), vcat)
            pv = jnp.concatenate([pv[0:t, 0:LANES], pv[t:2 * t, LANES:2 * LANES]], axis=0)
            acc = jnp.exp2(m - m_new) * acc + pv
            m = m_new
        out_a = acc[0:t] / acc[0:t, FOX_HEAD_DIM:FOX_HEAD_DIM + 1]
        out_b = acc[t:2 * t] / acc[t:2 * t, 0:1]
        o_ref[qrows, :] = jnp.where(lane < FOX_HEAD_DIM, out_a, out_b).astype(BF16)


def _fox_mixer(h, gm, layer, wqkv, wf, bf, batch, seq):
    tm = ROW_TILE * TILES_PER_STEP
    nt = seq // tm
    m = batch * seq
    tok = lambda b, i: (b * nt + i, 0)
    q, k, v, qc, kc = pl.pallas_call(
        _fox_in_kernel,
        grid=(batch, nt),
        in_specs=[pl.BlockSpec((tm, D_MODEL), tok),
                  _layer((1, D_MODEL), layer), _resident(wqkv.shape), _resident(wf.shape),
                  _resident((1, LANES))],
        out_specs=[pl.BlockSpec((tm, D_MODEL), tok), pl.BlockSpec((tm, D_MODEL), tok),
                   pl.BlockSpec((None, FOX_HEADS, tm, LANES), lambda b, i: (b, 0, i, 0)),
                   pl.BlockSpec((tm, LANES), tok), pl.BlockSpec((tm, LANES), tok)],
        out_shape=[jax.ShapeDtypeStruct((m, D_MODEL), BF16),
                   jax.ShapeDtypeStruct((m, D_MODEL), BF16),
                   jax.ShapeDtypeStruct((batch, FOX_HEADS, seq, LANES), BF16),
                   jax.ShapeDtypeStruct((m, LANES), BF16),
                   jax.ShapeDtypeStruct((m, LANES), BF16)],
        scratch_shapes=[pltpu.VMEM((1, LANES), F32)],
        compiler_params=_params("parallel", "arbitrary"),
    )(h, gm, wqkv, wf, bf)
    slab = pl.BlockSpec((seq, LANES), lambda b, p: (b, p))
    bias = pl.BlockSpec((seq, LANES), lambda b, p: (b, 0))
    return pl.pallas_call(
        functools.partial(_fox_flash_kernel, seq=seq),
        grid=(batch, FOX_HEADS // 2),
        in_specs=[slab, bias, slab, bias,
                  pl.BlockSpec((None, 2, seq, LANES), lambda b, p: (b, p, 0, 0))],
        out_specs=slab,
        out_shape=jax.ShapeDtypeStruct((m, D_MODEL), BF16),
        compiler_params=_params("parallel", "parallel"),
    )(q, qc, k, kc, v)


def _gla_in_kernel(h_ref, g_ref, w_ref, wg2_ref, bg_ref, qd_ref, ki_ref, ku_ref, v_ref, rs_ref,
                   dec_ref):
    tm = ROW_TILE
    nch = tm // GLA_CHUNK
    tb = 256
    row = lax.broadcasted_iota(jnp.int32, (tb, tb), 0)
    col = lax.broadcasted_iota(jnp.int32, (tb, tb), 1)
    tri = (((row >> 6) == (col >> 6)) & (row >= col)).astype(BF16)
    crow = lax.broadcasted_iota(jnp.int32, (nch, tm), 0)
    ccol = lax.broadcasted_iota(jnp.int32, (nch, tm), 1)
    pick = (crow == (ccol >> 6)).astype(BF16)
    for t in range(TILES_PER_STEP):
        rows = slice(t * tm, (t + 1) * tm)
        xn = _rms(h_ref[rows, :], g_ref[...]).astype(BF16)
        proj = _dot(xn, w_ref[...])
        q = proj[:, 0:GLA_KD]
        k = proj[:, GLA_KD:2 * GLA_KD]
        v = proj[:, 2 * GLA_KD:2 * GLA_KD + GLA_VD]
        r = proj[:, 2 * GLA_KD + GLA_VD:2 * GLA_KD + 2 * GLA_VD]
        g_lr = proj[:, 2 * GLA_KD + 2 * GLA_VD:].astype(BF16)
        log_a = _log_sigmoid(_dot(g_lr, wg2_ref[...]) + bg_ref[...]) / GLA_GATE_NORM
        hi = log_a.astype(BF16)
        lo = (log_a - hi.astype(F32)).astype(BF16)
        bcum = jnp.concatenate(
            [_dot(tri, hi[r0:r0 + tb, :]) + _dot(tri, lo[r0:r0 + tb, :]) for r0 in range(0, tm, tb)],
            axis=0)
        b_last = _dot(pick, hi) + _dot(pick, lo)
        dec_ref[t * nch:(t + 1) * nch, :] = jnp.exp(b_last)
        b_last_rows = jnp.broadcast_to(
            b_last[:, None, :], (nch, GLA_CHUNK, GLA_KD)).reshape(tm, GLA_KD)
        qd_ref[rows, :] = (q * GLA_DK ** -0.5 * jnp.exp(bcum)).astype(BF16)
        ki_ref[rows, :] = (k * jnp.exp(-bcum)).astype(BF16)
        ku_ref[rows, :] = (k * jnp.exp(b_last_rows - bcum)).astype(BF16)
        v_ref[rows, :] = v.astype(BF16)
        rs_ref[rows, :] = (r * jax.nn.sigmoid(r)).astype(BF16)


def _gla_core_kernel(qd_ref, ki_ref, ku_ref, v_ref, rs_ref, dec_ref, gn_ref, o_ref, *, seq):
    c = GLA_CHUNK
    nb = GLA_BLOCK_CHUNKS
    blk = nb * c
    row = lax.broadcasted_iota(jnp.int32, (blk, blk), 0)
    col = lax.broadcasted_iota(jnp.int32, (blk, blk), 1)
    causal = ((row >> 6) == (col >> 6)) & (row >= col)
    gn = gn_ref[...]

    def block(rows, dec, states):
        new = []
        for hd in range(GLA_HEADS):
            kk = slice(hd * GLA_DK, (hd + 1) * GLA_DK)
            vv = slice(hd * GLA_DV, (hd + 1) * GLA_DV)
            qd, ki, ku, v = qd_ref[rows, kk], ki_ref[rows, kk], ku_ref[rows, kk], v_ref[rows, vv]
            att = jnp.where(causal, _dot_nt(qd, ki), 0.0)
            o = _dot(att.astype(BF16), v)
            st = states[hd]
            inter = []
            for j in range(nb):
                cr = slice(j * c, (j + 1) * c)
                inter.append(_dot_nt(qd[cr], st.astype(BF16)))
                st = st * dec[j:j + 1, kk] + _dot_tn(v[cr], ku[cr])
            o = o + jnp.concatenate(inter, axis=0)
            o = o * lax.rsqrt(jnp.mean(o * o, axis=-1, keepdims=True) + EPS)
            o_ref[rows, vv] = (o * gn[:, vv] * rs_ref[rows, vv].astype(F32)).astype(BF16)
            new.append(st)
        return tuple(new)

    def two_blocks(i, states):
        dec = dec_ref[pl.ds(pl.multiple_of(i * 2 * nb, 2 * nb), 2 * nb), :]
        for sub in range(2):
            rows = pl.ds(pl.multiple_of(i * 2 * blk + sub * blk, blk), blk)
            states = block(rows, dec[sub * nb:(sub + 1) * nb], states)
        return states

    lax.fori_loop(0, seq // (2 * blk), two_blocks,
                  (jnp.zeros((GLA_DV, GLA_DK), F32),) * GLA_HEADS)


def _gla_mixer(h, gm, layer, w_cat, wg2, bg, gn, batch, seq):
    tm = ROW_TILE * TILES_PER_STEP
    m = batch * seq
    nch = tm // GLA_CHUNK
    row = lambda i: (i, 0)
    tok = lambda width, dtype: jax.ShapeDtypeStruct((m, width), dtype)
    qd, ki, ku, v, rs, dec = pl.pallas_call(
        _gla_in_kernel,
        grid=(m // tm,),
        in_specs=[pl.BlockSpec((tm, D_MODEL), row), _layer((1, D_MODEL), layer),
                  _resident(w_cat.shape), _resident(wg2.shape), _resident((1, GLA_KD))],
        out_specs=[pl.BlockSpec((tm, GLA_KD), row), pl.BlockSpec((tm, GLA_KD), row),
                   pl.BlockSpec((tm, GLA_KD), row), pl.BlockSpec((tm, GLA_VD), row),
                   pl.BlockSpec((tm, GLA_VD), row), pl.BlockSpec((nch, GLA_KD), row)],
        out_shape=[tok(GLA_KD, BF16), tok(GLA_KD, BF16), tok(GLA_KD, BF16), tok(GLA_VD, BF16),
                   tok(GLA_VD, BF16), jax.ShapeDtypeStruct((m // GLA_CHUNK, GLA_KD), F32)],
        compiler_params=_params("parallel"),
    )(h, gm, w_cat, wg2, bg)
    blk = lambda rows, width: pl.BlockSpec((rows, width), lambda b: (b, 0))
    return pl.pallas_call(
        functools.partial(_gla_core_kernel, seq=seq),
        grid=(batch,),
        in_specs=[blk(seq, GLA_KD), blk(seq, GLA_KD), blk(seq, GLA_KD), blk(seq, GLA_VD),
                  blk(seq, GLA_VD), blk(seq // GLA_CHUNK, GLA_KD), _resident((1, GLA_VD))],
        out_specs=blk(seq, GLA_VD),
        out_shape=jax.ShapeDtypeStruct((m, GLA_VD), BF16),
        compiler_params=_params("parallel"),
    )(qd, ki, ku, v, rs, dec, gn)


def _pad_lanes(w, width):
    return jnp.pad(w, ((0, 0), (0, width - w.shape[1])))


def kernel(x, p, norm_mix, norm_mlp, norm_ple, s5_w_in, s5_lam_re, s5_lam_im, s5_log_dt, s5_b_re,
           s5_b_im, s5_c_re, s5_c_im, s5_d, s5_w_glu, s5_w_out, fox_w_in, fox_b_f, fox_w_out,
           gla_w_in, gla_w_g2, gla_b_g, gla_norm, gla_w_out, mlp_w1, mlp_w2, ple_proj, ple_gate,
           final_norm):
    batch, seq, _ = x.shape
    depth = p.shape[0]
    m = batch * seq
    h = x.reshape(m, D_MODEL)
    p_all = p.reshape(depth, m, PLE_DIM)
    gf = final_norm.reshape(1, D_MODEL)
    gm_mix = norm_mix.reshape(depth, 1, D_MODEL)
    gm_mlp = norm_mlp.reshape(depth, 1, D_MODEL)
    gm_ple = norm_ple.reshape(depth, 1, D_MODEL)
    w1, w2 = _to_bf16(mlp_w1), _to_bf16(mlp_w2)
    wpg, wpp = _to_bf16(ple_gate), _to_bf16(ple_proj)
    wo_s5, wo_fox, wo_gla = _to_bf16(s5_w_out), _to_bf16(fox_w_out), _to_bf16(gla_w_out)
    s5_in, s5_glu = _to_bf16(s5_w_in), _to_bf16(s5_w_glu)
    s5_mats = _s5_prep(s5_lam_re, s5_lam_im, s5_log_dt, s5_b_re, s5_b_im, s5_c_re, s5_c_im, s5_d)
    for i in range(depth):
        mixer, j = i % 3, i // 3
        if mixer == 0:
            mix = _s5_mixer(h, gm_mix, i, s5_in, s5_mats, s5_glu, j, batch, seq)
            wo = wo_s5
        elif mixer == 1:
            w = fox_w_in[j]
            wqkv = w[:, :3 * D_MODEL].astype(BF16)
            wf = _pad_lanes(jnp.tile(w[:, 3 * D_MODEL:], (1, FOX_BIAS_GROUPS)), LANES).astype(BF16)
            bf = _pad_lanes(jnp.tile(fox_b_f[j].reshape(1, FOX_HEADS), (1, FOX_BIAS_GROUPS)), LANES)
            mix = _fox_mixer(h, gm_mix, i, wqkv, wf, bf, batch, seq)
            wo = wo_fox
        else:
            w_cat = _pad_lanes(gla_w_in[j], 2 * GLA_KD + 2 * GLA_VD + LANES).astype(BF16)
            wg2 = jnp.pad(gla_w_g2[j], ((0, LANES - GLA_RANK), (0, 0))).astype(BF16)
            mix = _gla_mixer(h, gm_mix, i, w_cat, wg2, gla_b_g[j].reshape(1, GLA_KD),
                             gla_norm[j].reshape(1, GLA_VD), batch, seq)
            wo = wo_gla
        h = _tail(h, mix, wo, j, gm_mlp, w1, w2, gm_ple, wpg, p_all, wpp, gf, layer=i,
                  final=(i == depth - 1))
    return h.reshape(batch, seq, D_MODEL)
```

```python
import functools

import jax
import jax.numpy as jnp
from jax import lax
from jax.experimental import pallas as pl
from jax.experimental.pallas import tpu as pltpu

F32 = jnp.float32
BF16 = jnp.bfloat16

D_MODEL = 1024
D_FF = 4 * D_MODEL
PLE_DIM = 256
EPS = 1e-6
NEG_INF = -1e30
LOG2E = 1.4426950408889634

V7X_VMEM_BYTES = 64 * 1024 * 1024
VMEM_LIMIT_BYTES = V7X_VMEM_BYTES - 8 * 1024 * 1024
LANES = 128

S5_GROUP = 16
S5_GROUPS = D_MODEL // S5_GROUP
S5_STATE = 64
S5_L = 16
S5_K = S5_L * S5_GROUP
S5_PREP_GROUPS = 16

FOX_HEADS = 16
FOX_HEAD_DIM = D_MODEL // FOX_HEADS
FOX_TQ = 512
FOX_BIAS_GROUPS = 6

GLA_HEADS = 4
GLA_KD = D_MODEL // 2
GLA_VD = D_MODEL
GLA_DK = GLA_KD // GLA_HEADS
GLA_DV = GLA_VD // GLA_HEADS
GLA_RANK = 16
GLA_GATE_NORM = 16.0
GLA_CHUNK = 64
GLA_BLOCK_CHUNKS = 4

ROW_TILE = 512
TILES_PER_STEP = 2
CAST_BLOCK_BYTES = 4 * 1024 * 1024


def _params(*sem):
    return pltpu.CompilerParams(dimension_semantics=sem, vmem_limit_bytes=VMEM_LIMIT_BYTES)


def _resident(shape):
    zeros = (0,) * len(shape)
    return pl.BlockSpec(shape, lambda *_: zeros, pipeline_mode=pl.Buffered(1))


def _layer(shape, layer):
    index = (layer,) + (0,) * len(shape)
    return pl.BlockSpec((None,) + tuple(shape), lambda *_: index, pipeline_mode=pl.Buffered(1))


def _cast_kernel(x_ref, o_ref):
    o_ref[...] = x_ref[:, 0:o_ref.shape[1]].astype(BF16)


def _to_bf16(w, cols=None):
    layers, k, n = w.shape
    cols = n if cols is None else cols
    tk = 8
    while tk * 2 <= k and tk * 2 * n * 4 <= CAST_BLOCK_BYTES:
        tk *= 2
    assert k % tk == 0
    return pl.pallas_call(
        _cast_kernel, grid=(layers, k // tk),
        in_specs=[pl.BlockSpec((None, tk, n), lambda l, i: (l, i, 0))],
        out_specs=pl.BlockSpec((None, tk, cols), lambda l, i: (l, i, 0)),
        out_shape=jax.ShapeDtypeStruct((layers, k, cols), BF16),
        compiler_params=_params("parallel", "parallel"),
    )(w)


def _rms(x, g):
    ms = jnp.mean(x * x, axis=-1, keepdims=True)
    return x * lax.rsqrt(ms + EPS) * g


def _log_sigmoid(x):
    return jnp.minimum(x, 0.0) - jnp.log1p(jnp.exp(-jnp.abs(x)))


def _dot(a, b):
    return jnp.dot(a, b, preferred_element_type=F32)


def _dot_nt(a, b):
    return lax.dot_general(a, b, (((1,), (1,)), ((), ())), preferred_element_type=F32)


def _dot_tn(a, b):
    return lax.dot_general(a, b, (((0,), (0,)), ((), ())), preferred_element_type=F32)


def _split3(x):
    hi = x.astype(BF16)
    r1 = x - hi.astype(F32)
    mid = r1.astype(BF16)
    lo = (r1 - mid.astype(F32)).astype(BF16)
    return hi, mid, lo


def _tail_kernel(h_ref, mix_ref, wo_ref, gm_ref, w1_ref, w2_ref, gp_ref, wpg_ref, p_ref,
                 wpp_ref, gf_ref, o_ref, *, final):
    h = h_ref[...] + _dot(mix_ref[...], wo_ref[...])
    xn = _rms(h, gm_ref[...]).astype(BF16)
    ff_chunk = D_FF // 4
    mlp = None
    for c in range(4):
        a = _dot(xn, w1_ref[:, c * ff_chunk:(c + 1) * ff_chunk])
        a = jnp.square(jnp.maximum(a, 0.0)).astype(BF16)
        part = _dot(a, w2_ref[c * ff_chunk:(c + 1) * ff_chunk, :])
        mlp = part if mlp is None else mlp + part
    h = h + mlp
    xg = _rms(h, gp_ref[...]).astype(BF16)
    gate = jax.nn.sigmoid(_dot(xg, wpg_ref[...]))
    pe = _dot(p_ref[...].astype(BF16), wpp_ref[...])
    h = h + pe * gate
    if final:
        h = _rms(h, gf_ref[...])
    o_ref[...] = h


def _tail(h, mix, wo, wo_layer, gm, w1, w2, gp, wpg, p_all, wpp, gf, layer, final):
    m = h.shape[0]
    tm = ROW_TILE
    row = lambda i: (i, 0)
    return pl.pallas_call(
        functools.partial(_tail_kernel, final=final),
        grid=(m // tm,),
        in_specs=[
            pl.BlockSpec((tm, D_MODEL), row),
            pl.BlockSpec((tm, mix.shape[1]), row),
            _layer(wo.shape[1:], wo_layer),
            _layer((1, D_MODEL), layer),
            _layer(w1.shape[1:], layer),
            _layer(w2.shape[1:], layer),
            _layer((1, D_MODEL), layer),
            _layer(wpg.shape[1:], layer),
            pl.BlockSpec((None, tm, PLE_DIM), lambda i: (layer, i, 0)),
            _layer(wpp.shape[1:], layer),
            _resident((1, D_MODEL)),
        ],
        out_specs=pl.BlockSpec((tm, D_MODEL), row),
        out_shape=jax.ShapeDtypeStruct((m, D_MODEL), F32),
        compiler_params=_params("parallel"),
    )(h, mix, wo, gm, w1, w2, gp, wpg, p_all, wpp, gf)


def _cmul(ar, ai, br, bi):
    return ar * br - ai * bi, ar * bi + ai * br


def _cpow_table(ar, ai, count):
    pr, pi = jnp.ones_like(ar), jnp.zeros_like(ar)
    table = [(pr, pi)]
    for _ in range(count):
        pr, pi = _cmul(pr, pi, ar, ai)
        table.append((pr, pi))
    return table


def _s5_discretize(lr, li, ld):
    dt = jnp.exp(ld)
    mag = jnp.exp(lr * dt)
    ar = mag * jnp.cos(li * dt)
    ai = mag * jnp.sin(li * dt)
    den = lr * lr + li * li
    nr = ar - 1.0
    coef_re = (nr * lr + ai * li) / den
    coef_im = (ai * lr - nr * li) / den
    return ar, ai, coef_re, coef_im


def _s5_prep_group(lrc, lic, lrr, lir, ld, b_re, b_im, cr, ci, d):
    n, k, c = S5_STATE, S5_K, S5_GROUP
    hp = lax.Precision.HIGHEST
    ar, ai, cfr, cfi = _s5_discretize(lrc, lic, ld)
    rep = (lax.broadcasted_iota(jnp.int32, (c, k), 0)
           == (lax.broadcasted_iota(jnp.int32, (c, k), 1) & (c - 1))).astype(F32)
    brt = jnp.dot(b_re, rep, precision=hp, preferred_element_type=F32)
    bit = jnp.dot(b_im, rep, precision=hp, preferred_element_type=F32)
    bbr = cfr * brt - cfi * bit
    bbi = cfr * bit + cfi * brt
    grp = lax.broadcasted_iota(jnp.int32, (n, k), 1) >> 4
    powers = _cpow_table(ar, ai, S5_L - 1)
    pr = jnp.broadcast_to(powers[0][0], (n, k))
    pi = jnp.broadcast_to(powers[0][1], (n, k))
    for i in range(S5_L - 1):
        pr = jnp.where(grp == i, powers[S5_L - 1 - i][0], pr)
        pi = jnp.where(grp == i, powers[S5_L - 1 - i][1], pi)
    er, ei = _cmul(pr, pi, bbr, bbi)
    krev = (jnp.dot(cr, er, precision=hp, preferred_element_type=F32)
            - jnp.dot(ci, ei, precision=hp, preferred_element_type=F32))
    row = lax.broadcasted_iota(jnp.int32, (c, k), 0)
    col = lax.broadcasted_iota(jnp.int32, (c, k), 1)
    krev = krev + jnp.where(col == row + (k - c), d, 0.0)
    m_rows = []
    for t in range(S5_L):
        width = c * (t + 1)
        shifted = krev if width == k else pltpu.roll(krev, width, axis=1)
        m_rows.append(jnp.where(col < width, shifted, 0.0))
    ar, ai, _, _ = _s5_discretize(lrr, lir, ld)
    powers = _cpow_table(ar, ai, S5_L)
    f_re = [cr * powers[t + 1][0] - ci * powers[t + 1][1] for t in range(S5_L)]
    f_im = [-(cr * powers[t + 1][1] + ci * powers[t + 1][0]) for t in range(S5_L)]
    return m_rows, er, ei, f_re, f_im, powers[S5_L]


def _s5_prep_kernel(lrc_ref, lic_ref, lrr_ref, lir_ref, ld_ref, br_ref, bi_ref, cr_ref, ci_ref,
                    d_ref, m_ref, e_ref, f_ref, al_ref):
    n, c = S5_STATE, S5_GROUP
    for gi in range(S5_PREP_GROUPS):
        m_rows, er, ei, f_re, f_im, (alr, ali) = _s5_prep_group(
            lrc_ref[gi], lic_ref[gi], lrr_ref[gi], lir_ref[gi], ld_ref[gi], br_ref[gi],
            bi_ref[gi], cr_ref[gi], ci_ref[gi], d_ref[gi])
        e_ref[gi, 0:n, :] = er.astype(BF16)
        e_ref[gi, n:2 * n, :] = ei.astype(BF16)
        al_ref[gi, 0] = alr
        al_ref[gi, 1] = ali
        for t in range(S5_L):
            m_ref[gi, c * t:c * (t + 1), :] = m_rows[t].astype(BF16)
            f_ref[gi, c * t:c * (t + 1), 0:n] = f_re[t].astype(BF16)
            f_ref[gi, c * t:c * (t + 1), n:2 * n] = f_im[t].astype(BF16)


def _s5_prep(lam_re, lam_im, log_dt, b_re, b_im, c_re, c_im, d_skip):
    n, k, c = S5_STATE, S5_K, S5_GROUP
    g = lam_re.shape[0] * S5_GROUPS
    gb = S5_PREP_GROUPS
    blk = lambda *shape: pl.BlockSpec((gb,) + shape, lambda i: (i,) + (0,) * len(shape))
    m, e, f, al = pl.pallas_call(
        _s5_prep_kernel,
        grid=(g // gb,),
        in_specs=[blk(n, 1), blk(n, 1), blk(1, n), blk(1, n), blk(1, 1),
                  blk(n, c), blk(n, c), blk(c, n), blk(c, n), blk(c, 1)],
        out_specs=[blk(k, k), blk(2 * n, k), blk(k, 2 * n), blk(2, 1, n)],
        out_shape=[jax.ShapeDtypeStruct((g, k, k), BF16),
                   jax.ShapeDtypeStruct((g, 2 * n, k), BF16),
                   jax.ShapeDtypeStruct((g, k, 2 * n), BF16),
                   jax.ShapeDtypeStruct((g, 2, 1, n), F32)],
        compiler_params=_params("parallel"),
    )(lam_re.reshape(g, n, 1), lam_im.reshape(g, n, 1), lam_re.reshape(g, 1, n),
      lam_im.reshape(g, 1, n), log_dt.reshape(g, 1, 1),
      b_re.reshape(g, n, c), b_im.reshape(g, n, c), c_re.reshape(g, c, n), c_im.reshape(g, c, n),
      d_skip.reshape(g, c, 1))
    al = al.reshape(g // 2, 2, 2, n).transpose(0, 2, 1, 3).reshape(g // 2, 2, 1, 2 * n)
    return m, e, f, al


def _s5_in_kernel(h_ref, g_ref, w_ref, ut_ref, u_s, *, seq):
    for r in range(seq // ROW_TILE):
        rows = slice(r * ROW_TILE, (r + 1) * ROW_TILE)
        xn = _rms(h_ref[rows, :], g_ref[...]).astype(BF16)
        u = _dot(xn, w_ref[...])
        for lt in range(D_MODEL // LANES):
            u_s[lt, rows, :] = u[:, lt * LANES:(lt + 1) * LANES]
    nck = seq // S5_L
    gpl = LANES // S5_GROUP
    for j in range(S5_L):
        for lt in range(D_MODEL // LANES):
            xj = u_s[lt, pl.ds(j, nck, stride=S5_L), :]
            ut_ref[gpl * lt:gpl * (lt + 1), S5_GROUP * j:S5_GROUP * (j + 1), :] = (
                xj.T.reshape(gpl, S5_GROUP, nck).astype(BF16))


def _s5_core_kernel(ut_ref, m_ref, e_ref, f_ref, al_ref, yt_ref, sr_s, si_s, *, nck, batch):
    n = S5_STATE
    u = [ut_ref[g] for g in range(2)]
    s = [_dot(e_ref[g], u[g]) for g in range(2)]
    s_re = jnp.concatenate([s[0][0:n, :], s[1][0:n, :]], axis=0)
    s_im = jnp.concatenate([s[0][n:2 * n, :], s[1][n:2 * n, :]], axis=0)
    for b in range(batch):
        cols = slice(b * nck, (b + 1) * nck)
        sr_s[pl.ds(b, nck, stride=batch), :] = s_re[:, cols].T
        si_s[pl.ds(b, nck, stride=batch), :] = s_im[:, cols].T
    ar, ai = al_ref[0], al_ref[1]

    def step(k, prev):
        pr, pi = prev
        rows = pl.ds(pl.multiple_of(k * batch, batch), batch)
        cr, ci = sr_s[rows, :], si_s[rows, :]
        sr_s[rows, :] = pr
        si_s[rows, :] = pi
        return ar * pr - ai * pi + cr, ar * pi + ai * pr + ci

    zero = jnp.zeros((batch, 2 * n), F32)
    lax.fori_loop(0, nck, step, (zero, zero), unroll=8)
    p_re = jnp.concatenate([sr_s[pl.ds(b, nck, stride=batch), :].T for b in range(batch)], axis=1)
    p_im = jnp.concatenate([si_s[pl.ds(b, nck, stride=batch), :].T for b in range(batch)], axis=1)
    for g in range(2):
        sp = jnp.concatenate([p_re[g * n:(g + 1) * n, :], p_im[g * n:(g + 1) * n, :]], axis=0)
        yt_ref[g] = _dot(m_ref[g], u[g]) + _dot(f_ref[g], sp.astype(BF16))


def _s5_post_kernel(yt_ref, wg_ref, mix_ref, y_s, *, seq):
    nck = seq // S5_L
    gpl = LANES // S5_GROUP
    nlt = D_MODEL // LANES
    for t in range(S5_L):
        for lt in range(nlt):
            piece = yt_ref[gpl * lt:gpl * (lt + 1), S5_GROUP * t:S5_GROUP * (t + 1), :]
            y_s[lt, pl.ds(t, nck, stride=S5_L), :] = piece.reshape(LANES, nck).T
    for r in range(seq // ROW_TILE):
        rows = slice(r * ROW_TILE, (r + 1) * ROW_TILE)
        y = jnp.concatenate([y_s[lt, rows, :] for lt in range(nlt)], axis=1)
        a = jax.nn.gelu(y).astype(BF16)
        z = _dot(a, wg_ref[...])
        mix_ref[rows, :] = (z[:, :D_MODEL] * jax.nn.sigmoid(z[:, D_MODEL:])).astype(BF16)


def _s5_mixer(h, gm, layer, w_in, mats, w_glu, s5_layer, batch, seq):
    m_t, e_t, f_t, al = mats
    nck = seq // S5_L
    lanes = batch * nck
    assert nck % LANES == 0 and nck & (nck - 1) == 0
    ut = pl.pallas_call(
        functools.partial(_s5_in_kernel, seq=seq),
        grid=(batch,),
        in_specs=[pl.BlockSpec((seq, D_MODEL), lambda b: (b, 0)),
                  _layer((1, D_MODEL), layer), _layer(w_in.shape[1:], s5_layer)],
        out_specs=pl.BlockSpec((S5_GROUPS, S5_K, nck), lambda b: (0, 0, b)),
        out_shape=jax.ShapeDtypeStruct((S5_GROUPS, S5_K, lanes), BF16),
        scratch_shapes=[pltpu.VMEM((D_MODEL // LANES, seq, LANES), F32)],
        compiler_params=_params("parallel"),
    )(h, gm, w_in)
    p0 = s5_layer * (S5_GROUPS // 2)
    act = lambda *shape: pl.BlockSpec((2,) + shape, lambda g: (g,) + (0,) * len(shape))
    par = lambda *shape: pl.BlockSpec((2,) + shape, lambda g: (p0 + g,) + (0,) * len(shape))
    yt = pl.pallas_call(
        functools.partial(_s5_core_kernel, nck=nck, batch=batch),
        grid=(S5_GROUPS // 2,),
        in_specs=[act(S5_K, lanes), par(S5_K, S5_K), par(2 * S5_STATE, S5_K),
                  par(S5_K, 2 * S5_STATE),
                  pl.BlockSpec((None, 2, 1, 2 * S5_STATE), lambda g: (p0 + g, 0, 0, 0))],
        out_specs=act(S5_K, lanes),
        out_shape=jax.ShapeDtypeStruct((S5_GROUPS, S5_K, lanes), F32),
        scratch_shapes=[pltpu.VMEM((lanes, 2 * S5_STATE), F32),
                        pltpu.VMEM((lanes, 2 * S5_STATE), F32)],
        compiler_params=_params("parallel"),
    )(ut, m_t, e_t, f_t, al)
    return pl.pallas_call(
        functools.partial(_s5_post_kernel, seq=seq),
        grid=(batch,),
        in_specs=[pl.BlockSpec((S5_GROUPS, S5_K, nck), lambda b: (0, 0, b)),
                  _layer(w_glu.shape[1:], s5_layer)],
        out_specs=pl.BlockSpec((seq, D_MODEL), lambda b: (b, 0)),
        out_shape=jax.ShapeDtypeStruct((batch * seq, D_MODEL), BF16),
        scratch_shapes=[pltpu.VMEM((D_MODEL // LANES, seq, LANES), F32)],
        compiler_params=_params("parallel"),
    )(yt, w_glu)


def _fox_in_kernel(h_ref, g_ref, wqkv_ref, wf_ref, bf_ref, q_ref, k_ref, v_ref, qc_ref, kc_ref,
                   carry_s):
    tm = ROW_TILE

    @pl.when(pl.program_id(1) == 0)
    def _():
        carry_s[...] = jnp.zeros_like(carry_s)

    row = lax.broadcasted_iota(jnp.int32, (tm, tm), 0)
    col = lax.broadcasted_iota(jnp.int32, (tm, tm), 1)
    tri = (row >= col).astype(BF16)
    lane = lax.broadcasted_iota(jnp.int32, (tm, LANES), 1)
    grp = lane >> 4
    carry = carry_s[...]
    for t in range(TILES_PER_STEP):
        rows = slice(t * tm, (t + 1) * tm)
        xn = _rms(h_ref[rows, :], g_ref[...]).astype(BF16)
        proj = _dot(xn, wqkv_ref[...])
        log_f = _log_sigmoid(_dot(xn, wf_ref[...]) + bf_ref[...])
        hi, mid, lo = _split3(log_f)
        cum = _dot(tri, hi) + _dot(tri, mid) + _dot(tri, lo) + carry
        carry = cum[tm - 1:tm, :]
        hi, mid, lo = _split3(cum * LOG2E)
        hi, mid, lo = hi.astype(F32), mid.astype(F32), lo.astype(F32)
        qc = jnp.where(grp == 0, hi, jnp.where(grp == 1, mid, jnp.where(grp == 2, lo,
             jnp.where(grp < FOX_BIAS_GROUPS, 1.0, 0.0))))
        kc = jnp.where(grp < 3, 1.0, jnp.where(grp == 3, -hi, jnp.where(grp == 4, -mid,
             jnp.where(grp == 5, -lo, 0.0))))
        qc_ref[rows, :] = qc.astype(BF16)
        kc_ref[rows, :] = kc.astype(BF16)
        q_ref[rows, :] = (proj[:, 0:D_MODEL] * (FOX_HEAD_DIM ** -0.5 * LOG2E)).astype(BF16)
        k_ref[rows, :] = proj[:, D_MODEL:2 * D_MODEL].astype(BF16)
        for pair in range(FOX_HEADS // 2):
            vp = proj[:, 2 * D_MODEL + pair * LANES:2 * D_MODEL + (pair + 1) * LANES]
            v_ref[2 * pair, rows, :] = jnp.where(
                lane < FOX_HEAD_DIM, vp, jnp.where(lane == FOX_HEAD_DIM, 1.0, 0.0)).astype(BF16)
            v_ref[2 * pair + 1, rows, :] = jnp.where(
                lane >= FOX_HEAD_DIM, vp, jnp.where(lane == 0, 1.0, 0.0)).astype(BF16)
    carry_s[...] = carry


def _fox_flash_kernel(q_ref, qc_ref, k_ref, kc_ref, v_ref, o_ref, *, seq):
    t = FOX_TQ
    pair = pl.program_id(1)
    lane = lax.broadcasted_iota(jnp.int32, (t, LANES), 1)
    row = lax.broadcasted_iota(jnp.int32, (t, t), 0)
    col = lax.broadcasted_iota(jnp.int32, (t, t), 1)
    causal = row >= col
    own = (lane < FOX_HEAD_DIM, lane >= FOX_HEAD_DIM)
    mine = [((lane & (FOX_HEADS - 1)) == 2 * pair + half) & (lane < FOX_HEADS * FOX_BIAS_GROUPS)
            for half in range(2)]
    causal2 = jnp.concatenate([causal, causal], axis=0)
    for i in range(seq // t):
        qrows = slice(i * t, (i + 1) * t)
        q = q_ref[qrows, :].astype(F32)
        qc = qc_ref[qrows, :].astype(F32)
        qcat = jnp.concatenate(
            [jnp.concatenate([jnp.where(own[half], q, 0.0), jnp.where(mine[half], qc, 0.0)], axis=1)
             for half in range(2)], axis=0).astype(BF16)
        m = jnp.full((2 * t, 1), NEG_INF, F32)
        acc = jnp.zeros((2 * t, LANES), F32)
        for j in range(i + 1):
            krows = slice(j * t, (j + 1) * t)
            kcat = jnp.concatenate([k_ref[krows, :], kc_ref[krows, :]], axis=1)
            vcat = jnp.concatenate([v_ref[0, krows, :], v_ref[1, krows, :]], axis=1)
            s = _dot_nt(qcat, kcat)
            if j == i:
                s = jnp.where(causal2, s, NEG_INF)
            m_new = jnp.maximum(m, jnp.max(s, axis=-1, keepdims=True))
            p = jnp.exp2(s - m_new)
            pv = _dot(p.astype(BF16), vcat)
            pv = jnp.concatenate([pv[0:t, 0:LANES], pv[t:2 * t, LANES:2 * LANES]], axis=0)
            acc = jnp.exp2(m - m_new) * acc + pv
            m = m_new
        out_a = acc[0:t] / acc[0:t, FOX_HEAD_DIM:FOX_HEAD_DIM + 1]
        out_b = acc[t:2 * t] / acc[t:2 * t, 0:1]
        o_ref[qrows, :] = jnp.where(lane < FOX_HEAD_DIM, out_a, out_b).astype(BF16)


def _fox_mixer(h, gm, layer, wqkv, fox_layer, wf, bf, batch, seq):
    tm = ROW_TILE * TILES_PER_STEP
    nt = seq // tm
    m = batch * seq
    tok = lambda b, i: (b * nt + i, 0)
    q, k, v, qc, kc = pl.pallas_call(
        _fox_in_kernel,
        grid=(batch, nt),
        in_specs=[pl.BlockSpec((tm, D_MODEL), tok),
                  _layer((1, D_MODEL), layer), _layer(wqkv.shape[1:], fox_layer),
                  _resident(wf.shape), _resident((1, LANES))],
        out_specs=[pl.BlockSpec((tm, D_MODEL), tok), pl.BlockSpec((tm, D_MODEL), tok),
                   pl.BlockSpec((None, FOX_HEADS, tm, LANES), lambda b, i: (b, 0, i, 0)),
                   pl.BlockSpec((tm, LANES), tok), pl.BlockSpec((tm, LANES), tok)],
        out_shape=[jax.ShapeDtypeStruct((m, D_MODEL), BF16),
                   jax.ShapeDtypeStruct((m, D_MODEL), BF16),
                   jax.ShapeDtypeStruct((batch, FOX_HEADS, seq, LANES), BF16),
                   jax.ShapeDtypeStruct((m, LANES), BF16),
                   jax.ShapeDtypeStruct((m, LANES), BF16)],
        scratch_shapes=[pltpu.VMEM((1, LANES), F32)],
        compiler_params=_params("parallel", "arbitrary"),
    )(h, gm, wqkv, wf, bf)
    slab = pl.BlockSpec((seq, LANES), lambda b, p: (b, p))
    bias = pl.BlockSpec((seq, LANES), lambda b, p: (b, 0))
    return pl.pallas_call(
        functools.partial(_fox_flash_kernel, seq=seq),
        grid=(batch, FOX_HEADS // 2),
        in_specs=[slab, bias, slab, bias,
                  pl.BlockSpec((None, 2, seq, LANES), lambda b, p: (b, p, 0, 0))],
        out_specs=slab,
        out_shape=jax.ShapeDtypeStruct((m, D_MODEL), BF16),
        compiler_params=_params("parallel", "parallel"),
    )(q, qc, k, kc, v)


def _gla_in_kernel(h_ref, g_ref, w_ref, wgl_ref, wg2_ref, bg_ref, qd_ref, ki_ref, ku_ref, v_ref,
                   rs_ref, dec_ref):
    tm = ROW_TILE
    nch = tm // GLA_CHUNK
    tb = 256
    row = lax.broadcasted_iota(jnp.int32, (tb, tb), 0)
    col = lax.broadcasted_iota(jnp.int32, (tb, tb), 1)
    tri = (((row >> 6) == (col >> 6)) & (row >= col)).astype(BF16)
    crow = lax.broadcasted_iota(jnp.int32, (nch, tm), 0)
    ccol = lax.broadcasted_iota(jnp.int32, (nch, tm), 1)
    pick = (crow == (ccol >> 6)).astype(BF16)
    for t in range(TILES_PER_STEP):
        rows = slice(t * tm, (t + 1) * tm)
        xn = _rms(h_ref[rows, :], g_ref[...]).astype(BF16)
        proj = _dot(xn, w_ref[...])
        q = proj[:, 0:GLA_KD]
        k = proj[:, GLA_KD:2 * GLA_KD]
        v = proj[:, 2 * GLA_KD:2 * GLA_KD + GLA_VD]
        r = proj[:, 2 * GLA_KD + GLA_VD:2 * GLA_KD + 2 * GLA_VD]
        g_lr = _dot(xn, wgl_ref[...]).astype(BF16)
        log_a = _log_sigmoid(_dot(g_lr, wg2_ref[...]) + bg_ref[...]) / GLA_GATE_NORM
        hi = log_a.astype(BF16)
        lo = (log_a - hi.astype(F32)).astype(BF16)
        bcum = jnp.concatenate(
            [_dot(tri, hi[r0:r0 + tb, :]) + _dot(tri, lo[r0:r0 + tb, :]) for r0 in range(0, tm, tb)],
            axis=0)
        b_last = _dot(pick, hi) + _dot(pick, lo)
        dec_ref[t * nch:(t + 1) * nch, :] = jnp.exp(b_last)
        b_last_rows = jnp.broadcast_to(
            b_last[:, None, :], (nch, GLA_CHUNK, GLA_KD)).reshape(tm, GLA_KD)
        qd_ref[rows, :] = (q * GLA_DK ** -0.5 * jnp.exp(bcum)).astype(BF16)
        ki_ref[rows, :] = (k * jnp.exp(-bcum)).astype(BF16)
        ku_ref[rows, :] = (k * jnp.exp(b_last_rows - bcum)).astype(BF16)
        v_ref[rows, :] = v.astype(BF16)
        rs_ref[rows, :] = (r * jax.nn.sigmoid(r)).astype(BF16)


def _gla_core_kernel(qd_ref, ki_ref, ku_ref, v_ref, rs_ref, dec_ref, gn_ref, o_ref, *, seq):
    c = GLA_CHUNK
    nb = GLA_BLOCK_CHUNKS
    blk = nb * c
    row = lax.broadcasted_iota(jnp.int32, (blk, blk), 0)
    col = lax.broadcasted_iota(jnp.int32, (blk, blk), 1)
    causal = ((row >> 6) == (col >> 6)) & (row >= col)
    gn = gn_ref[...]

    def block(rows, dec, states):
        new = []
        for hd in range(GLA_HEADS):
            kk = slice(hd * GLA_DK, (hd + 1) * GLA_DK)
            vv = slice(hd * GLA_DV, (hd + 1) * GLA_DV)
            qd, ki, ku, v = qd_ref[rows, kk], ki_ref[rows, kk], ku_ref[rows, kk], v_ref[rows, vv]
            att = jnp.where(causal, _dot_nt(qd, ki), 0.0)
            o = _dot(att.astype(BF16), v)
            st = states[hd]
            inter = []
            for j in range(nb):
                cr = slice(j * c, (j + 1) * c)
                inter.append(_dot_nt(qd[cr], st.astype(BF16)))
                st = st * dec[j:j + 1, kk] + _dot_tn(v[cr], ku[cr])
            o = o + jnp.concatenate(inter, axis=0)
            o = o * lax.rsqrt(jnp.mean(o * o, axis=-1, keepdims=True) + EPS)
            o_ref[rows, vv] = (o * gn[:, vv] * rs_ref[rows, vv].astype(F32)).astype(BF16)
            new.append(st)
        return tuple(new)

    def two_blocks(i, states):
        dec = dec_ref[pl.ds(pl.multiple_of(i * 2 * nb, 2 * nb), 2 * nb), :]
        for sub in range(2):
            rows = pl.ds(pl.multiple_of(i * 2 * blk + sub * blk, blk), blk)
            states = block(rows, dec[sub * nb:(sub + 1) * nb], states)
        return states

    lax.fori_loop(0, seq // (2 * blk), two_blocks,
                  (jnp.zeros((GLA_DV, GLA_DK), F32),) * GLA_HEADS)


def _gla_mixer(h, gm, layer, w_main, gla_layer, wgl, wg2, bg, gn, batch, seq):
    tm = ROW_TILE * TILES_PER_STEP
    m = batch * seq
    nch = tm // GLA_CHUNK
    row = lambda i: (i, 0)
    tok = lambda width, dtype: jax.ShapeDtypeStruct((m, width), dtype)
    qd, ki, ku, v, rs, dec = pl.pallas_call(
        _gla_in_kernel,
        grid=(m // tm,),
        in_specs=[pl.BlockSpec((tm, D_MODEL), row), _layer((1, D_MODEL), layer),
                  _layer(w_main.shape[1:], gla_layer), _resident(wgl.shape), _resident(wg2.shape),
                  _resident((1, GLA_KD))],
        out_specs=[pl.BlockSpec((tm, GLA_KD), row), pl.BlockSpec((tm, GLA_KD), row),
                   pl.BlockSpec((tm, GLA_KD), row), pl.BlockSpec((tm, GLA_VD), row),
                   pl.BlockSpec((tm, GLA_VD), row), pl.BlockSpec((nch, GLA_KD), row)],
        out_shape=[tok(GLA_KD, BF16), tok(GLA_KD, BF16), tok(GLA_KD, BF16), tok(GLA_VD, BF16),
                   tok(GLA_VD, BF16), jax.ShapeDtypeStruct((m // GLA_CHUNK, GLA_KD), F32)],
        compiler_params=_params("parallel"),
    )(h, gm, w_main, wgl, wg2, bg)
    blk = lambda rows, width: pl.BlockSpec((rows, width), lambda b: (b, 0))
    return pl.pallas_call(
        functools.partial(_gla_core_kernel, seq=seq),
        grid=(batch,),
        in_specs=[blk(seq, GLA_KD), blk(seq, GLA_KD), blk(seq, GLA_KD), blk(seq, GLA_VD),
                  blk(seq, GLA_VD), blk(seq // GLA_CHUNK, GLA_KD), _resident((1, GLA_VD))],
        out_specs=blk(seq, GLA_VD),
        out_shape=jax.ShapeDtypeStruct((m, GLA_VD), BF16),
        compiler_params=_params("parallel"),
    )(qd, ki, ku, v, rs, dec, gn)


def _pad_lanes(w, width):
    return jnp.pad(w, ((0, 0), (0, width - w.shape[1])))


def kernel(x, p, norm_mix, norm_mlp, norm_ple, s5_w_in, s5_lam_re, s5_lam_im, s5_log_dt, s5_b_re,
           s5_b_im, s5_c_re, s5_c_im, s5_d, s5_w_glu, s5_w_out, fox_w_in, fox_b_f, fox_w_out,
           gla_w_in, gla_w_g2, gla_b_g, gla_norm, gla_w_out, mlp_w1, mlp_w2, ple_proj, ple_gate,
           final_norm):
    batch, seq, _ = x.shape
    depth = p.shape[0]
    m = batch * seq
    h = x.reshape(m, D_MODEL)
    p_all = p.reshape(depth, m, PLE_DIM)
    gf = final_norm.reshape(1, D_MODEL)
    gm_mix = norm_mix.reshape(depth, 1, D_MODEL)
    gm_mlp = norm_mlp.reshape(depth, 1, D_MODEL)
    gm_ple = norm_ple.reshape(depth, 1, D_MODEL)
    w1, w2 = _to_bf16(mlp_w1), _to_bf16(mlp_w2)
    wpg, wpp = _to_bf16(ple_gate), _to_bf16(ple_proj)
    wo_s5, wo_fox, wo_gla = _to_bf16(s5_w_out), _to_bf16(fox_w_out), _to_bf16(gla_w_out)
    s5_in, s5_glu = _to_bf16(s5_w_in), _to_bf16(s5_w_glu)
    fox_qkv = _to_bf16(fox_w_in, cols=3 * D_MODEL)
    gla_main = _to_bf16(gla_w_in, cols=2 * GLA_KD + 2 * GLA_VD)
    s5_mats = _s5_prep(s5_lam_re, s5_lam_im, s5_log_dt, s5_b_re, s5_b_im, s5_c_re, s5_c_im, s5_d)
    for i in range(depth):
        mixer, j = i % 3, i // 3
        if mixer == 0:
            mix = _s5_mixer(h, gm_mix, i, s5_in, s5_mats, s5_glu, j, batch, seq)
            wo = wo_s5
        elif mixer == 1:
            w_f = fox_w_in[j, :, 3 * D_MODEL:]
            wf = _pad_lanes(jnp.tile(w_f, (1, FOX_BIAS_GROUPS)), LANES).astype(BF16)
            bf = _pad_lanes(jnp.tile(fox_b_f[j].reshape(1, FOX_HEADS), (1, FOX_BIAS_GROUPS)), LANES)
            mix = _fox_mixer(h, gm_mix, i, fox_qkv, j, wf, bf, batch, seq)
            wo = wo_fox
        else:
            wgl = _pad_lanes(gla_w_in[j, :, 2 * GLA_KD + 2 * GLA_VD:], LANES).astype(BF16)
            wg2 = jnp.pad(gla_w_g2[j], ((0, LANES - GLA_RANK), (0, 0))).astype(BF16)
            mix = _gla_mixer(h, gm_mix, i, gla_main, j, wgl, wg2, gla_b_g[j].reshape(1, GLA_KD),
                             gla_norm[j].reshape(1, GLA_VD), batch, seq)
            wo = wo_gla
        h = _tail(h, mix, wo, j, gm_mlp, w1, w2, gm_ple, wpg, p_all, wpp, gf, layer=i,
                  final=(i == depth - 1))
    return h.reshape(batch, seq, D_MODEL)
```

```python
import functools

import jax
import jax.numpy as jnp
from jax import lax
from jax.experimental import pallas as pl
from jax.experimental.pallas import tpu as pltpu

F32 = jnp.float32
BF16 = jnp.bfloat16

D_MODEL = 1024
D_FF = 4 * D_MODEL
PLE_DIM = 256
EPS = 1e-6
NEG_INF = -1e30
LOG2E = 1.4426950408889634

V7X_VMEM_BYTES = 64 * 1024 * 1024
VMEM_LIMIT_BYTES = V7X_VMEM_BYTES - 8 * 1024 * 1024
LANES = 128

S5_GROUP = 16
S5_GROUPS = D_MODEL // S5_GROUP
S5_STATE = 64
S5_L = 16
S5_K = S5_L * S5_GROUP
S5_PREP_GROUPS = 16

FOX_HEADS = 16
FOX_HEAD_DIM = D_MODEL // FOX_HEADS
FOX_TQ = 512
FOX_BIAS_GROUPS = 6

GLA_HEADS = 4
GLA_KD = D_MODEL // 2
GLA_VD = D_MODEL
GLA_DK = GLA_KD // GLA_HEADS
GLA_DV = GLA_VD // GLA_HEADS
GLA_RANK = 16
GLA_GATE_NORM = 16.0
GLA_CHUNK = 64
GLA_BLOCK_CHUNKS = 4

ROW_TILE = 512
TILES_PER_STEP = 2
CAST_BLOCK_BYTES = 4 * 1024 * 1024


def _params(*sem):
    return pltpu.CompilerParams(dimension_semantics=sem, vmem_limit_bytes=VMEM_LIMIT_BYTES)


def _resident(shape):
    zeros = (0,) * len(shape)
    return pl.BlockSpec(shape, lambda *_: zeros, pipeline_mode=pl.Buffered(1))


def _layer(shape, layer):
    index = (layer,) + (0,) * len(shape)
    return pl.BlockSpec((None,) + tuple(shape), lambda *_: index, pipeline_mode=pl.Buffered(1))


def _cast_kernel(x_ref, o_ref):
    o_ref[...] = x_ref[:, 0:o_ref.shape[1]].astype(BF16)


def _to_bf16(w, cols=None):
    layers, k, n = w.shape
    cols = n if cols is None else cols
    tk = 8
    while tk * 2 <= k and tk * 2 * n * 4 <= CAST_BLOCK_BYTES:
        tk *= 2
    assert k % tk == 0
    return pl.pallas_call(
        _cast_kernel, grid=(layers, k // tk),
        in_specs=[pl.BlockSpec((None, tk, n), lambda l, i: (l, i, 0))],
        out_specs=pl.BlockSpec((None, tk, cols), lambda l, i: (l, i, 0)),
        out_shape=jax.ShapeDtypeStruct((layers, k, cols), BF16),
        compiler_params=_params("parallel", "parallel"),
    )(w)


def _rms(x, g):
    ms = jnp.mean(x * x, axis=-1, keepdims=True)
    return x * lax.rsqrt(ms + EPS) * g


def _log_sigmoid(x):
    return jnp.minimum(x, 0.0) - jnp.log1p(jnp.exp(-jnp.abs(x)))


def _dot(a, b):
    return jnp.dot(a, b, preferred_element_type=F32)


def _dot_nt(a, b):
    return lax.dot_general(a, b, (((1,), (1,)), ((), ())), preferred_element_type=F32)


def _dot_tn(a, b):
    return lax.dot_general(a, b, (((0,), (0,)), ((), ())), preferred_element_type=F32)


def _split3(x):
    hi = x.astype(BF16)
    r1 = x - hi.astype(F32)
    mid = r1.astype(BF16)
    lo = (r1 - mid.astype(F32)).astype(BF16)
    return hi, mid, lo


def _tail_kernel(h_ref, mix_ref, wo_ref, gm_ref, w1_ref, w2_ref, gp_ref, wpg_ref, p_ref,
                 wpp_ref, gf_ref, o_ref, *, final):
    h = h_ref[...] + _dot(mix_ref[...], wo_ref[...])
    xn = _rms(h, gm_ref[...]).astype(BF16)
    ff_chunk = D_FF // 4
    mlp = None
    for c in range(4):
        a = _dot(xn, w1_ref[:, c * ff_chunk:(c + 1) * ff_chunk])
        a = jnp.square(jnp.maximum(a, 0.0)).astype(BF16)
        part = _dot(a, w2_ref[c * ff_chunk:(c + 1) * ff_chunk, :])
        mlp = part if mlp is None else mlp + part
    h = h + mlp
    xg = _rms(h, gp_ref[...]).astype(BF16)
    gate = jax.nn.sigmoid(_dot(xg, wpg_ref[...]))
    pe = _dot(p_ref[...].astype(BF16), wpp_ref[...])
    h = h + pe * gate
    if final:
        h = _rms(h, gf_ref[...])
    o_ref[...] = h


def _tail(h, mix, wo, wo_layer, gm, w1, w2, gp, wpg, p_all, wpp, gf, layer, final):
    m = h.shape[0]
    tm = ROW_TILE
    row = lambda i: (i, 0)
    return pl.pallas_call(
        functools.partial(_tail_kernel, final=final),
        grid=(m // tm,),
        in_specs=[
            pl.BlockSpec((tm, D_MODEL), row),
            pl.BlockSpec((tm, mix.shape[1]), row),
            _layer(wo.shape[1:], wo_layer),
            _layer((1, D_MODEL), layer),
            _layer(w1.shape[1:], layer),
            _layer(w2.shape[1:], layer),
            _layer((1, D_MODEL), layer),
            _layer(wpg.shape[1:], layer),
            pl.BlockSpec((None, tm, PLE_DIM), lambda i: (layer, i, 0)),
            _layer(wpp.shape[1:], layer),
            _resident((1, D_MODEL)),
        ],
        out_specs=pl.BlockSpec((tm, D_MODEL), row),
        out_shape=jax.ShapeDtypeStruct((m, D_MODEL), F32),
        compiler_params=_params("parallel"),
    )(h, mix, wo, gm, w1, w2, gp, wpg, p_all, wpp, gf)


def _cmul(ar, ai, br, bi):
    return ar * br - ai * bi, ar * bi + ai * br


def _cpow_table(ar, ai, count):
    pr, pi = jnp.ones_like(ar), jnp.zeros_like(ar)
    table = [(pr, pi)]
    for _ in range(count):
        pr, pi = _cmul(pr, pi, ar, ai)
        table.append((pr, pi))
    return table


def _s5_discretize(lr, li, ld):
    dt = jnp.exp(ld)
    mag = jnp.exp(lr * dt)
    ar = mag * jnp.cos(li * dt)
    ai = mag * jnp.sin(li * dt)
    den = lr * lr + li * li
    nr = ar - 1.0
    coef_re = (nr * lr + ai * li) / den
    coef_im = (ai * lr - nr * li) / den
    return ar, ai, coef_re, coef_im


def _s5_prep_group(col_pow, cfr, cfi, row_pow, b_re, b_im, cr, ci, d):
    n, k, c = S5_STATE, S5_K, S5_GROUP
    hp = lax.Precision.HIGHEST
    rep = (lax.broadcasted_iota(jnp.int32, (c, k), 0)
           == (lax.broadcasted_iota(jnp.int32, (c, k), 1) & (c - 1))).astype(F32)
    brt = jnp.dot(b_re, rep, precision=hp, preferred_element_type=F32)
    bit = jnp.dot(b_im, rep, precision=hp, preferred_element_type=F32)
    bbr = cfr * brt - cfi * bit
    bbi = cfr * bit + cfi * brt
    grp = lax.broadcasted_iota(jnp.int32, (n, k), 1) >> 4
    pr = jnp.broadcast_to(col_pow[0][0], (n, k))
    pi = jnp.broadcast_to(col_pow[0][1], (n, k))
    for i in range(S5_L - 1):
        pr = jnp.where(grp == i, col_pow[S5_L - 1 - i][0], pr)
        pi = jnp.where(grp == i, col_pow[S5_L - 1 - i][1], pi)
    er, ei = _cmul(pr, pi, bbr, bbi)
    krev = (jnp.dot(cr, er, precision=hp, preferred_element_type=F32)
            - jnp.dot(ci, ei, precision=hp, preferred_element_type=F32))
    row = lax.broadcasted_iota(jnp.int32, (c, k), 0)
    col = lax.broadcasted_iota(jnp.int32, (c, k), 1)
    krev = krev + jnp.where(col == row + (k - c), d, 0.0)
    m_rows = []
    for t in range(S5_L):
        width = c * (t + 1)
        shifted = krev if width == k else pltpu.roll(krev, width, axis=1)
        m_rows.append(jnp.where(col < width, shifted, 0.0))
    f_re = [cr * row_pow[t + 1][0] - ci * row_pow[t + 1][1] for t in range(S5_L)]
    f_im = [-(cr * row_pow[t + 1][1] + ci * row_pow[t + 1][0]) for t in range(S5_L)]
    return m_rows, er, ei, f_re, f_im


def _s5_prep_kernel(lrc_ref, lic_ref, ldc_ref, lrr_ref, lir_ref, ldr_ref, br_ref, bi_ref, cr_ref,
                    ci_ref, d_ref, m_ref, e_ref, f_ref, al_ref):
    n, c = S5_STATE, S5_GROUP
    ar, ai, cfr, cfi = _s5_discretize(lrc_ref[...], lic_ref[...], ldc_ref[...])
    col_pow = _cpow_table(ar, ai, S5_L - 1)
    ar, ai, _, _ = _s5_discretize(lrr_ref[...], lir_ref[...], ldr_ref[...])
    row_pow = _cpow_table(ar, ai, S5_L)
    al_ref[:, 0, :] = row_pow[S5_L][0]
    al_ref[:, 1, :] = row_pow[S5_L][1]
    for gi in range(S5_PREP_GROUPS):
        colg = [(pr[:, gi:gi + 1], pi[:, gi:gi + 1]) for pr, pi in col_pow]
        rowg = [(pr[gi:gi + 1, :], pi[gi:gi + 1, :]) for pr, pi in row_pow]
        m_rows, er, ei, f_re, f_im = _s5_prep_group(
            colg, cfr[:, gi:gi + 1], cfi[:, gi:gi + 1], rowg, br_ref[gi], bi_ref[gi], cr_ref[gi],
            ci_ref[gi], d_ref[gi])
        e_ref[gi, 0:n, :] = er.astype(BF16)
        e_ref[gi, n:2 * n, :] = ei.astype(BF16)
        for t in range(S5_L):
            m_ref[gi, c * t:c * (t + 1), :] = m_rows[t].astype(BF16)
            f_ref[gi, c * t:c * (t + 1), 0:n] = f_re[t].astype(BF16)
            f_ref[gi, c * t:c * (t + 1), n:2 * n] = f_im[t].astype(BF16)


def _s5_prep(lam_re, lam_im, log_dt, b_re, b_im, c_re, c_im, d_skip):
    n, k, c = S5_STATE, S5_K, S5_GROUP
    g = lam_re.shape[0] * S5_GROUPS
    gb = S5_PREP_GROUPS
    steps = g // gb
    blk = lambda *shape: pl.BlockSpec((gb,) + shape, lambda i: (i,) + (0,) * len(shape))
    one = lambda *shape: pl.BlockSpec((None,) + shape, lambda i: (i,) + (0,) * len(shape))
    lr, li = lam_re.reshape(steps, gb, n), lam_im.reshape(steps, gb, n)
    m, e, f, al = pl.pallas_call(
        _s5_prep_kernel,
        grid=(steps,),
        in_specs=[one(n, gb), one(n, gb), one(1, gb), one(gb, n), one(gb, n), one(gb, 1),
                  blk(n, c), blk(n, c), blk(c, n), blk(c, n), blk(c, 1)],
        out_specs=[blk(k, k), blk(2 * n, k), blk(k, 2 * n), blk(2, n)],
        out_shape=[jax.ShapeDtypeStruct((g, k, k), BF16),
                   jax.ShapeDtypeStruct((g, 2 * n, k), BF16),
                   jax.ShapeDtypeStruct((g, k, 2 * n), BF16),
                   jax.ShapeDtypeStruct((g, 2, n), F32)],
        compiler_params=_params("parallel"),
    )(lr.transpose(0, 2, 1), li.transpose(0, 2, 1), log_dt.reshape(steps, 1, gb), lr, li,
      log_dt.reshape(steps, gb, 1),
      b_re.reshape(g, n, c), b_im.reshape(g, n, c), c_re.reshape(g, c, n), c_im.reshape(g, c, n),
      d_skip.reshape(g, c, 1))
    al = al.reshape(g // 2, 2, 2, n).transpose(0, 2, 1, 3).reshape(g // 2, 2, 1, 2 * n)
    return m, e, f, al


def _s5_in_kernel(h_ref, g_ref, w_ref, ut_ref, u_s, *, seq):
    for r in range(seq // ROW_TILE):
        rows = slice(r * ROW_TILE, (r + 1) * ROW_TILE)
        xn = _rms(h_ref[rows, :], g_ref[...]).astype(BF16)
        u = _dot(xn, w_ref[...])
        for lt in range(D_MODEL // LANES):
            u_s[lt, rows, :] = u[:, lt * LANES:(lt + 1) * LANES]
    nck = seq // S5_L
    gpl = LANES // S5_GROUP
    for j in range(S5_L):
        for lt in range(D_MODEL // LANES):
            xj = u_s[lt, pl.ds(j, nck, stride=S5_L), :]
            ut_ref[gpl * lt:gpl * (lt + 1), S5_GROUP * j:S5_GROUP * (j + 1), :] = (
                xj.T.reshape(gpl, S5_GROUP, nck).astype(BF16))


def _s5_core_kernel(ut_ref, m_ref, e_ref, f_ref, al_ref, yt_ref, sr_s, si_s, *, nck, batch):
    n = S5_STATE
    u = [ut_ref[g] for g in range(2)]
    s = [_dot(e_ref[g], u[g]) for g in range(2)]
    s_re = jnp.concatenate([s[0][0:n, :], s[1][0:n, :]], axis=0)
    s_im = jnp.concatenate([s[0][n:2 * n, :], s[1][n:2 * n, :]], axis=0)
    for b in range(batch):
        cols = slice(b * nck, (b + 1) * nck)
        sr_s[pl.ds(b, nck, stride=batch), :] = s_re[:, cols].T
        si_s[pl.ds(b, nck, stride=batch), :] = s_im[:, cols].T
    ar, ai = al_ref[0], al_ref[1]

    def step(k, prev):
        pr, pi = prev
        rows = pl.ds(pl.multiple_of(k * batch, batch), batch)
        cr, ci = sr_s[rows, :], si_s[rows, :]
        sr_s[rows, :] = pr
        si_s[rows, :] = pi
        return ar * pr - ai * pi + cr, ar * pi + ai * pr + ci

    zero = jnp.zeros((batch, 2 * n), F32)
    lax.fori_loop(0, nck, step, (zero, zero), unroll=8)
    p_re = jnp.concatenate([sr_s[pl.ds(b, nck, stride=batch), :].T for b in range(batch)], axis=1)
    p_im = jnp.concatenate([si_s[pl.ds(b, nck, stride=batch), :].T for b in range(batch)], axis=1)
    for g in range(2):
        sp = jnp.concatenate([p_re[g * n:(g + 1) * n, :], p_im[g * n:(g + 1) * n, :]], axis=0)
        yt_ref[g] = (_dot(m_ref[g], u[g]) + _dot(f_ref[g], sp.astype(BF16))).astype(BF16)


def _s5_post_kernel(yt_ref, wg_ref, mix_ref, y_s, *, seq):
    nck = seq // S5_L
    gpl = LANES // S5_GROUP
    nlt = D_MODEL // LANES
    for t in range(S5_L):
        for lt in range(nlt):
            piece = yt_ref[gpl * lt:gpl * (lt + 1), S5_GROUP * t:S5_GROUP * (t + 1), :]
            y_s[lt, pl.ds(t, nck, stride=S5_L), :] = piece.reshape(LANES, nck).T.astype(F32)
    for r in range(seq // ROW_TILE):
        rows = slice(r * ROW_TILE, (r + 1) * ROW_TILE)
        y = jnp.concatenate([y_s[lt, rows, :] for lt in range(nlt)], axis=1)
        a = jax.nn.gelu(y).astype(BF16)
        z = _dot(a, wg_ref[...])
        mix_ref[rows, :] = (z[:, :D_MODEL] * jax.nn.sigmoid(z[:, D_MODEL:])).astype(BF16)


def _s5_mixer(h, gm, layer, w_in, mats, w_glu, s5_layer, batch, seq):
    m_t, e_t, f_t, al = mats
    nck = seq // S5_L
    lanes = batch * nck
    assert nck % LANES == 0 and nck & (nck - 1) == 0
    ut = pl.pallas_call(
        functools.partial(_s5_in_kernel, seq=seq),
        grid=(batch,),
        in_specs=[pl.BlockSpec((seq, D_MODEL), lambda b: (b, 0)),
                  _layer((1, D_MODEL), layer), _layer(w_in.shape[1:], s5_layer)],
        out_specs=pl.BlockSpec((S5_GROUPS, S5_K, nck), lambda b: (0, 0, b)),
        out_shape=jax.ShapeDtypeStruct((S5_GROUPS, S5_K, lanes), BF16),
        scratch_shapes=[pltpu.VMEM((D_MODEL // LANES, seq, LANES), F32)],
        compiler_params=_params("parallel"),
    )(h, gm, w_in)
    p0 = s5_layer * (S5_GROUPS // 2)
    act = lambda *shape: pl.BlockSpec((2,) + shape, lambda g: (g,) + (0,) * len(shape))
    par = lambda *shape: pl.BlockSpec((2,) + shape, lambda g: (p0 + g,) + (0,) * len(shape))
    yt = pl.pallas_call(
        functools.partial(_s5_core_kernel, nck=nck, batch=batch),
        grid=(S5_GROUPS // 2,),
        in_specs=[act(S5_K, lanes), par(S5_K, S5_K), par(2 * S5_STATE, S5_K),
                  par(S5_K, 2 * S5_STATE),
                  pl.BlockSpec((None, 2, 1, 2 * S5_STATE), lambda g: (p0 + g, 0, 0, 0))],
        out_specs=act(S5_K, lanes),
        out_shape=jax.ShapeDtypeStruct((S5_GROUPS, S5_K, lanes), BF16),
        scratch_shapes=[pltpu.VMEM((lanes, 2 * S5_STATE), F32),
                        pltpu.VMEM((lanes, 2 * S5_STATE), F32)],
        compiler_params=_params("parallel"),
    )(ut, m_t, e_t, f_t, al)
    return pl.pallas_call(
        functools.partial(_s5_post_kernel, seq=seq),
        grid=(batch,),
        in_specs=[pl.BlockSpec((S5_GROUPS, S5_K, nck), lambda b: (0, 0, b)),
                  _layer(w_glu.shape[1:], s5_layer)],
        out_specs=pl.BlockSpec((seq, D_MODEL), lambda b: (b, 0)),
        out_shape=jax.ShapeDtypeStruct((batch * seq, D_MODEL), BF16),
        scratch_shapes=[pltpu.VMEM((D_MODEL // LANES, seq, LANES), F32)],
        compiler_params=_params("parallel"),
    )(yt, w_glu)


def _fox_in_kernel(h_ref, g_ref, wqkv_ref, wf_ref, bf_ref, q_ref, k_ref, v_ref, qc_ref, kc_ref,
                   carry_s):
    tm = ROW_TILE

    @pl.when(pl.program_id(1) == 0)
    def _():
        carry_s[...] = jnp.zeros_like(carry_s)

    row = lax.broadcasted_iota(jnp.int32, (tm, tm), 0)
    col = lax.broadcasted_iota(jnp.int32, (tm, tm), 1)
    tri = (row >= col).astype(BF16)
    lane = lax.broadcasted_iota(jnp.int32, (tm, LANES), 1)
    grp = lane >> 4
    carry = carry_s[...]
    for t in range(TILES_PER_STEP):
        rows = slice(t * tm, (t + 1) * tm)
        xn = _rms(h_ref[rows, :], g_ref[...]).astype(BF16)
        proj = _dot(xn, wqkv_ref[...])
        log_f = _log_sigmoid(_dot(xn, wf_ref[...]) + bf_ref[...])
        hi, mid, lo = _split3(log_f)
        cum = _dot(tri, hi) + _dot(tri, mid) + _dot(tri, lo) + carry
        carry = cum[tm - 1:tm, :]
        hi, mid, lo = _split3(cum * LOG2E)
        hi, mid, lo = hi.astype(F32), mid.astype(F32), lo.astype(F32)
        qc = jnp.where(grp == 0, hi, jnp.where(grp == 1, mid, jnp.where(grp == 2, lo,
             jnp.where(grp < FOX_BIAS_GROUPS, 1.0, 0.0))))
        kc = jnp.where(grp < 3, 1.0, jnp.where(grp == 3, -hi, jnp.where(grp == 4, -mid,
             jnp.where(grp == 5, -lo, 0.0))))
        qc_ref[rows, :] = qc.astype(BF16)
        kc_ref[rows, :] = kc.astype(BF16)
        q_ref[rows, :] = (proj[:, 0:D_MODEL] * (FOX_HEAD_DIM ** -0.5 * LOG2E)).astype(BF16)
        k_ref[rows, :] = proj[:, D_MODEL:2 * D_MODEL].astype(BF16)
        for pair in range(FOX_HEADS // 2):
            vp = proj[:, 2 * D_MODEL + pair * LANES:2 * D_MODEL + (pair + 1) * LANES]
            v_ref[2 * pair, rows, :] = jnp.where(
                lane < FOX_HEAD_DIM, vp, jnp.where(lane == FOX_HEAD_DIM, 1.0, 0.0)).astype(BF16)
            v_ref[2 * pair + 1, rows, :] = jnp.where(
                lane >= FOX_HEAD_DIM, vp, jnp.where(lane == 0, 1.0, 0.0)).astype(BF16)
    carry_s[...] = carry


def _fox_flash_kernel(q_ref, qc_ref, k_ref, kc_ref, v_ref, o_ref, *, seq):
    t = FOX_TQ
    pair = pl.program_id(1)
    lane = lax.broadcasted_iota(jnp.int32, (t, LANES), 1)
    row = lax.broadcasted_iota(jnp.int32, (t, t), 0)
    col = lax.broadcasted_iota(jnp.int32, (t, t), 1)
    causal = row >= col
    own = (lane < FOX_HEAD_DIM, lane >= FOX_HEAD_DIM)
    mine = [((lane & (FOX_HEADS - 1)) == 2 * pair + half) & (lane < FOX_HEADS * FOX_BIAS_GROUPS)
            for half in range(2)]
    causal2 = jnp.concatenate([causal, causal], axis=0)
    for i in range(seq // t):
        qrows = slice(i * t, (i + 1) * t)
        q = q_ref[qrows, :].astype(F32)
        qc = qc_ref[qrows, :].astype(F32)
        qcat = jnp.concatenate(
            [jnp.concatenate([jnp.where(own[half], q, 0.0), jnp.where(mine[half], qc, 0.0)], axis=1)
             for half in range(2)], axis=0).astype(BF16)
        m = jnp.full((2 * t, 1), NEG_INF, F32)
        acc = jnp.zeros((2 * t, LANES), F32)
        for j in range(i + 1):
            krows = slice(j * t, (j + 1) * t)
            kcat = jnp.concatenate([k_ref[krows, :], kc_ref[krows, :]], axis=1)
            vcat = jnp.concatenate([v_ref[0, krows, :], v_ref[1, krows, :]], axis=1)
            s = _dot_nt(qcat, kcat)
            if j == i:
                s = jnp.where(causal2, s, NEG_INF)
            m_new = jnp.maximum(m, jnp.max(s, axis=-1, keepdims=True))
            p = jnp.exp2(s - m_new)
            pv = _dot(p.astype(BF16), vcat)
            pv = jnp.concatenate([pv[0:t, 0:LANES], pv[t:2 * t, LANES:2 * LANES]], axis=0)
            acc = jnp.exp2(m - m_new) * acc + pv
            m = m_new
        out_a = acc[0:t] / acc[0:t, FOX_HEAD_DIM:FOX_HEAD_DIM + 1]
        out_b = acc[t:2 * t] / acc[t:2 * t, 0:1]
        o_ref[qrows, :] = jnp.where(lane < FOX_HEAD_DIM, out_a, out_b).astype(BF16)


def _fox_mixer(h, gm, layer, wqkv, fox_layer, wf, bf, batch, seq):
    tm = ROW_TILE * TILES_PER_STEP
    nt = seq // tm
    m = batch * seq
    tok = lambda b, i: (b * nt + i, 0)
    q, k, v, qc, kc = pl.pallas_call(
        _fox_in_kernel,
        grid=(batch, nt),
        in_specs=[pl.BlockSpec((tm, D_MODEL), tok),
                  _layer((1, D_MODEL), layer), _layer(wqkv.shape[1:], fox_layer),
                  _resident(wf.shape), _resident((1, LANES))],
        out_specs=[pl.BlockSpec((tm, D_MODEL), tok), pl.BlockSpec((tm, D_MODEL), tok),
                   pl.BlockSpec((None, FOX_HEADS, tm, LANES), lambda b, i: (b, 0, i, 0)),
                   pl.BlockSpec((tm, LANES), tok), pl.BlockSpec((tm, LANES), tok)],
        out_shape=[jax.ShapeDtypeStruct((m, D_MODEL), BF16),
                   jax.ShapeDtypeStruct((m, D_MODEL), BF16),
                   jax.ShapeDtypeStruct((batch, FOX_HEADS, seq, LANES), BF16),
                   jax.ShapeDtypeStruct((m, LANES), BF16),
                   jax.ShapeDtypeStruct((m, LANES), BF16)],
        scratch_shapes=[pltpu.VMEM((1, LANES), F32)],
        compiler_params=_params("parallel", "arbitrary"),
    )(h, gm, wqkv, wf, bf)
    slab = pl.BlockSpec((seq, LANES), lambda b, p: (b, p))
    bias = pl.BlockSpec((seq, LANES), lambda b, p: (b, 0))
    return pl.pallas_call(
        functools.partial(_fox_flash_kernel, seq=seq),
        grid=(batch, FOX_HEADS // 2),
        in_specs=[slab, bias, slab, bias,
                  pl.BlockSpec((None, 2, seq, LANES), lambda b, p: (b, p, 0, 0))],
        out_specs=slab,
        out_shape=jax.ShapeDtypeStruct((m, D_MODEL), BF16),
        compiler_params=_params("parallel", "parallel"),
    )(q, qc, k, kc, v)


def _gla_in_kernel(h_ref, g_ref, w_ref, wgl_ref, wg2_ref, bg_ref, qd_ref, ki_ref, ku_ref, v_ref,
                   rs_ref, dec_ref):
    tm = ROW_TILE
    nch = tm // GLA_CHUNK
    tb = 256
    row = lax.broadcasted_iota(jnp.int32, (tb, tb), 0)
    col = lax.broadcasted_iota(jnp.int32, (tb, tb), 1)
    tri = (((row >> 6) == (col >> 6)) & (row >= col)).astype(BF16)
    crow = lax.broadcasted_iota(jnp.int32, (nch, tm), 0)
    ccol = lax.broadcasted_iota(jnp.int32, (nch, tm), 1)
    pick = (crow == (ccol >> 6)).astype(BF16)
    for t in range(TILES_PER_STEP):
        rows = slice(t * tm, (t + 1) * tm)
        xn = _rms(h_ref[rows, :], g_ref[...]).astype(BF16)
        proj = _dot(xn, w_ref[...])
        q = proj[:, 0:GLA_KD]
        k = proj[:, GLA_KD:2 * GLA_KD]
        v = proj[:, 2 * GLA_KD:2 * GLA_KD + GLA_VD]
        r = proj[:, 2 * GLA_KD + GLA_VD:2 * GLA_KD + 2 * GLA_VD]
        g_lr = _dot(xn, wgl_ref[...]).astype(BF16)
        log_a = _log_sigmoid(_dot(g_lr, wg2_ref[...]) + bg_ref[...]) / GLA_GATE_NORM
        hi = log_a.astype(BF16)
        lo = (log_a - hi.astype(F32)).astype(BF16)
        bcum = jnp.concatenate(
            [_dot(tri, hi[r0:r0 + tb, :]) + _dot(tri, lo[r0:r0 + tb, :]) for r0 in range(0, tm, tb)],
            axis=0)
        b_last = _dot(pick, hi) + _dot(pick, lo)
        dec_ref[t * nch:(t + 1) * nch, :] = jnp.exp(b_last)
        b_last_rows = jnp.broadcast_to(
            b_last[:, None, :], (nch, GLA_CHUNK, GLA_KD)).reshape(tm, GLA_KD)
        qd_ref[rows, :] = (q * GLA_DK ** -0.5 * jnp.exp(bcum)).astype(BF16)
        ki_ref[rows, :] = (k * jnp.exp(-bcum)).astype(BF16)
        ku_ref[rows, :] = (k * jnp.exp(b_last_rows - bcum)).astype(BF16)
        v_ref[rows, :] = v.astype(BF16)
        rs_ref[rows, :] = (r * jax.nn.sigmoid(r)).astype(BF16)


def _gla_core_kernel(qd_ref, ki_ref, ku_ref, v_ref, rs_ref, dec_ref, gn_ref, o_ref, *, seq):
    c = GLA_CHUNK
    nb = GLA_BLOCK_CHUNKS
    blk = nb * c
    row = lax.broadcasted_iota(jnp.int32, (blk, blk), 0)
    col = lax.broadcasted_iota(jnp.int32, (blk, blk), 1)
    causal = ((row >> 6) == (col >> 6)) & (row >= col)
    gn = gn_ref[...]

    def block(rows, dec, states):
        new = []
        for hd in range(GLA_HEADS):
            kk = slice(hd * GLA_DK, (hd + 1) * GLA_DK)
            vv = slice(hd * GLA_DV, (hd + 1) * GLA_DV)
            qd, ki, ku, v = qd_ref[rows, kk], ki_ref[rows, kk], ku_ref[rows, kk], v_ref[rows, vv]
            att = jnp.where(causal, _dot_nt(qd, ki), 0.0)
            o = _dot(att.astype(BF16), v)
            st = states[hd]
            inter = []
            for j in range(nb):
                cr = slice(j * c, (j + 1) * c)
                inter.append(_dot_nt(qd[cr], st.astype(BF16)))
                st = st * dec[j:j + 1, kk] + _dot_tn(v[cr], ku[cr])
            o = o + jnp.concatenate(inter, axis=0)
            o = o * lax.rsqrt(jnp.mean(o * o, axis=-1, keepdims=True) + EPS)
            o_ref[rows, vv] = (o * gn[:, vv] * rs_ref[rows, vv].astype(F32)).astype(BF16)
            new.append(st)
        return tuple(new)

    def two_blocks(i, states):
        dec = dec_ref[pl.ds(pl.multiple_of(i * 2 * nb, 2 * nb), 2 * nb), :]
        for sub in range(2):
            rows = pl.ds(pl.multiple_of(i * 2 * blk + sub * blk, blk), blk)
            states = block(rows, dec[sub * nb:(sub + 1) * nb], states)
        return states

    lax.fori_loop(0, seq // (2 * blk), two_blocks,
                  (jnp.zeros((GLA_DV, GLA_DK), F32),) * GLA_HEADS)


def _gla_mixer(h, gm, layer, w_main, gla_layer, wgl, wg2, bg, gn, batch, seq):
    tm = ROW_TILE * TILES_PER_STEP
    m = batch * seq
    nch = tm // GLA_CHUNK
    row = lambda i: (i, 0)
    tok = lambda width, dtype: jax.ShapeDtypeStruct((m, width), dtype)
    qd, ki, ku, v, rs, dec = pl.pallas_call(
        _gla_in_kernel,
        grid=(m // tm,),
        in_specs=[pl.BlockSpec((tm, D_MODEL), row), _layer((1, D_MODEL), layer),
                  _layer(w_main.shape[1:], gla_layer), _resident(wgl.shape), _resident(wg2.shape),
                  _resident((1, GLA_KD))],
        out_specs=[pl.BlockSpec((tm, GLA_KD), row), pl.BlockSpec((tm, GLA_KD), row),
                   pl.BlockSpec((tm, GLA_KD), row), pl.BlockSpec((tm, GLA_VD), row),
                   pl.BlockSpec((tm, GLA_VD), row), pl.BlockSpec((nch, GLA_KD), row)],
        out_shape=[tok(GLA_KD, BF16), tok(GLA_KD, BF16), tok(GLA_KD, BF16), tok(GLA_VD, BF16),
                   tok(GLA_VD, BF16), jax.ShapeDtypeStruct((m // GLA_CHUNK, GLA_KD), F32)],
        compiler_params=_params("parallel"),
    )(h, gm, w_main, wgl, wg2, bg)
    blk = lambda rows, width: pl.BlockSpec((rows, width), lambda b: (b, 0))
    return pl.pallas_call(
        functools.partial(_gla_core_kernel, seq=seq),
        grid=(batch,),
        in_specs=[blk(seq, GLA_KD), blk(seq, GLA_KD), blk(seq, GLA_KD), blk(seq, GLA_VD),
                  blk(seq, GLA_VD), blk(seq // GLA_CHUNK, GLA_KD), _resident((1, GLA_VD))],
        out_specs=blk(seq, GLA_VD),
        out_shape=jax.ShapeDtypeStruct((m, GLA_VD), BF16),
        compiler_params=_params("parallel"),
    )(qd, ki, ku, v, rs, dec, gn)


def _pad_lanes(w, width):
    return jnp.pad(w, ((0, 0), (0, width - w.shape[1])))


def kernel(x, p, norm_mix, norm_mlp, norm_ple, s5_w_in, s5_lam_re, s5_lam_im, s5_log_dt, s5_b_re,
           s5_b_im, s5_c_re, s5_c_im, s5_d, s5_w_glu, s5_w_out, fox_w_in, fox_b_f, fox_w_out,
           gla_w_in, gla_w_g2, gla_b_g, gla_norm, gla_w_out, mlp_w1, mlp_w2, ple_proj, ple_gate,
           final_norm):
    batch, seq, _ = x.shape
    depth = p.shape[0]
    m = batch * seq
    h = x.reshape(m, D_MODEL)
    p_all = p.reshape(depth, m, PLE_DIM)
    gf = final_norm.reshape(1, D_MODEL)
    gm_mix = norm_mix.reshape(depth, 1, D_MODEL)
    gm_mlp = norm_mlp.reshape(depth, 1, D_MODEL)
    gm_ple = norm_ple.reshape(depth, 1, D_MODEL)
    w1, w2 = _to_bf16(mlp_w1), _to_bf16(mlp_w2)
    wpg, wpp = _to_bf16(ple_gate), _to_bf16(ple_proj)
    wo_s5, wo_fox, wo_gla = _to_bf16(s5_w_out), _to_bf16(fox_w_out), _to_bf16(gla_w_out)
    s5_in, s5_glu = _to_bf16(s5_w_in), _to_bf16(s5_w_glu)
    fox_qkv = _to_bf16(fox_w_in, cols=3 * D_MODEL)
    gla_main = _to_bf16(gla_w_in, cols=2 * GLA_KD + 2 * GLA_VD)
    s5_mats = _s5_prep(s5_lam_re, s5_lam_im, s5_log_dt, s5_b_re, s5_b_im, s5_c_re, s5_c_im, s5_d)
    for i in range(depth):
        mixer, j = i % 3, i // 3
        if mixer == 0:
            mix = _s5_mixer(h, gm_mix, i, s5_in, s5_mats, s5_glu, j, batch, seq)
            wo = wo_s5
        elif mixer == 1:
            w_f = fox_w_in[j, :, 3 * D_MODEL:]
            wf = _pad_lanes(jnp.tile(w_f, (1, FOX_BIAS_GROUPS)), LANES).astype(BF16)
            bf = _pad_lanes(jnp.tile(fox_b_f[j].reshape(1, FOX_HEADS), (1, FOX_BIAS_GROUPS)), LANES)
            mix = _fox_mixer(h, gm_mix, i, fox_qkv, j, wf, bf, batch, seq)
            wo = wo_fox
        else:
            wgl = _pad_lanes(gla_w_in[j, :, 2 * GLA_KD + 2 * GLA_VD:], LANES).astype(BF16)
            wg2 = jnp.pad(gla_w_g2[j], ((0, LANES - GLA_RANK), (0, 0))).astype(BF16)
            mix = _gla_mixer(h, gm_mix, i, gla_main, j, wgl, wg2, gla_b_g[j].reshape(1, GLA_KD),
                             gla_norm[j].reshape(1, GLA_VD), batch, seq)
            wo = wo_gla
        h = _tail(h, mix, wo, j, gm_mlp, w1, w2, gm_ple, wpg, p_all, wpp, gf, layer=i,
                  final=(i == depth - 1))
    return h.reshape(batch, seq, D_MODEL)
```

```python
import functools

import jax
import jax.numpy as jnp
from jax import lax
from jax.experimental import pallas as pl
from jax.experimental.pallas import tpu as pltpu

F32 = jnp.float32
BF16 = jnp.bfloat16

D_MODEL = 1024
D_FF = 4 * D_MODEL
PLE_DIM = 256
EPS = 1e-6
NEG_INF = -1e30
LOG2E = 1.4426950408889634

V7X_VMEM_BYTES = 64 * 1024 * 1024
VMEM_LIMIT_BYTES = V7X_VMEM_BYTES - 8 * 1024 * 1024
LANES = 128

S5_GROUP = 16
S5_GROUPS = D_MODEL // S5_GROUP
S5_STATE = 64
S5_L = 16
S5_K = S5_L * S5_GROUP
S5_PREP_GROUPS = 16

FOX_HEADS = 16
FOX_HEAD_DIM = D_MODEL // FOX_HEADS
FOX_TQ = 512
FOX_BIAS_GROUPS = 6

GLA_HEADS = 4
GLA_KD = D_MODEL // 2
GLA_VD = D_MODEL
GLA_DK = GLA_KD // GLA_HEADS
GLA_DV = GLA_VD // GLA_HEADS
GLA_RANK = 16
GLA_GATE_NORM = 16.0
GLA_CHUNK = 64
GLA_BLOCK_CHUNKS = 4

ROW_TILE = 512
TILES_PER_STEP = 2
TAIL_TILES_PER_STEP = 2
CAST_BLOCK_BYTES = 4 * 1024 * 1024


def _params(*sem):
    return pltpu.CompilerParams(dimension_semantics=sem, vmem_limit_bytes=VMEM_LIMIT_BYTES)


def _resident(shape):
    zeros = (0,) * len(shape)
    return pl.BlockSpec(shape, lambda *_: zeros, pipeline_mode=pl.Buffered(1))


def _layer(shape, layer):
    index = (layer,) + (0,) * len(shape)
    return pl.BlockSpec((None,) + tuple(shape), lambda *_: index, pipeline_mode=pl.Buffered(1))


def _cast_kernel(x_ref, o_ref):
    o_ref[...] = x_ref[:, 0:o_ref.shape[1]].astype(BF16)


def _to_bf16(w, cols=None):
    layers, k, n = w.shape
    cols = n if cols is None else cols
    tk = 8
    while tk * 2 <= k and tk * 2 * n * 4 <= CAST_BLOCK_BYTES:
        tk *= 2
    assert k % tk == 0
    return pl.pallas_call(
        _cast_kernel, grid=(layers, k // tk),
        in_specs=[pl.BlockSpec((None, tk, n), lambda l, i: (l, i, 0))],
        out_specs=pl.BlockSpec((None, tk, cols), lambda l, i: (l, i, 0)),
        out_shape=jax.ShapeDtypeStruct((layers, k, cols), BF16),
        compiler_params=_params("parallel", "parallel"),
    )(w)


def _rms(x, g):
    ms = jnp.mean(x * x, axis=-1, keepdims=True)
    return x * lax.rsqrt(ms + EPS) * g


def _log_sigmoid(x):
    return jnp.minimum(x, 0.0) - jnp.log1p(jnp.exp(-jnp.abs(x)))


def _dot(a, b):
    return jnp.dot(a, b, preferred_element_type=F32)


def _dot_nt(a, b):
    return lax.dot_general(a, b, (((1,), (1,)), ((), ())), preferred_element_type=F32)


def _dot_tn(a, b):
    return lax.dot_general(a, b, (((0,), (0,)), ((), ())), preferred_element_type=F32)


def _split3(x):
    hi = x.astype(BF16)
    r1 = x - hi.astype(F32)
    mid = r1.astype(BF16)
    lo = (r1 - mid.astype(F32)).astype(BF16)
    return hi, mid, lo


def _tail_kernel(h_ref, mix_ref, wo_ref, gm_ref, w1_ref, w2_ref, gp_ref, wpg_ref, p_ref,
                 wpp_ref, gf_ref, o_ref, *, final):
    ff_chunk = D_FF // 4
    for t in range(TAIL_TILES_PER_STEP):
        rows = slice(t * ROW_TILE, (t + 1) * ROW_TILE)
        h = h_ref[rows, :] + _dot(mix_ref[rows, :], wo_ref[...])
        xn = _rms(h, gm_ref[...]).astype(BF16)
        mlp = None
        for c in range(4):
            a = _dot(xn, w1_ref[:, c * ff_chunk:(c + 1) * ff_chunk])
            a = jnp.square(jnp.maximum(a, 0.0)).astype(BF16)
            part = _dot(a, w2_ref[c * ff_chunk:(c + 1) * ff_chunk, :])
            mlp = part if mlp is None else mlp + part
        h = h + mlp
        xg = _rms(h, gp_ref[...]).astype(BF16)
        gate = jax.nn.sigmoid(_dot(xg, wpg_ref[...]))
        pe = _dot(p_ref[rows, :].astype(BF16), wpp_ref[...])
        h = h + pe * gate
        if final:
            h = _rms(h, gf_ref[...])
        o_ref[rows, :] = h


def _tail(h, mix, wo, wo_layer, gm, w1, w2, gp, wpg, p_all, wpp, gf, layer, final):
    m = h.shape[0]
    tm = ROW_TILE * TAIL_TILES_PER_STEP
    row = lambda i: (i, 0)
    return pl.pallas_call(
        functools.partial(_tail_kernel, final=final),
        grid=(m // tm,),
        in_specs=[
            pl.BlockSpec((tm, D_MODEL), row),
            pl.BlockSpec((tm, mix.shape[1]), row),
            _layer(wo.shape[1:], wo_layer),
            _layer((1, D_MODEL), layer),
            _layer(w1.shape[1:], layer),
            _layer(w2.shape[1:], layer),
            _layer((1, D_MODEL), layer),
            _layer(wpg.shape[1:], layer),
            pl.BlockSpec((None, tm, PLE_DIM), lambda i: (layer, i, 0)),
            _layer(wpp.shape[1:], layer),
            _resident((1, D_MODEL)),
        ],
        out_specs=pl.BlockSpec((tm, D_MODEL), row),
        out_shape=jax.ShapeDtypeStruct((m, D_MODEL), F32),
        compiler_params=_params("parallel"),
    )(h, mix, wo, gm, w1, w2, gp, wpg, p_all, wpp, gf)


def _cmul(ar, ai, br, bi):
    return ar * br - ai * bi, ar * bi + ai * br


def _cpow_table(ar, ai, count):
    pr, pi = jnp.ones_like(ar), jnp.zeros_like(ar)
    table = [(pr, pi)]
    for _ in range(count):
        pr, pi = _cmul(pr, pi, ar, ai)
        table.append((pr, pi))
    return table


def _s5_discretize(lr, li, ld):
    dt = jnp.exp(ld)
    mag = jnp.exp(lr * dt)
    ar = mag * jnp.cos(li * dt)
    ai = mag * jnp.sin(li * dt)
    den = lr * lr + li * li
    nr = ar - 1.0
    coef_re = (nr * lr + ai * li) / den
    coef_im = (ai * lr - nr * li) / den
    return ar, ai, coef_re, coef_im


def _s5_prep_group(col_pow, cfr, cfi, row_pow, b_re, b_im, cr, ci, d):
    n, k, c = S5_STATE, S5_K, S5_GROUP
    hp = lax.Precision.HIGHEST
    rep = (lax.broadcasted_iota(jnp.int32, (c, k), 0)
           == (lax.broadcasted_iota(jnp.int32, (c, k), 1) & (c - 1))).astype(F32)
    brt = jnp.dot(b_re, rep, precision=hp, preferred_element_type=F32)
    bit = jnp.dot(b_im, rep, precision=hp, preferred_element_type=F32)
    bbr = cfr * brt - cfi * bit
    bbi = cfr * bit + cfi * brt
    grp = lax.broadcasted_iota(jnp.int32, (n, k), 1) >> 4
    pr = jnp.broadcast_to(col_pow[0][0], (n, k))
    pi = jnp.broadcast_to(col_pow[0][1], (n, k))
    for i in range(S5_L - 1):
        pr = jnp.where(grp == i, col_pow[S5_L - 1 - i][0], pr)
        pi = jnp.where(grp == i, col_pow[S5_L - 1 - i][1], pi)
    er, ei = _cmul(pr, pi, bbr, bbi)
    krev = (jnp.dot(cr, er, precision=hp, preferred_element_type=F32)
            - jnp.dot(ci, ei, precision=hp, preferred_element_type=F32))
    row = lax.broadcasted_iota(jnp.int32, (c, k), 0)
    col = lax.broadcasted_iota(jnp.int32, (c, k), 1)
    krev = krev + jnp.where(col == row + (k - c), d, 0.0)
    m_rows = []
    for t in range(S5_L):
        width = c * (t + 1)
        shifted = krev if width == k else pltpu.roll(krev, width, axis=1)
        m_rows.append(jnp.where(col < width, shifted, 0.0))
    f_re = [cr * row_pow[t + 1][0] - ci * row_pow[t + 1][1] for t in range(S5_L)]
    f_im = [-(cr * row_pow[t + 1][1] + ci * row_pow[t + 1][0]) for t in range(S5_L)]
    return m_rows, er, ei, f_re, f_im


def _s5_prep_kernel(lrc_ref, lic_ref, ldc_ref, lrr_ref, lir_ref, ldr_ref, br_ref, bi_ref, cr_ref,
                    ci_ref, d_ref, m_ref, e_ref, f_ref, al_ref):
    n, c = S5_STATE, S5_GROUP
    ar, ai, cfr, cfi = _s5_discretize(lrc_ref[...], lic_ref[...], ldc_ref[...])
    col_pow = _cpow_table(ar, ai, S5_L - 1)
    ar, ai, _, _ = _s5_discretize(lrr_ref[...], lir_ref[...], ldr_ref[...])
    row_pow = _cpow_table(ar, ai, S5_L)
    al_ref[:, 0, :] = row_pow[S5_L][0]
    al_ref[:, 1, :] = row_pow[S5_L][1]
    for gi in range(S5_PREP_GROUPS):
        colg = [(pr[:, gi:gi + 1], pi[:, gi:gi + 1]) for pr, pi in col_pow]
        rowg = [(pr[gi:gi + 1, :], pi[gi:gi + 1, :]) for pr, pi in row_pow]
        m_rows, er, ei, f_re, f_im = _s5_prep_group(
            colg, cfr[:, gi:gi + 1], cfi[:, gi:gi + 1], rowg, br_ref[gi], bi_ref[gi], cr_ref[gi],
            ci_ref[gi], d_ref[gi])
        e_ref[gi, 0:n, :] = er.astype(BF16)
        e_ref[gi, n:2 * n, :] = ei.astype(BF16)
        for t in range(S5_L):
            m_ref[gi, c * t:c * (t + 1), :] = m_rows[t].astype(BF16)
            f_ref[gi, c * t:c * (t + 1), 0:n] = f_re[t].astype(BF16)
            f_ref[gi, c * t:c * (t + 1), n:2 * n] = f_im[t].astype(BF16)


def _s5_prep(lam_re, lam_im, log_dt, b_re, b_im, c_re, c_im, d_skip):
    n, k, c = S5_STATE, S5_K, S5_GROUP
    g = lam_re.shape[0] * S5_GROUPS
    gb = S5_PREP_GROUPS
    steps = g // gb
    blk = lambda *shape: pl.BlockSpec((gb,) + shape, lambda i: (i,) + (0,) * len(shape))
    one = lambda *shape: pl.BlockSpec((None,) + shape, lambda i: (i,) + (0,) * len(shape))
    lr, li = lam_re.reshape(steps, gb, n), lam_im.reshape(steps, gb, n)
    m, e, f, al = pl.pallas_call(
        _s5_prep_kernel,
        grid=(steps,),
        in_specs=[one(n, gb), one(n, gb), one(1, gb), one(gb, n), one(gb, n), one(gb, 1),
                  blk(n, c), blk(n, c), blk(c, n), blk(c, n), blk(c, 1)],
        out_specs=[blk(k, k), blk(2 * n, k), blk(k, 2 * n), blk(2, n)],
        out_shape=[jax.ShapeDtypeStruct((g, k, k), BF16),
                   jax.ShapeDtypeStruct((g, 2 * n, k), BF16),
                   jax.ShapeDtypeStruct((g, k, 2 * n), BF16),
                   jax.ShapeDtypeStruct((g, 2, n), F32)],
        compiler_params=_params("parallel"),
    )(lr.transpose(0, 2, 1), li.transpose(0, 2, 1), log_dt.reshape(steps, 1, gb), lr, li,
      log_dt.reshape(steps, gb, 1),
      b_re.reshape(g, n, c), b_im.reshape(g, n, c), c_re.reshape(g, c, n), c_im.reshape(g, c, n),
      d_skip.reshape(g, c, 1))
    al = al.reshape(g // 2, 2, 2, n).transpose(0, 2, 1, 3).reshape(g // 2, 2, 1, 2 * n)
    return m, e, f, al


def _s5_in_kernel(h_ref, g_ref, w_ref, ut_ref, u_s, *, seq):
    for r in range(seq // ROW_TILE):
        rows = slice(r * ROW_TILE, (r + 1) * ROW_TILE)
        xn = _rms(h_ref[rows, :], g_ref[...]).astype(BF16)
        u = _dot(xn, w_ref[...])
        for lt in range(D_MODEL // LANES):
            u_s[lt, rows, :] = u[:, lt * LANES:(lt + 1) * LANES]
    nck = seq // S5_L
    gpl = LANES // S5_GROUP
    for j in range(S5_L):
        for lt in range(D_MODEL // LANES):
            xj = u_s[lt, pl.ds(j, nck, stride=S5_L), :]
            ut_ref[gpl * lt:gpl * (lt + 1), S5_GROUP * j:S5_GROUP * (j + 1), :] = (
                xj.T.reshape(gpl, S5_GROUP, nck).astype(BF16))


def _s5_core_kernel(ut_ref, m_ref, e_ref, f_ref, al_ref, yt_ref, sr_s, si_s, *, nck, batch):
    n = S5_STATE
    u = [ut_ref[g] for g in range(2)]
    s = [_dot(e_ref[g], u[g]) for g in range(2)]
    s_re = jnp.concatenate([s[0][0:n, :], s[1][0:n, :]], axis=0)
    s_im = jnp.concatenate([s[0][n:2 * n, :], s[1][n:2 * n, :]], axis=0)
    for b in range(batch):
        cols = slice(b * nck, (b + 1) * nck)
        sr_s[pl.ds(b, nck, stride=batch), :] = s_re[:, cols].T
        si_s[pl.ds(b, nck, stride=batch), :] = s_im[:, cols].T
    ar, ai = al_ref[0], al_ref[1]

    def step(k, prev):
        pr, pi = prev
        rows = pl.ds(pl.multiple_of(k * batch, batch), batch)
        cr, ci = sr_s[rows, :], si_s[rows, :]
        sr_s[rows, :] = pr
        si_s[rows, :] = pi
        return ar * pr - ai * pi + cr, ar * pi + ai * pr + ci

    zero = jnp.zeros((batch, 2 * n), F32)
    lax.fori_loop(0, nck, step, (zero, zero), unroll=8)
    p_re = jnp.concatenate([sr_s[pl.ds(b, nck, stride=batch), :].T for b in range(batch)], axis=1)
    p_im = jnp.concatenate([si_s[pl.ds(b, nck, stride=batch), :].T for b in range(batch)], axis=1)
    for g in range(2):
        sp = jnp.concatenate([p_re[g * n:(g + 1) * n, :], p_im[g * n:(g + 1) * n, :]], axis=0)
        yt_ref[g] = (_dot(m_ref[g], u[g]) + _dot(f_ref[g], sp.astype(BF16))).astype(BF16)


def _s5_post_kernel(yt_ref, wg_ref, mix_ref, y_s, *, seq):
    nck = seq // S5_L
    gpl = LANES // S5_GROUP
    nlt = D_MODEL // LANES
    for t in range(S5_L):
        for lt in range(nlt):
            piece = yt_ref[gpl * lt:gpl * (lt + 1), S5_GROUP * t:S5_GROUP * (t + 1), :]
            y_s[lt, pl.ds(t, nck, stride=S5_L), :] = piece.reshape(LANES, nck).T.astype(F32)
    for r in range(seq // ROW_TILE):
        rows = slice(r * ROW_TILE, (r + 1) * ROW_TILE)
        y = jnp.concatenate([y_s[lt, rows, :] for lt in range(nlt)], axis=1)
        a = jax.nn.gelu(y).astype(BF16)
        z = _dot(a, wg_ref[...])
        mix_ref[rows, :] = (z[:, :D_MODEL] * jax.nn.sigmoid(z[:, D_MODEL:])).astype(BF16)


def _s5_mixer(h, gm, layer, w_in, mats, w_glu, s5_layer, batch, seq):
    m_t, e_t, f_t, al = mats
    nck = seq // S5_L
    lanes = batch * nck
    assert nck % LANES == 0 and nck & (nck - 1) == 0
    ut = pl.pallas_call(
        functools.partial(_s5_in_kernel, seq=seq),
        grid=(batch,),
        in_specs=[pl.BlockSpec((seq, D_MODEL), lambda b: (b, 0)),
                  _layer((1, D_MODEL), layer), _layer(w_in.shape[1:], s5_layer)],
        out_specs=pl.BlockSpec((S5_GROUPS, S5_K, nck), lambda b: (0, 0, b)),
        out_shape=jax.ShapeDtypeStruct((S5_GROUPS, S5_K, lanes), BF16),
        scratch_shapes=[pltpu.VMEM((D_MODEL // LANES, seq, LANES), F32)],
        compiler_params=_params("parallel"),
    )(h, gm, w_in)
    p0 = s5_layer * (S5_GROUPS // 2)
    act = lambda *shape: pl.BlockSpec((2,) + shape, lambda g: (g,) + (0,) * len(shape))
    par = lambda *shape: pl.BlockSpec((2,) + shape, lambda g: (p0 + g,) + (0,) * len(shape))
    yt = pl.pallas_call(
        functools.partial(_s5_core_kernel, nck=nck, batch=batch),
        grid=(S5_GROUPS // 2,),
        in_specs=[act(S5_K, lanes), par(S5_K, S5_K), par(2 * S5_STATE, S5_K),
                  par(S5_K, 2 * S5_STATE),
                  pl.BlockSpec((None, 2, 1, 2 * S5_STATE), lambda g: (p0 + g, 0, 0, 0))],
        out_specs=act(S5_K, lanes),
        out_shape=jax.ShapeDtypeStruct((S5_GROUPS, S5_K, lanes), BF16),
        scratch_shapes=[pltpu.VMEM((lanes, 2 * S5_STATE), F32),
                        pltpu.VMEM((lanes, 2 * S5_STATE), F32)],
        compiler_params=_params("parallel"),
    )(ut, m_t, e_t, f_t, al)
    return pl.pallas_call(
        functools.partial(_s5_post_kernel, seq=seq),
        grid=(batch,),
        in_specs=[pl.BlockSpec((S5_GROUPS, S5_K, nck), lambda b: (0, 0, b)),
                  _layer(w_glu.shape[1:], s5_layer)],
        out_specs=pl.BlockSpec((seq, D_MODEL), lambda b: (b, 0)),
        out_shape=jax.ShapeDtypeStruct((batch * seq, D_MODEL), BF16),
        scratch_shapes=[pltpu.VMEM((D_MODEL // LANES, seq, LANES), F32)],
        compiler_params=_params("parallel"),
    )(yt, w_glu)


def _fox_in_kernel(h_ref, g_ref, wqkv_ref, wf_ref, bf_ref, q_ref, k_ref, v_ref, qc_ref, kc_ref,
                   carry_s):
    tm = ROW_TILE

    @pl.when(pl.program_id(1) == 0)
    def _():
        carry_s[...] = jnp.zeros_like(carry_s)

    row = lax.broadcasted_iota(jnp.int32, (tm, tm), 0)
    col = lax.broadcasted_iota(jnp.int32, (tm, tm), 1)
    tri = (row >= col).astype(BF16)
    lane = lax.broadcasted_iota(jnp.int32, (tm, LANES), 1)
    grp = lane >> 4
    carry = carry_s[...]
    for t in range(TILES_PER_STEP):
        rows = slice(t * tm, (t + 1) * tm)
        xn = _rms(h_ref[rows, :], g_ref[...]).astype(BF16)
        log_f = _log_sigmoid(_dot(xn, wf_ref[...]) + bf_ref[...])
        hi, mid, lo = _split3(log_f)
        q = _dot(xn, wqkv_ref[:, 0:D_MODEL])
        q_ref[rows, :] = (q * (FOX_HEAD_DIM ** -0.5 * LOG2E)).astype(BF16)
        cum = _dot(tri, hi) + _dot(tri, mid) + _dot(tri, lo) + carry
        carry = cum[tm - 1:tm, :]
        k_ref[rows, :] = _dot(xn, wqkv_ref[:, D_MODEL:2 * D_MODEL]).astype(BF16)
        hi, mid, lo = _split3(cum * LOG2E)
        hi, mid, lo = hi.astype(F32), mid.astype(F32), lo.astype(F32)
        qc = jnp.where(grp == 0, hi, jnp.where(grp == 1, mid, jnp.where(grp == 2, lo,
             jnp.where(grp < FOX_BIAS_GROUPS, 1.0, 0.0))))
        kc = jnp.where(grp < 3, 1.0, jnp.where(grp == 3, -hi, jnp.where(grp == 4, -mid,
             jnp.where(grp == 5, -lo, 0.0))))
        qc_ref[rows, :] = qc.astype(BF16)
        kc_ref[rows, :] = kc.astype(BF16)
        v = _dot(xn, wqkv_ref[:, 2 * D_MODEL:3 * D_MODEL])
        for pair in range(FOX_HEADS // 2):
            vp = v[:, pair * LANES:(pair + 1) * LANES]
            v_ref[2 * pair, rows, :] = jnp.where(
                lane < FOX_HEAD_DIM, vp, jnp.where(lane == FOX_HEAD_DIM, 1.0, 0.0)).astype(BF16)
            v_ref[2 * pair + 1, rows, :] = jnp.where(
                lane >= FOX_HEAD_DIM, vp, jnp.where(lane == 0, 1.0, 0.0)).astype(BF16)
    carry_s[...] = carry


def _fox_flash_kernel(q_ref, qc_ref, k_ref, kc_ref, v_ref, o_ref, *, seq):
    t = FOX_TQ
    pair = pl.program_id(1)
    lane = lax.broadcasted_iota(jnp.int32, (t, LANES), 1)
    row = lax.broadcasted_iota(jnp.int32, (t, t), 0)
    col = lax.broadcasted_iota(jnp.int32, (t, t), 1)
    causal = row >= col
    own = (lane < FOX_HEAD_DIM, lane >= FOX_HEAD_DIM)
    mine = [((lane & (FOX_HEADS - 1)) == 2 * pair + half) & (lane < FOX_HEADS * FOX_BIAS_GROUPS)
            for half in range(2)]
    causal2 = jnp.concatenate([causal, causal], axis=0)
    for i in range(seq // t):
        qrows = slice(i * t, (i + 1) * t)
        q = q_ref[qrows, :].astype(F32)
        qc = qc_ref[qrows, :].astype(F32)
        qcat = jnp.concatenate(
            [jnp.concatenate([jnp.where(own[half], q, 0.0), jnp.where(mine[half], qc, 0.0)], axis=1)
             for half in range(2)], axis=0).astype(BF16)
        m = jnp.full((2 * t, 1), NEG_INF, F32)
        acc = jnp.zeros((2 * t, LANES), F32)
        for j in range(i + 1):
            krows = slice(j * t, (j + 1) * t)
            kcat = jnp.concatenate([k_ref[krows, :], kc_ref[krows, :]], axis=1)
            vcat = jnp.concatenate([v_ref[0, krows, :], v_ref[1, krows, :]], axis=1)
            s = _dot_nt(qcat, kcat)
            if j == i:
                s = jnp.where(causal2, s, NEG_INF)
            m_new = jnp.maximum(m, jnp.max(s, axis=-1, keepdims=True))
            p = jnp.exp2(s - m_new)
            pv = _dot(p.astype(BF16), vcat)
            pv = jnp.concatenate([pv[0:t, 0:LANES], pv[t:2 * t, LANES:2 * LANES]], axis=0)
            acc = jnp.exp2(m - m_new) * acc + pv
            m = m_new
        out_a = acc[0:t] / acc[0:t, FOX_HEAD_DIM:FOX_HEAD_DIM + 1]
        out_b = acc[t:2 * t] / acc[t:2 * t, 0:1]
        o_ref[qrows, :] = jnp.where(lane < FOX_HEAD_DIM, out_a, out_b).astype(BF16)


def _fox_mixer(h, gm, layer, wqkv, fox_layer, wf, bf, batch, seq):
    tm = ROW_TILE * TILES_PER_STEP
    nt = seq // tm
    m = batch * seq
    tok = lambda b, i: (b * nt + i, 0)
    q, k, v, qc, kc = pl.pallas_call(
        _fox_in_kernel,
        grid=(batch, nt),
        in_specs=[pl.BlockSpec((tm, D_MODEL), tok),
                  _layer((1, D_MODEL), layer), _layer(wqkv.shape[1:], fox_layer),
                  _resident(wf.shape), _resident((1, LANES))],
        out_specs=[pl.BlockSpec((tm, D_MODEL), tok), pl.BlockSpec((tm, D_MODEL), tok),
                   pl.BlockSpec((None, FOX_HEADS, tm, LANES), lambda b, i: (b, 0, i, 0)),
                   pl.BlockSpec((tm, LANES), tok), pl.BlockSpec((tm, LANES), tok)],
        out_shape=[jax.ShapeDtypeStruct((m, D_MODEL), BF16),
                   jax.ShapeDtypeStruct((m, D_MODEL), BF16),
                   jax.ShapeDtypeStruct((batch, FOX_HEADS, seq, LANES), BF16),
                   jax.ShapeDtypeStruct((m, LANES), BF16),
                   jax.ShapeDtypeStruct((m, LANES), BF16)],
        scratch_shapes=[pltpu.VMEM((1, LANES), F32)],
        compiler_params=_params("parallel", "arbitrary"),
    )(h, gm, wqkv, wf, bf)
    slab = pl.BlockSpec((seq, LANES), lambda b, p: (b, p))
    bias = pl.BlockSpec((seq, LANES), lambda b, p: (b, 0))
    return pl.pallas_call(
        functools.partial(_fox_flash_kernel, seq=seq),
        grid=(batch, FOX_HEADS // 2),
        in_specs=[slab, bias, slab, bias,
                  pl.BlockSpec((None, 2, seq, LANES), lambda b, p: (b, p, 0, 0))],
        out_specs=slab,
        out_shape=jax.ShapeDtypeStruct((m, D_MODEL), BF16),
        compiler_params=_params("parallel", "parallel"),
    )(q, qc, k, kc, v)


def _gla_in_kernel(h_ref, g_ref, w_ref, wgl_ref, wg2_ref, bg_ref, qd_ref, ki_ref, ku_ref, v_ref,
                   rs_ref, dec_ref):
    tm = ROW_TILE
    nch = tm // GLA_CHUNK
    tb = 256
    row = lax.broadcasted_iota(jnp.int32, (tb, tb), 0)
    col = lax.broadcasted_iota(jnp.int32, (tb, tb), 1)
    tri = (((row >> 6) == (col >> 6)) & (row >= col)).astype(BF16)
    crow = lax.broadcasted_iota(jnp.int32, (nch, tm), 0)
    ccol = lax.broadcasted_iota(jnp.int32, (nch, tm), 1)
    pick = (crow == (ccol >> 6)).astype(BF16)
    for t in range(TILES_PER_STEP):
        rows = slice(t * tm, (t + 1) * tm)
        xn = _rms(h_ref[rows, :], g_ref[...]).astype(BF16)
        g_lr = _dot(xn, wgl_ref[...]).astype(BF16)
        log_a = _log_sigmoid(_dot(g_lr, wg2_ref[...]) + bg_ref[...]) / GLA_GATE_NORM
        hi = log_a.astype(BF16)
        lo = (log_a - hi.astype(F32)).astype(BF16)
        v_ref[rows, :] = _dot(xn, w_ref[:, 2 * GLA_KD:2 * GLA_KD + GLA_VD]).astype(BF16)
        bcum = jnp.concatenate(
            [_dot(tri, hi[r0:r0 + tb, :]) + _dot(tri, lo[r0:r0 + tb, :]) for r0 in range(0, tm, tb)],
            axis=0)
        b_last = _dot(pick, hi) + _dot(pick, lo)
        dec_ref[t * nch:(t + 1) * nch, :] = jnp.exp(b_last)
        r = _dot(xn, w_ref[:, 2 * GLA_KD + GLA_VD:2 * GLA_KD + 2 * GLA_VD])
        rs_ref[rows, :] = (r * jax.nn.sigmoid(r)).astype(BF16)
        b_last_rows = jnp.broadcast_to(
            b_last[:, None, :], (nch, GLA_CHUNK, GLA_KD)).reshape(tm, GLA_KD)
        q = _dot(xn, w_ref[:, 0:GLA_KD])
        qd_ref[rows, :] = (q * GLA_DK ** -0.5 * jnp.exp(bcum)).astype(BF16)
        k = _dot(xn, w_ref[:, GLA_KD:2 * GLA_KD])
        ki_ref[rows, :] = (k * jnp.exp(-bcum)).astype(BF16)
        ku_ref[rows, :] = (k * jnp.exp(b_last_rows - bcum)).astype(BF16)


def _gla_core_kernel(qd_ref, ki_ref, ku_ref, v_ref, rs_ref, dec_ref, gn_ref, o_ref, *, seq):
    c = GLA_CHUNK
    nb = GLA_BLOCK_CHUNKS
    blk = nb * c
    row = lax.broadcasted_iota(jnp.int32, (blk, blk), 0)
    col = lax.broadcasted_iota(jnp.int32, (blk, blk), 1)
    causal = ((row >> 6) == (col >> 6)) & (row >= col)
    gn = gn_ref[...]

    def block(rows, dec, states):
        new = []
        for hd in range(GLA_HEADS):
            kk = slice(hd * GLA_DK, (hd + 1) * GLA_DK)
            vv = slice(hd * GLA_DV, (hd + 1) * GLA_DV)
            qd, ki, ku, v = qd_ref[rows, kk], ki_ref[rows, kk], ku_ref[rows, kk], v_ref[rows, vv]
            att = jnp.where(causal, _dot_nt(qd, ki), 0.0)
            o = _dot(att.astype(BF16), v)
            st = states[hd]
            inter = []
            for j in range(nb):
                cr = slice(j * c, (j + 1) * c)
                inter.append(_dot_nt(qd[cr], st.astype(BF16)))
                st = st * dec[j:j + 1, kk] + _dot_tn(v[cr], ku[cr])
            o = o + jnp.concatenate(inter, axis=0)
            o = o * lax.rsqrt(jnp.mean(o * o, axis=-1, keepdims=True) + EPS)
            o_ref[rows, vv] = (o * gn[:, vv] * rs_ref[rows, vv].astype(F32)).astype(BF16)
            new.append(st)
        return tuple(new)

    def two_blocks(i, states):
        dec = dec_ref[pl.ds(pl.multiple_of(i * 2 * nb, 2 * nb), 2 * nb), :]
        for sub in range(2):
            rows = pl.ds(pl.multiple_of(i * 2 * blk + sub * blk, blk), blk)
            states = block(rows, dec[sub * nb:(sub + 1) * nb], states)
        return states

    lax.fori_loop(0, seq // (2 * blk), two_blocks,
                  (jnp.zeros((GLA_DV, GLA_DK), F32),) * GLA_HEADS)


def _gla_mixer(h, gm, layer, w_main, gla_layer, wgl, wg2, bg, gn, batch, seq):
    tm = ROW_TILE * TILES_PER_STEP
    m = batch * seq
    nch = tm // GLA_CHUNK
    row = lambda i: (i, 0)
    tok = lambda width, dtype: jax.ShapeDtypeStruct((m, width), dtype)
    qd, ki, ku, v, rs, dec = pl.pallas_call(
        _gla_in_kernel,
        grid=(m // tm,),
        in_specs=[pl.BlockSpec((tm, D_MODEL), row), _layer((1, D_MODEL), layer),
                  _layer(w_main.shape[1:], gla_layer), _resident(wgl.shape), _resident(wg2.shape),
                  _resident((1, GLA_KD))],
        out_specs=[pl.BlockSpec((tm, GLA_KD), row), pl.BlockSpec((tm, GLA_KD), row),
                   pl.BlockSpec((tm, GLA_KD), row), pl.BlockSpec((tm, GLA_VD), row),
                   pl.BlockSpec((tm, GLA_VD), row), pl.BlockSpec((nch, GLA_KD), row)],
        out_shape=[tok(GLA_KD, BF16), tok(GLA_KD, BF16), tok(GLA_KD, BF16), tok(GLA_VD, BF16),
                   tok(GLA_VD, BF16), jax.ShapeDtypeStruct((m // GLA_CHUNK, GLA_KD), F32)],
        compiler_params=_params("parallel"),
    )(h, gm, w_main, wgl, wg2, bg)
    blk = lambda rows, width: pl.BlockSpec((rows, width), lambda b: (b, 0))
    return pl.pallas_call(
        functools.partial(_gla_core_kernel, seq=seq),
        grid=(batch,),
        in_specs=[blk(seq, GLA_KD), blk(seq, GLA_KD), blk(seq, GLA_KD), blk(seq, GLA_VD),
                  blk(seq, GLA_VD), blk(seq // GLA_CHUNK, GLA_KD), _resident((1, GLA_VD))],
        out_specs=blk(seq, GLA_VD),
        out_shape=jax.ShapeDtypeStruct((m, GLA_VD), BF16),
        compiler_params=_params("parallel"),
    )(qd, ki, ku, v, rs, dec, gn)


def _pad_lanes(w, width):
    return jnp.pad(w, ((0, 0), (0, width - w.shape[1])))


def kernel(x, p, norm_mix, norm_mlp, norm_ple, s5_w_in, s5_lam_re, s5_lam_im, s5_log_dt, s5_b_re,
           s5_b_im, s5_c_re, s5_c_im, s5_d, s5_w_glu, s5_w_out, fox_w_in, fox_b_f, fox_w_out,
           gla_w_in, gla_w_g2, gla_b_g, gla_norm, gla_w_out, mlp_w1, mlp_w2, ple_proj, ple_gate,
           final_norm):
    batch, seq, _ = x.shape
    depth = p.shape[0]
    m = batch * seq
    h = x.reshape(m, D_MODEL)
    p_all = p.reshape(depth, m, PLE_DIM)
    gf = final_norm.reshape(1, D_MODEL)
    gm_mix = norm_mix.reshape(depth, 1, D_MODEL)
    gm_mlp = norm_mlp.reshape(depth, 1, D_MODEL)
    gm_ple = norm_ple.reshape(depth, 1, D_MODEL)
    w1, w2 = _to_bf16(mlp_w1), _to_bf16(mlp_w2)
    wpg, wpp = _to_bf16(ple_gate), _to_bf16(ple_proj)
    wo_s5, wo_fox, wo_gla = _to_bf16(s5_w_out), _to_bf16(fox_w_out), _to_bf16(gla_w_out)
    s5_in, s5_glu = _to_bf16(s5_w_in), _to_bf16(s5_w_glu)
    fox_qkv = _to_bf16(fox_w_in, cols=3 * D_MODEL)
    gla_main = _to_bf16(gla_w_in, cols=2 * GLA_KD + 2 * GLA_VD)
    s5_mats = _s5_prep(s5_lam_re, s5_lam_im, s5_log_dt, s5_b_re, s5_b_im, s5_c_re, s5_c_im, s5_d)
    for i in range(depth):
        mixer, j = i % 3, i // 3
        if mixer == 0:
            mix = _s5_mixer(h, gm_mix, i, s5_in, s5_mats, s5_glu, j, batch, seq)
            wo = wo_s5
        elif mixer == 1:
            w_f = fox_w_in[j, :, 3 * D_MODEL:]
            wf = _pad_lanes(jnp.tile(w_f, (1, FOX_BIAS_GROUPS)), LANES).astype(BF16)
            bf = _pad_lanes(jnp.tile(fox_b_f[j].reshape(1, FOX_HEADS), (1, FOX_BIAS_GROUPS)), LANES)
            mix = _fox_mixer(h, gm_mix, i, fox_qkv, j, wf, bf, batch, seq)
            wo = wo_fox
        else:
            wgl = _pad_lanes(gla_w_in[j, :, 2 * GLA_KD + 2 * GLA_VD:], LANES).astype(BF16)
            wg2 = jnp.pad(gla_w_g2[j], ((0, LANES - GLA_RANK), (0, 0))).astype(BF16)
            mix = _gla_mixer(h, gm_mix, i, gla_main, j, wgl, wg2, gla_b_g[j].reshape(1, GLA_KD),
                             gla_norm[j].reshape(1, GLA_VD), batch, seq)
            wo = wo_gla
        h = _tail(h, mix, wo, j, gm_mlp, w1, w2, gm_ple, wpg, p_all, wpp, gf, layer=i,
                  final=(i == depth - 1))
    return h.reshape(batch, seq, D_MODEL)
```

```python
import functools

import jax
import jax.numpy as jnp
from jax import lax
from jax.experimental import pallas as pl
from jax.experimental.pallas import tpu as pltpu

F32 = jnp.float32
BF16 = jnp.bfloat16

D_MODEL = 1024
D_FF = 4 * D_MODEL
PLE_DIM = 256
EPS = 1e-6
NEG_INF = -1e30
LOG2E = 1.4426950408889634

V7X_VMEM_BYTES = 64 * 1024 * 1024
VMEM_LIMIT_BYTES = V7X_VMEM_BYTES - 8 * 1024 * 1024
LANES = 128

S5_GROUP = 16
S5_GROUPS = D_MODEL // S5_GROUP
S5_STATE = 64
S5_L = 16
S5_K = S5_L * S5_GROUP
S5_PREP_GROUPS = 16

FOX_HEADS = 16
FOX_HEAD_DIM = D_MODEL // FOX_HEADS
FOX_TQ = 512
FOX_BIAS_GROUPS = 6

GLA_HEADS = 4
GLA_KD = D_MODEL // 2
GLA_VD = D_MODEL
GLA_DK = GLA_KD // GLA_HEADS
GLA_DV = GLA_VD // GLA_HEADS
GLA_RANK = 16
GLA_GATE_NORM = 16.0
GLA_CHUNK = 64
GLA_BLOCK_CHUNKS = 4

ROW_TILE = 512
TILES_PER_STEP = 2
TAIL_TILES_PER_STEP = 2
CAST_BLOCK_BYTES = 4 * 1024 * 1024


def _params(*sem):
    return pltpu.CompilerParams(dimension_semantics=sem, vmem_limit_bytes=VMEM_LIMIT_BYTES)


def _resident(shape):
    zeros = (0,) * len(shape)
    return pl.BlockSpec(shape, lambda *_: zeros, pipeline_mode=pl.Buffered(1))


def _layer(shape, layer):
    index = (layer,) + (0,) * len(shape)
    return pl.BlockSpec((None,) + tuple(shape), lambda *_: index, pipeline_mode=pl.Buffered(1))


def _cast_kernel(x_ref, o_ref):
    o_ref[...] = x_ref[:, 0:o_ref.shape[1]].astype(BF16)


def _to_bf16(w, cols=None):
    layers, k, n = w.shape
    cols = n if cols is None else cols
    tk = 8
    while tk * 2 <= k and tk * 2 * n * 4 <= CAST_BLOCK_BYTES:
        tk *= 2
    assert k % tk == 0
    return pl.pallas_call(
        _cast_kernel, grid=(layers, k // tk),
        in_specs=[pl.BlockSpec((None, tk, n), lambda l, i: (l, i, 0))],
        out_specs=pl.BlockSpec((None, tk, cols), lambda l, i: (l, i, 0)),
        out_shape=jax.ShapeDtypeStruct((layers, k, cols), BF16),
        compiler_params=_params("parallel", "parallel"),
    )(w)


def _rms(x, g):
    ms = jnp.mean(x * x, axis=-1, keepdims=True)
    return x * lax.rsqrt(ms + EPS) * g


def _log_sigmoid(x):
    return jnp.minimum(x, 0.0) - jnp.log1p(jnp.exp(-jnp.abs(x)))


def _dot(a, b):
    return jnp.dot(a, b, preferred_element_type=F32)


def _dot_nt(a, b):
    return lax.dot_general(a, b, (((1,), (1,)), ((), ())), preferred_element_type=F32)


def _dot_tn(a, b):
    return lax.dot_general(a, b, (((0,), (0,)), ((), ())), preferred_element_type=F32)


def _split3(x):
    hi = x.astype(BF16)
    r1 = x - hi.astype(F32)
    mid = r1.astype(BF16)
    lo = (r1 - mid.astype(F32)).astype(BF16)
    return hi, mid, lo


def _tail_kernel(h_ref, mix_ref, wo_ref, gm_ref, w1_ref, w2_ref, gp_ref, wpg_ref, p_ref,
                 wpp_ref, gf_ref, o_ref, *, final):
    ff_chunk = D_FF // 4
    for t in range(TAIL_TILES_PER_STEP):
        rows = slice(t * ROW_TILE, (t + 1) * ROW_TILE)
        h = h_ref[rows, :] + _dot(mix_ref[rows, :], wo_ref[...])
        xn = _rms(h, gm_ref[...]).astype(BF16)
        mlp = None
        for c in range(4):
            a = _dot(xn, w1_ref[:, c * ff_chunk:(c + 1) * ff_chunk])
            a = jnp.square(jnp.maximum(a, 0.0)).astype(BF16)
            part = _dot(a, w2_ref[c * ff_chunk:(c + 1) * ff_chunk, :])
            mlp = part if mlp is None else mlp + part
        h = h + mlp
        xg = _rms(h, gp_ref[...]).astype(BF16)
        gate = jax.nn.sigmoid(_dot(xg, wpg_ref[...]))
        pe = _dot(p_ref[rows, :].astype(BF16), wpp_ref[...])
        h = h + pe * gate
        if final:
            h = _rms(h, gf_ref[...])
        o_ref[rows, :] = h


def _tail(h, mix, wo, wo_layer, gm, w1, w2, gp, wpg, p_all, wpp, gf, layer, final):
    m = h.shape[0]
    tm = ROW_TILE * TAIL_TILES_PER_STEP
    row = lambda i: (i, 0)
    return pl.pallas_call(
        functools.partial(_tail_kernel, final=final),
        grid=(m // tm,),
        in_specs=[
            pl.BlockSpec((tm, D_MODEL), row),
            pl.BlockSpec((tm, mix.shape[1]), row),
            _layer(wo.shape[1:], wo_layer),
            _layer((1, D_MODEL), layer),
            _layer(w1.shape[1:], layer),
            _layer(w2.shape[1:], layer),
            _layer((1, D_MODEL), layer),
            _layer(wpg.shape[1:], layer),
            pl.BlockSpec((None, tm, PLE_DIM), lambda i: (layer, i, 0)),
            _layer(wpp.shape[1:], layer),
            _resident((1, D_MODEL)),
        ],
        out_specs=pl.BlockSpec((tm, D_MODEL), row),
        out_shape=jax.ShapeDtypeStruct((m, D_MODEL), F32),
        compiler_params=_params("parallel"),
    )(h, mix, wo, gm, w1, w2, gp, wpg, p_all, wpp, gf)


def _cmul(ar, ai, br, bi):
    return ar * br - ai * bi, ar * bi + ai * br


def _cpow_table(ar, ai, count):
    pr, pi = jnp.ones_like(ar), jnp.zeros_like(ar)
    table = [(pr, pi)]
    for _ in range(count):
        pr, pi = _cmul(pr, pi, ar, ai)
        table.append((pr, pi))
    return table


def _s5_discretize(lr, li, ld):
    dt = jnp.exp(ld)
    mag = jnp.exp(lr * dt)
    ar = mag * jnp.cos(li * dt)
    ai = mag * jnp.sin(li * dt)
    den = lr * lr + li * li
    nr = ar - 1.0
    coef_re = (nr * lr + ai * li) / den
    coef_im = (ai * lr - nr * li) / den
    return ar, ai, coef_re, coef_im


def _s5_prep_group(col_pow, cfr, cfi, row_pow, b_re, b_im, cr, ci, d):
    n, k, c = S5_STATE, S5_K, S5_GROUP
    hp = lax.Precision.HIGHEST
    rep = (lax.broadcasted_iota(jnp.int32, (c, k), 0)
           == (lax.broadcasted_iota(jnp.int32, (c, k), 1) & (c - 1))).astype(F32)
    brt = jnp.dot(b_re, rep, precision=hp, preferred_element_type=F32)
    bit = jnp.dot(b_im, rep, precision=hp, preferred_element_type=F32)
    bbr = cfr * brt - cfi * bit
    bbi = cfr * bit + cfi * brt
    grp = lax.broadcasted_iota(jnp.int32, (n, k), 1) >> 4
    pr = jnp.broadcast_to(col_pow[0][0], (n, k))
    pi = jnp.broadcast_to(col_pow[0][1], (n, k))
    for i in range(S5_L - 1):
        pr = jnp.where(grp == i, col_pow[S5_L - 1 - i][0], pr)
        pi = jnp.where(grp == i, col_pow[S5_L - 1 - i][1], pi)
    er, ei = _cmul(pr, pi, bbr, bbi)
    krev = (jnp.dot(cr, er, precision=hp, preferred_element_type=F32)
            - jnp.dot(ci, ei, precision=hp, preferred_element_type=F32))
    row = lax.broadcasted_iota(jnp.int32, (c, k), 0)
    col = lax.broadcasted_iota(jnp.int32, (c, k), 1)
    krev = krev + jnp.where(col == row + (k - c), d, 0.0)
    m_rows = []
    for t in range(S5_L):
        width = c * (t + 1)
        shifted = krev if width == k else pltpu.roll(krev, width, axis=1)
        m_rows.append(jnp.where(col < width, shifted, 0.0))
    f_re = [cr * row_pow[t + 1][0] - ci * row_pow[t + 1][1] for t in range(S5_L)]
    f_im = [-(cr * row_pow[t + 1][1] + ci * row_pow[t + 1][0]) for t in range(S5_L)]
    return m_rows, er, ei, f_re, f_im


def _s5_prep_kernel(lrc_ref, lic_ref, ldc_ref, lrr_ref, lir_ref, ldr_ref, br_ref, bi_ref, cr_ref,
                    ci_ref, d_ref, m_ref, e_ref, f_ref, al_ref):
    n, c = S5_STATE, S5_GROUP
    ar, ai, cfr, cfi = _s5_discretize(lrc_ref[...], lic_ref[...], ldc_ref[...])
    col_pow = _cpow_table(ar, ai, S5_L - 1)
    ar, ai, _, _ = _s5_discretize(lrr_ref[...], lir_ref[...], ldr_ref[...])
    row_pow = _cpow_table(ar, ai, S5_L)
    al_ref[:, 0, :] = row_pow[S5_L][0]
    al_ref[:, 1, :] = row_pow[S5_L][1]
    for gi in range(S5_PREP_GROUPS):
        colg = [(pr[:, gi:gi + 1], pi[:, gi:gi + 1]) for pr, pi in col_pow]
        rowg = [(pr[gi:gi + 1, :], pi[gi:gi + 1, :]) for pr, pi in row_pow]
        m_rows, er, ei, f_re, f_im = _s5_prep_group(
            colg, cfr[:, gi:gi + 1], cfi[:, gi:gi + 1], rowg, br_ref[gi], bi_ref[gi], cr_ref[gi],
            ci_ref[gi], d_ref[gi])
        e_ref[gi, 0:n, :] = er.astype(BF16)
        e_ref[gi, n:2 * n, :] = ei.astype(BF16)
        for t in range(S5_L):
            m_ref[gi, c * t:c * (t + 1), :] = m_rows[t].astype(BF16)
            f_ref[gi, c * t:c * (t + 1), 0:n] = f_re[t].astype(BF16)
            f_ref[gi, c * t:c * (t + 1), n:2 * n] = f_im[t].astype(BF16)


def _s5_prep(lam_re, lam_im, log_dt, b_re, b_im, c_re, c_im, d_skip):
    n, k, c = S5_STATE, S5_K, S5_GROUP
    g = lam_re.shape[0] * S5_GROUPS
    gb = S5_PREP_GROUPS
    steps = g // gb
    blk = lambda *shape: pl.BlockSpec((gb,) + shape, lambda i: (i,) + (0,) * len(shape))
    one = lambda *shape: pl.BlockSpec((None,) + shape, lambda i: (i,) + (0,) * len(shape))
    lr, li = lam_re.reshape(steps, gb, n), lam_im.reshape(steps, gb, n)
    m, e, f, al = pl.pallas_call(
        _s5_prep_kernel,
        grid=(steps,),
        in_specs=[one(n, gb), one(n, gb), one(1, gb), one(gb, n), one(gb, n), one(gb, 1),
                  blk(n, c), blk(n, c), blk(c, n), blk(c, n), blk(c, 1)],
        out_specs=[blk(k, k), blk(2 * n, k), blk(k, 2 * n), blk(2, n)],
        out_shape=[jax.ShapeDtypeStruct((g, k, k), BF16),
                   jax.ShapeDtypeStruct((g, 2 * n, k), BF16),
                   jax.ShapeDtypeStruct((g, k, 2 * n), BF16),
                   jax.ShapeDtypeStruct((g, 2, n), F32)],
        compiler_params=_params("parallel"),
    )(lr.transpose(0, 2, 1), li.transpose(0, 2, 1), log_dt.reshape(steps, 1, gb), lr, li,
      log_dt.reshape(steps, gb, 1),
      b_re.reshape(g, n, c), b_im.reshape(g, n, c), c_re.reshape(g, c, n), c_im.reshape(g, c, n),
      d_skip.reshape(g, c, 1))
    al = al.reshape(g // 2, 2, 2, n).transpose(0, 2, 1, 3).reshape(g // 2, 2, 1, 2 * n)
    return m, e, f, al


def _s5_in_kernel(h_ref, g_ref, w_ref, ut_ref, u_s, *, seq):
    for r in range(seq // ROW_TILE):
        rows = slice(r * ROW_TILE, (r + 1) * ROW_TILE)
        xn = _rms(h_ref[rows, :], g_ref[...]).astype(BF16)
        u = _dot(xn, w_ref[...])
        for lt in range(D_MODEL // LANES):
            u_s[lt, rows, :] = u[:, lt * LANES:(lt + 1) * LANES]
    nck = seq // S5_L
    gpl = LANES // S5_GROUP
    for j in range(S5_L):
        for lt in range(D_MODEL // LANES):
            xj = u_s[lt, pl.ds(j, nck, stride=S5_L), :]
            ut_ref[gpl * lt:gpl * (lt + 1), S5_GROUP * j:S5_GROUP * (j + 1), :] = (
                xj.T.reshape(gpl, S5_GROUP, nck).astype(BF16))


def _s5_core_kernel(ut_ref, m_ref, e_ref, f_ref, al_ref, yt_ref, sr_s, si_s, *, nck, batch):
    n = S5_STATE
    u = [ut_ref[g] for g in range(2)]
    s = [_dot(e_ref[g], u[g]) for g in range(2)]
    s_re = jnp.concatenate([s[0][0:n, :], s[1][0:n, :]], axis=0)
    s_im = jnp.concatenate([s[0][n:2 * n, :], s[1][n:2 * n, :]], axis=0)
    for b in range(batch):
        cols = slice(b * nck, (b + 1) * nck)
        sr_s[pl.ds(b, nck, stride=batch), :] = s_re[:, cols].T
        si_s[pl.ds(b, nck, stride=batch), :] = s_im[:, cols].T
    ar, ai = al_ref[0], al_ref[1]

    y_intra = [_dot(m_ref[g], u[g]) for g in range(2)]
    pr = pi = jnp.zeros((batch, 2 * n), F32)
    for k in range(nck):
        rows = slice(k * batch, (k + 1) * batch)
        cr, ci = sr_s[rows, :], si_s[rows, :]
        sr_s[rows, :] = pr
        si_s[rows, :] = pi
        pr, pi = ar * pr - ai * pi + cr, ar * pi + ai * pr + ci
    p_re = jnp.concatenate([sr_s[pl.ds(b, nck, stride=batch), :].T for b in range(batch)], axis=1)
    p_im = jnp.concatenate([si_s[pl.ds(b, nck, stride=batch), :].T for b in range(batch)], axis=1)
    for g in range(2):
        sp = jnp.concatenate([p_re[g * n:(g + 1) * n, :], p_im[g * n:(g + 1) * n, :]], axis=0)
        yt_ref[g] = (y_intra[g] + _dot(f_ref[g], sp.astype(BF16))).astype(BF16)


def _s5_post_kernel(yt_ref, wg_ref, mix_ref, y_s, *, seq):
    nck = seq // S5_L
    gpl = LANES // S5_GROUP
    nlt = D_MODEL // LANES
    for t in range(S5_L):
        for lt in range(nlt):
            piece = yt_ref[gpl * lt:gpl * (lt + 1), S5_GROUP * t:S5_GROUP * (t + 1), :]
            y_s[lt, pl.ds(t, nck, stride=S5_L), :] = piece.reshape(LANES, nck).T.astype(F32)
    for r in range(seq // ROW_TILE):
        rows = slice(r * ROW_TILE, (r + 1) * ROW_TILE)
        y = jnp.concatenate([y_s[lt, rows, :] for lt in range(nlt)], axis=1)
        a = jax.nn.gelu(y).astype(BF16)
        z = _dot(a, wg_ref[...])
        mix_ref[rows, :] = (z[:, :D_MODEL] * jax.nn.sigmoid(z[:, D_MODEL:])).astype(BF16)


def _s5_mixer(h, gm, layer, w_in, mats, w_glu, s5_layer, batch, seq):
    m_t, e_t, f_t, al = mats
    nck = seq // S5_L
    lanes = batch * nck
    assert nck % LANES == 0 and nck & (nck - 1) == 0
    ut = pl.pallas_call(
        functools.partial(_s5_in_kernel, seq=seq),
        grid=(batch,),
        in_specs=[pl.BlockSpec((seq, D_MODEL), lambda b: (b, 0)),
                  _layer((1, D_MODEL), layer), _layer(w_in.shape[1:], s5_layer)],
        out_specs=pl.BlockSpec((S5_GROUPS, S5_K, nck), lambda b: (0, 0, b)),
        out_shape=jax.ShapeDtypeStruct((S5_GROUPS, S5_K, lanes), BF16),
        scratch_shapes=[pltpu.VMEM((D_MODEL // LANES, seq, LANES), F32)],
        compiler_params=_params("parallel"),
    )(h, gm, w_in)
    p0 = s5_layer * (S5_GROUPS // 2)
    act = lambda *shape: pl.BlockSpec((2,) + shape, lambda g: (g,) + (0,) * len(shape))
    par = lambda *shape: pl.BlockSpec((2,) + shape, lambda g: (p0 + g,) + (0,) * len(shape))
    yt = pl.pallas_call(
        functools.partial(_s5_core_kernel, nck=nck, batch=batch),
        grid=(S5_GROUPS // 2,),
        in_specs=[act(S5_K, lanes), par(S5_K, S5_K), par(2 * S5_STATE, S5_K),
                  par(S5_K, 2 * S5_STATE),
                  pl.BlockSpec((None, 2, 1, 2 * S5_STATE), lambda g: (p0 + g, 0, 0, 0))],
        out_specs=act(S5_K, lanes),
        out_shape=jax.ShapeDtypeStruct((S5_GROUPS, S5_K, lanes), BF16),
        scratch_shapes=[pltpu.VMEM((lanes, 2 * S5_STATE), F32),
                        pltpu.VMEM((lanes, 2 * S5_STATE), F32)],
        compiler_params=_params("parallel"),
    )(ut, m_t, e_t, f_t, al)
    return pl.pallas_call(
        functools.partial(_s5_post_kernel, seq=seq),
        grid=(batch,),
        in_specs=[pl.BlockSpec((S5_GROUPS, S5_K, nck), lambda b: (0, 0, b)),
                  _layer(w_glu.shape[1:], s5_layer)],
        out_specs=pl.BlockSpec((seq, D_MODEL), lambda b: (b, 0)),
        out_shape=jax.ShapeDtypeStruct((batch * seq, D_MODEL), BF16),
        scratch_shapes=[pltpu.VMEM((D_MODEL // LANES, seq, LANES), F32)],
        compiler_params=_params("parallel"),
    )(yt, w_glu)


def _fox_in_kernel(h_ref, g_ref, wqkv_ref, wf_ref, bf_ref, q_ref, k_ref, v_ref, qc_ref, kc_ref,
                   carry_s):
    tm = ROW_TILE

    @pl.when(pl.program_id(1) == 0)
    def _():
        carry_s[...] = jnp.zeros_like(carry_s)

    row = lax.broadcasted_iota(jnp.int32, (tm, tm), 0)
    col = lax.broadcasted_iota(jnp.int32, (tm, tm), 1)
    tri = (row >= col).astype(BF16)
    lane = lax.broadcasted_iota(jnp.int32, (tm, LANES), 1)
    grp = lane >> 4
    carry = carry_s[...]
    for t in range(TILES_PER_STEP):
        rows = slice(t * tm, (t + 1) * tm)
        xn = _rms(h_ref[rows, :], g_ref[...]).astype(BF16)
        log_f = _log_sigmoid(_dot(xn, wf_ref[...]) + bf_ref[...])
        hi, mid, lo = _split3(log_f)
        q = _dot(xn, wqkv_ref[:, 0:D_MODEL])
        q_ref[rows, :] = (q * (FOX_HEAD_DIM ** -0.5 * LOG2E)).astype(BF16)
        cum = _dot(tri, hi) + _dot(tri, mid) + _dot(tri, lo) + carry
        carry = cum[tm - 1:tm, :]
        k_ref[rows, :] = _dot(xn, wqkv_ref[:, D_MODEL:2 * D_MODEL]).astype(BF16)
        hi, mid, lo = _split3(cum * LOG2E)
        hi, mid, lo = hi.astype(F32), mid.astype(F32), lo.astype(F32)
        qc = jnp.where(grp == 0, hi, jnp.where(grp == 1, mid, jnp.where(grp == 2, lo,
             jnp.where(grp < FOX_BIAS_GROUPS, 1.0, 0.0))))
        kc = jnp.where(grp < 3, 1.0, jnp.where(grp == 3, -hi, jnp.where(grp == 4, -mid,
             jnp.where(grp == 5, -lo, 0.0))))
        qc_ref[rows, :] = qc.astype(BF16)
        kc_ref[rows, :] = kc.astype(BF16)
        v = _dot(xn, wqkv_ref[:, 2 * D_MODEL:3 * D_MODEL])
        for pair in range(FOX_HEADS // 2):
            vp = v[:, pair * LANES:(pair + 1) * LANES]
            v_ref[2 * pair, rows, :] = jnp.where(
                lane < FOX_HEAD_DIM, vp, jnp.where(lane == FOX_HEAD_DIM, 1.0, 0.0)).astype(BF16)
            v_ref[2 * pair + 1, rows, :] = jnp.where(
                lane >= FOX_HEAD_DIM, vp, jnp.where(lane == 0, 1.0, 0.0)).astype(BF16)
    carry_s[...] = carry


def _fox_flash_kernel(q_ref, qc_ref, k_ref, kc_ref, v_ref, o_ref, *, seq):
    t = FOX_TQ
    pair = pl.program_id(1)
    lane = lax.broadcasted_iota(jnp.int32, (t, LANES), 1)
    row = lax.broadcasted_iota(jnp.int32, (t, t), 0)
    col = lax.broadcasted_iota(jnp.int32, (t, t), 1)
    causal = row >= col
    own = (lane < FOX_HEAD_DIM, lane >= FOX_HEAD_DIM)
    mine = [((lane & (FOX_HEADS - 1)) == 2 * pair + half) & (lane < FOX_HEADS * FOX_BIAS_GROUPS)
            for half in range(2)]
    causal2 = jnp.concatenate([causal, causal], axis=0)
    for i in range(seq // t):
        qrows = slice(i * t, (i + 1) * t)
        q = q_ref[qrows, :].astype(F32)
        qc = qc_ref[qrows, :].astype(F32)
        qcat = jnp.concatenate(
            [jnp.concatenate([jnp.where(own[half], q, 0.0), jnp.where(mine[half], qc, 0.0)], axis=1)
             for half in range(2)], axis=0).astype(BF16)
        m = jnp.full((2 * t, 1), NEG_INF, F32)
        acc = jnp.zeros((2 * t, LANES), F32)
        for j in range(i + 1):
            krows = slice(j * t, (j + 1) * t)
            kcat = jnp.concatenate([k_ref[krows, :], kc_ref[krows, :]], axis=1)
            vcat = jnp.concatenate([v_ref[0, krows, :], v_ref[1, krows, :]], axis=1)
            s = _dot_nt(qcat, kcat)
            if j == i:
                s = jnp.where(causal2, s, NEG_INF)
            m_new = jnp.maximum(m, jnp.max(s, axis=-1, keepdims=True))
            p = jnp.exp2(s - m_new)
            pv = _dot(p.astype(BF16), vcat)
            pv = jnp.concatenate([pv[0:t, 0:LANES], pv[t:2 * t, LANES:2 * LANES]], axis=0)
            acc = jnp.exp2(m - m_new) * acc + pv
            m = m_new
        out_a = acc[0:t] / acc[0:t, FOX_HEAD_DIM:FOX_HEAD_DIM + 1]
        out_b = acc[t:2 * t] / acc[t:2 * t, 0:1]
        o_ref[qrows, :] = jnp.where(lane < FOX_HEAD_DIM, out_a, out_b).astype(BF16)


def _fox_mixer(h, gm, layer, wqkv, fox_layer, wf, bf, batch, seq):
    tm = ROW_TILE * TILES_PER_STEP
    nt = seq // tm
    m = batch * seq
    tok = lambda b, i: (b * nt + i, 0)
    q, k, v, qc, kc = pl.pallas_call(
        _fox_in_kernel,
        grid=(batch, nt),
        in_specs=[pl.BlockSpec((tm, D_MODEL), tok),
                  _layer((1, D_MODEL), layer), _layer(wqkv.shape[1:], fox_layer),
                  _resident(wf.shape), _resident((1, LANES))],
        out_specs=[pl.BlockSpec((tm, D_MODEL), tok), pl.BlockSpec((tm, D_MODEL), tok),
                   pl.BlockSpec((None, FOX_HEADS, tm, LANES), lambda b, i: (b, 0, i, 0)),
                   pl.BlockSpec((tm, LANES), tok), pl.BlockSpec((tm, LANES), tok)],
        out_shape=[jax.ShapeDtypeStruct((m, D_MODEL), BF16),
                   jax.ShapeDtypeStruct((m, D_MODEL), BF16),
                   jax.ShapeDtypeStruct((batch, FOX_HEADS, seq, LANES), BF16),
                   jax.ShapeDtypeStruct((m, LANES), BF16),
                   jax.ShapeDtypeStruct((m, LANES), BF16)],
        scratch_shapes=[pltpu.VMEM((1, LANES), F32)],
        compiler_params=_params("parallel", "arbitrary"),
    )(h, gm, wqkv, wf, bf)
    slab = pl.BlockSpec((seq, LANES), lambda b, p: (b, p))
    bias = pl.BlockSpec((seq, LANES), lambda b, p: (b, 0))
    return pl.pallas_call(
        functools.partial(_fox_flash_kernel, seq=seq),
        grid=(batch, FOX_HEADS // 2),
        in_specs=[slab, bias, slab, bias,
                  pl.BlockSpec((None, 2, seq, LANES), lambda b, p: (b, p, 0, 0))],
        out_specs=slab,
        out_shape=jax.ShapeDtypeStruct((m, D_MODEL), BF16),
        compiler_params=_params("parallel", "parallel"),
    )(q, qc, k, kc, v)


def _gla_in_kernel(h_ref, g_ref, w_ref, wgl_ref, wg2_ref, bg_ref, qd_ref, ki_ref, ku_ref, v_ref,
                   rs_ref, dec_ref):
    tm = ROW_TILE
    nch = tm // GLA_CHUNK
    tb = 256
    row = lax.broadcasted_iota(jnp.int32, (tb, tb), 0)
    col = lax.broadcasted_iota(jnp.int32, (tb, tb), 1)
    tri = (((row >> 6) == (col >> 6)) & (row >= col)).astype(BF16)
    crow = lax.broadcasted_iota(jnp.int32, (nch, tm), 0)
    ccol = lax.broadcasted_iota(jnp.int32, (nch, tm), 1)
    pick = (crow == (ccol >> 6)).astype(BF16)
    for t in range(TILES_PER_STEP):
        rows = slice(t * tm, (t + 1) * tm)
        xn = _rms(h_ref[rows, :], g_ref[...]).astype(BF16)
        g_lr = _dot(xn, wgl_ref[...]).astype(BF16)
        log_a = _log_sigmoid(_dot(g_lr, wg2_ref[...]) + bg_ref[...]) / GLA_GATE_NORM
        hi = log_a.astype(BF16)
        lo = (log_a - hi.astype(F32)).astype(BF16)
        v_ref[rows, :] = _dot(xn, w_ref[:, 2 * GLA_KD:2 * GLA_KD + GLA_VD]).astype(BF16)
        bcum = jnp.concatenate(
            [_dot(tri, hi[r0:r0 + tb, :]) + _dot(tri, lo[r0:r0 + tb, :]) for r0 in range(0, tm, tb)],
            axis=0)
        b_last = _dot(pick, hi) + _dot(pick, lo)
        dec_ref[t * nch:(t + 1) * nch, :] = jnp.exp(b_last)
        r = _dot(xn, w_ref[:, 2 * GLA_KD + GLA_VD:2 * GLA_KD + 2 * GLA_VD])
        rs_ref[rows, :] = (r * jax.nn.sigmoid(r)).astype(BF16)
        b_last_rows = jnp.broadcast_to(
            b_last[:, None, :], (nch, GLA_CHUNK, GLA_KD)).reshape(tm, GLA_KD)
        q = _dot(xn, w_ref[:, 0:GLA_KD])
        qd_ref[rows, :] = (q * GLA_DK ** -0.5 * jnp.exp(bcum)).astype(BF16)
        k = _dot(xn, w_ref[:, GLA_KD:2 * GLA_KD])
        ki_ref[rows, :] = (k * jnp.exp(-bcum)).astype(BF16)
        ku_ref[rows, :] = (k * jnp.exp(b_last_rows - bcum)).astype(BF16)


def _gla_core_kernel(qd_ref, ki_ref, ku_ref, v_ref, rs_ref, dec_ref, gn_ref, o_ref, *, seq):
    c = GLA_CHUNK
    nb = GLA_BLOCK_CHUNKS
    blk = nb * c
    row = lax.broadcasted_iota(jnp.int32, (blk, blk), 0)
    col = lax.broadcasted_iota(jnp.int32, (blk, blk), 1)
    causal = ((row >> 6) == (col >> 6)) & (row >= col)
    gn = gn_ref[...]

    def block(rows, dec, states):
        new = []
        for hd in range(GLA_HEADS):
            kk = slice(hd * GLA_DK, (hd + 1) * GLA_DK)
            vv = slice(hd * GLA_DV, (hd + 1) * GLA_DV)
            qd, ki, ku, v = qd_ref[rows, kk], ki_ref[rows, kk], ku_ref[rows, kk], v_ref[rows, vv]
            att = jnp.where(causal, _dot_nt(qd, ki), 0.0)
            o = _dot(att.astype(BF16), v)
            st = states[hd]
            inter = []
            for j in range(nb):
                cr = slice(j * c, (j + 1) * c)
                inter.append(_dot_nt(qd[cr], st.astype(BF16)))
                st = st * dec[j:j + 1, kk] + _dot_tn(v[cr], ku[cr])
            o = o + jnp.concatenate(inter, axis=0)
            o = o * lax.rsqrt(jnp.mean(o * o, axis=-1, keepdims=True) + EPS)
            o_ref[rows, vv] = (o * gn[:, vv] * rs_ref[rows, vv].astype(F32)).astype(BF16)
            new.append(st)
        return tuple(new)

    def two_blocks(i, states):
        dec = dec_ref[pl.ds(pl.multiple_of(i * 2 * nb, 2 * nb), 2 * nb), :]
        for sub in range(2):
            rows = pl.ds(pl.multiple_of(i * 2 * blk + sub * blk, blk), blk)
            states = block(rows, dec[sub * nb:(sub + 1) * nb], states)
        return states

    lax.fori_loop(0, seq // (2 * blk), two_blocks,
                  (jnp.zeros((GLA_DV, GLA_DK), F32),) * GLA_HEADS)


def _gla_mixer(h, gm, layer, w_main, gla_layer, wgl, wg2, bg, gn, batch, seq):
    tm = ROW_TILE * TILES_PER_STEP
    m = batch * seq
    nch = tm // GLA_CHUNK
    row = lambda i: (i, 0)
    tok = lambda width, dtype: jax.ShapeDtypeStruct((m, width), dtype)
    qd, ki, ku, v, rs, dec = pl.pallas_call(
        _gla_in_kernel,
        grid=(m // tm,),
        in_specs=[pl.BlockSpec((tm, D_MODEL), row), _layer((1, D_MODEL), layer),
                  _layer(w_main.shape[1:], gla_layer), _resident(wgl.shape), _resident(wg2.shape),
                  _resident((1, GLA_KD))],
        out_specs=[pl.BlockSpec((tm, GLA_KD), row), pl.BlockSpec((tm, GLA_KD), row),
                   pl.BlockSpec((tm, GLA_KD), row), pl.BlockSpec((tm, GLA_VD), row),
                   pl.BlockSpec((tm, GLA_VD), row), pl.BlockSpec((nch, GLA_KD), row)],
        out_shape=[tok(GLA_KD, BF16), tok(GLA_KD, BF16), tok(GLA_KD, BF16), tok(GLA_VD, BF16),
                   tok(GLA_VD, BF16), jax.ShapeDtypeStruct((m // GLA_CHUNK, GLA_KD), F32)],
        compiler_params=_params("parallel"),
    )(h, gm, w_main, wgl, wg2, bg)
    blk = lambda rows, width: pl.BlockSpec((rows, width), lambda b: (b, 0))
    return pl.pallas_call(
        functools.partial(_gla_core_kernel, seq=seq),
        grid=(batch,),
        in_specs=[blk(seq, GLA_KD), blk(seq, GLA_KD), blk(seq, GLA_KD), blk(seq, GLA_VD),
                  blk(seq, GLA_VD), blk(seq // GLA_CHUNK, GLA_KD), _resident((1, GLA_VD))],
        out_specs=blk(seq, GLA_VD),
        out_shape=jax.ShapeDtypeStruct((m, GLA_VD), BF16),
        compiler_params=_params("parallel"),
    )(qd, ki, ku, v, rs, dec, gn)


def _pad_lanes(w, width):
    return jnp.pad(w, ((0, 0), (0, width - w.shape[1])))


def kernel(x, p, norm_mix, norm_mlp, norm_ple, s5_w_in, s5_lam_re, s5_lam_im, s5_log_dt, s5_b_re,
           s5_b_im, s5_c_re, s5_c_im, s5_d, s5_w_glu, s5_w_out, fox_w_in, fox_b_f, fox_w_out,
           gla_w_in, gla_w_g2, gla_b_g, gla_norm, gla_w_out, mlp_w1, mlp_w2, ple_proj, ple_gate,
           final_norm):
    batch, seq, _ = x.shape
    depth = p.shape[0]
    m = batch * seq
    h = x.reshape(m, D_MODEL)
    p_all = p.reshape(depth, m, PLE_DIM)
    gf = final_norm.reshape(1, D_MODEL)
    gm_mix = norm_mix.reshape(depth, 1, D_MODEL)
    gm_mlp = norm_mlp.reshape(depth, 1, D_MODEL)
    gm_ple = norm_ple.reshape(depth, 1, D_MODEL)
    w1, w2 = _to_bf16(mlp_w1), _to_bf16(mlp_w2)
    wpg, wpp = _to_bf16(ple_gate), _to_bf16(ple_proj)
    wo_s5, wo_fox, wo_gla = _to_bf16(s5_w_out), _to_bf16(fox_w_out), _to_bf16(gla_w_out)
    s5_in, s5_glu = _to_bf16(s5_w_in), _to_bf16(s5_w_glu)
    fox_qkv = _to_bf16(fox_w_in, cols=3 * D_MODEL)
    gla_main = _to_bf16(gla_w_in, cols=2 * GLA_KD + 2 * GLA_VD)
    s5_mats = _s5_prep(s5_lam_re, s5_lam_im, s5_log_dt, s5_b_re, s5_b_im, s5_c_re, s5_c_im, s5_d)
    for i in range(depth):
        mixer, j = i % 3, i // 3
        if mixer == 0:
            mix = _s5_mixer(h, gm_mix, i, s5_in, s5_mats, s5_glu, j, batch, seq)
            wo = wo_s5
        elif mixer == 1:
            w_f = fox_w_in[j, :, 3 * D_MODEL:]
            wf = _pad_lanes(jnp.tile(w_f, (1, FOX_BIAS_GROUPS)), LANES).astype(BF16)
            bf = _pad_lanes(jnp.tile(fox_b_f[j].reshape(1, FOX_HEADS), (1, FOX_BIAS_GROUPS)), LANES)
            mix = _fox_mixer(h, gm_mix, i, fox_qkv, j, wf, bf, batch, seq)
            wo = wo_fox
        else:
            wgl = _pad_lanes(gla_w_in[j, :, 2 * GLA_KD + 2 * GLA_VD:], LANES).astype(BF16)
            wg2 = jnp.pad(gla_w_g2[j], ((0, LANES - GLA_RANK), (0, 0))).astype(BF16)
            mix = _gla_mixer(h, gm_mix, i, gla_main, j, wgl, wg2, gla_b_g[j].reshape(1, GLA_KD),
                             gla_norm[j].reshape(1, GLA_VD), batch, seq)
            wo = wo_gla
        h = _tail(h, mix, wo, j, gm_mlp, w1, w2, gm_ple, wpg, p_all, wpp, gf, layer=i,
                  final=(i == depth - 1))
    return h.reshape(batch, seq, D_MODEL)
```

```python
import functools

import jax
import jax.numpy as jnp
from jax import lax
from jax.experimental import pallas as pl
from jax.experimental.pallas import tpu as pltpu

F32 = jnp.float32
BF16 = jnp.bfloat16

D_MODEL = 1024
D_FF = 4 * D_MODEL
PLE_DIM = 256
EPS = 1e-6
NEG_INF = -1e30
LOG2E = 1.4426950408889634

V7X_VMEM_BYTES = 64 * 1024 * 1024
VMEM_LIMIT_BYTES = V7X_VMEM_BYTES - 8 * 1024 * 1024
LANES = 128
V7X_MXU_DEPTH = 256

S5_GROUP = 16
S5_GROUPS = D_MODEL // S5_GROUP
S5_STATE = 64
S5_GROUP_SHIFT = S5_GROUP.bit_length() - 1
S5_L = 16
S5_K = S5_L * S5_GROUP
S5_PREP_GROUPS = 16

FOX_HEADS = 16
FOX_HEAD_DIM = D_MODEL // FOX_HEADS
FOX_HEADS_SHIFT = FOX_HEADS.bit_length() - 1
FOX_TQ = 512
FOX_BIAS_GROUPS = 6

GLA_HEADS = 4
GLA_KD = D_MODEL // 2
GLA_VD = D_MODEL
GLA_DK = GLA_KD // GLA_HEADS
GLA_DV = GLA_VD // GLA_HEADS
GLA_RANK = 16
GLA_GATE_NORM = 16.0
GLA_CHUNK = 64
GLA_CHUNK_SHIFT = GLA_CHUNK.bit_length() - 1
GLA_BLOCK_CHUNKS = 4

ROW_TILE = 512
TILES_PER_STEP = 2
TAIL_TILES_PER_STEP = 2
CAST_BLOCK_BYTES = 4 * 1024 * 1024


assert S5_GROUP == 1 << S5_GROUP_SHIFT and FOX_HEADS == 1 << FOX_HEADS_SHIFT
assert GLA_CHUNK == 1 << GLA_CHUNK_SHIFT


def _params(*sem):
    return pltpu.CompilerParams(dimension_semantics=sem, vmem_limit_bytes=VMEM_LIMIT_BYTES)


def _resident(shape):
    zeros = (0,) * len(shape)
    return pl.BlockSpec(shape, lambda *_: zeros, pipeline_mode=pl.Buffered(1))


def _layer(shape, layer):
    index = (layer,) + (0,) * len(shape)
    return pl.BlockSpec((None,) + tuple(shape), lambda *_: index, pipeline_mode=pl.Buffered(1))


def _cast_kernel(x_ref, o_ref):
    o_ref[...] = x_ref[:, 0:o_ref.shape[1]].astype(BF16)


def _to_bf16(w, cols=None):
    layers, k, n = w.shape
    cols = n if cols is None else cols
    tk = 8
    while tk * 2 <= k and tk * 2 * n * 4 <= CAST_BLOCK_BYTES:
        tk *= 2
    assert k % tk == 0
    return pl.pallas_call(
        _cast_kernel, grid=(layers, k // tk),
        in_specs=[pl.BlockSpec((None, tk, n), lambda l, i: (l, i, 0))],
        out_specs=pl.BlockSpec((None, tk, cols), lambda l, i: (l, i, 0)),
        out_shape=jax.ShapeDtypeStruct((layers, k, cols), BF16),
        compiler_params=_params("parallel", "parallel"),
    )(w)


def _rms(x, g):
    ms = jnp.mean(x * x, axis=-1, keepdims=True)
    return x * lax.rsqrt(ms + EPS) * g


def _log_sigmoid(x):
    return jnp.minimum(x, 0.0) - jnp.log1p(jnp.exp(-jnp.abs(x)))


def _dot(a, b):
    return jnp.dot(a, b, preferred_element_type=F32)


def _dot_nt(a, b):
    return lax.dot_general(a, b, (((1,), (1,)), ((), ())), preferred_element_type=F32)


def _dot_tn(a, b):
    return lax.dot_general(a, b, (((0,), (0,)), ((), ())), preferred_element_type=F32)


def _split3(x):
    hi = x.astype(BF16)
    r1 = x - hi.astype(F32)
    mid = r1.astype(BF16)
    lo = (r1 - mid.astype(F32)).astype(BF16)
    return hi, mid, lo


def _tail_kernel(h_ref, mix_ref, wo_ref, gm_ref, w1_ref, w2_ref, gp_ref, wpg_ref, p_ref,
                 wpp_ref, gf_ref, o_ref, *, final):
    ff_chunk = D_FF // 4
    for t in range(TAIL_TILES_PER_STEP):
        rows = slice(t * ROW_TILE, (t + 1) * ROW_TILE)
        h = h_ref[rows, :] + _dot(mix_ref[rows, :], wo_ref[...])
        xn = _rms(h, gm_ref[...]).astype(BF16)
        mlp = None
        for c in range(4):
            a = _dot(xn, w1_ref[:, c * ff_chunk:(c + 1) * ff_chunk])
            a = jnp.square(jnp.maximum(a, 0.0)).astype(BF16)
            part = _dot(a, w2_ref[c * ff_chunk:(c + 1) * ff_chunk, :])
            mlp = part if mlp is None else mlp + part
        h = h + mlp
        xg = _rms(h, gp_ref[...]).astype(BF16)
        gate = jax.nn.sigmoid(_dot(xg, wpg_ref[...]))
        pe = _dot(p_ref[rows, :].astype(BF16), wpp_ref[...])
        h = h + pe * gate
        if final:
            h = _rms(h, gf_ref[...])
        o_ref[rows, :] = h


def _tail(h, mix, wo, wo_layer, gm, w1, w2, gp, wpg, p_all, wpp, gf, layer, final):
    m = h.shape[0]
    tm = ROW_TILE * TAIL_TILES_PER_STEP
    row = lambda i: (i, 0)
    return pl.pallas_call(
        functools.partial(_tail_kernel, final=final),
        grid=(m // tm,),
        in_specs=[
            pl.BlockSpec((tm, D_MODEL), row),
            pl.BlockSpec((tm, mix.shape[1]), row),
            _layer(wo.shape[1:], wo_layer),
            _layer((1, D_MODEL), layer),
            _layer(w1.shape[1:], layer),
            _layer(w2.shape[1:], layer),
            _layer((1, D_MODEL), layer),
            _layer(wpg.shape[1:], layer),
            pl.BlockSpec((None, tm, PLE_DIM), lambda i: (layer, i, 0)),
            _layer(wpp.shape[1:], layer),
            _resident((1, D_MODEL)),
        ],
        out_specs=pl.BlockSpec((tm, D_MODEL), row),
        out_shape=jax.ShapeDtypeStruct((m, D_MODEL), F32),
        compiler_params=_params("parallel"),
    )(h, mix, wo, gm, w1, w2, gp, wpg, p_all, wpp, gf)


def _cmul(ar, ai, br, bi):
    return ar * br - ai * bi, ar * bi + ai * br


def _cpow_table(ar, ai, count):
    pr, pi = jnp.ones_like(ar), jnp.zeros_like(ar)
    table = [(pr, pi)]
    for _ in range(count):
        pr, pi = _cmul(pr, pi, ar, ai)
        table.append((pr, pi))
    return table


def _s5_discretize(lr, li, ld):
    dt = jnp.exp(ld)
    mag = jnp.exp(lr * dt)
    ar = mag * jnp.cos(li * dt)
    ai = mag * jnp.sin(li * dt)
    den = lr * lr + li * li
    nr = ar - 1.0
    coef_re = (nr * lr + ai * li) / den
    coef_im = (ai * lr - nr * li) / den
    return ar, ai, coef_re, coef_im


def _s5_prep_group(col_pow, cfr, cfi, row_pow, b_re, b_im, cr, ci, d):
    n, k, c = S5_STATE, S5_K, S5_GROUP
    hp = lax.Precision.HIGHEST
    rep = (lax.broadcasted_iota(jnp.int32, (c, k), 0)
           == (lax.broadcasted_iota(jnp.int32, (c, k), 1) & (c - 1))).astype(F32)
    brt = jnp.dot(b_re, rep, precision=hp, preferred_element_type=F32)
    bit = jnp.dot(b_im, rep, precision=hp, preferred_element_type=F32)
    bbr = cfr * brt - cfi * bit
    bbi = cfr * bit + cfi * brt
    grp = lax.broadcasted_iota(jnp.int32, (n, k), 1) >> S5_GROUP_SHIFT
    pr = jnp.broadcast_to(col_pow[0][0], (n, k))
    pi = jnp.broadcast_to(col_pow[0][1], (n, k))
    for i in range(S5_L - 1):
        pr = jnp.where(grp == i, col_pow[S5_L - 1 - i][0], pr)
        pi = jnp.where(grp == i, col_pow[S5_L - 1 - i][1], pi)
    er, ei = _cmul(pr, pi, bbr, bbi)
    krev = (jnp.dot(cr, er, precision=hp, preferred_element_type=F32)
            - jnp.dot(ci, ei, precision=hp, preferred_element_type=F32))
    row = lax.broadcasted_iota(jnp.int32, (c, k), 0)
    col = lax.broadcasted_iota(jnp.int32, (c, k), 1)
    krev = krev + jnp.where(col == row + (k - c), d, 0.0)
    m_rows = []
    for t in range(S5_L):
        width = c * (t + 1)
        shifted = krev if width == k else pltpu.roll(krev, width, axis=1)
        m_rows.append(jnp.where(col < width, shifted, 0.0))
    f_re = [cr * row_pow[t + 1][0] - ci * row_pow[t + 1][1] for t in range(S5_L)]
    f_im = [-(cr * row_pow[t + 1][1] + ci * row_pow[t + 1][0]) for t in range(S5_L)]
    return m_rows, er, ei, f_re, f_im


def _s5_prep_kernel(lrc_ref, lic_ref, ldc_ref, lrr_ref, lir_ref, ldr_ref, br_ref, bi_ref, cr_ref,
                    ci_ref, d_ref, m_ref, e_ref, f_ref, al_ref):
    n, c = S5_STATE, S5_GROUP
    ar, ai, cfr, cfi = _s5_discretize(lrc_ref[...], lic_ref[...], ldc_ref[...])
    col_pow = _cpow_table(ar, ai, S5_L - 1)
    ar, ai, _, _ = _s5_discretize(lrr_ref[...], lir_ref[...], ldr_ref[...])
    row_pow = _cpow_table(ar, ai, S5_L)
    al_ref[:, 0, :] = row_pow[S5_L][0]
    al_ref[:, 1, :] = row_pow[S5_L][1]
    for gi in range(S5_PREP_GROUPS):
        colg = [(pr[:, gi:gi + 1], pi[:, gi:gi + 1]) for pr, pi in col_pow]
        rowg = [(pr[gi:gi + 1, :], pi[gi:gi + 1, :]) for pr, pi in row_pow]
        m_rows, er, ei, f_re, f_im = _s5_prep_group(
            colg, cfr[:, gi:gi + 1], cfi[:, gi:gi + 1], rowg, br_ref[gi], bi_ref[gi], cr_ref[gi],
            ci_ref[gi], d_ref[gi])
        e_ref[gi, 0:n, :] = er.astype(BF16)
        e_ref[gi, n:2 * n, :] = ei.astype(BF16)
        for t in range(S5_L):
            m_ref[gi, c * t:c * (t + 1), :] = m_rows[t].astype(BF16)
            f_ref[gi, c * t:c * (t + 1), 0:n] = f_re[t].astype(BF16)
            f_ref[gi, c * t:c * (t + 1), n:2 * n] = f_im[t].astype(BF16)


def _s5_prep(lam_re, lam_im, log_dt, b_re, b_im, c_re, c_im, d_skip):
    n, k, c = S5_STATE, S5_K, S5_GROUP
    g = lam_re.shape[0] * S5_GROUPS
    gb = S5_PREP_GROUPS
    steps = g // gb
    blk = lambda *shape: pl.BlockSpec((gb,) + shape, lambda i: (i,) + (0,) * len(shape))
    one = lambda *shape: pl.BlockSpec((None,) + shape, lambda i: (i,) + (0,) * len(shape))
    lr, li = lam_re.reshape(steps, gb, n), lam_im.reshape(steps, gb, n)
    m, e, f, al = pl.pallas_call(
        _s5_prep_kernel,
        grid=(steps,),
        in_specs=[one(n, gb), one(n, gb), one(1, gb), one(gb, n), one(gb, n), one(gb, 1),
                  blk(n, c), blk(n, c), blk(c, n), blk(c, n), blk(c, 1)],
        out_specs=[blk(k, k), blk(2 * n, k), blk(k, 2 * n), blk(2, n)],
        out_shape=[jax.ShapeDtypeStruct((g, k, k), BF16),
                   jax.ShapeDtypeStruct((g, 2 * n, k), BF16),
                   jax.ShapeDtypeStruct((g, k, 2 * n), BF16),
                   jax.ShapeDtypeStruct((g, 2, n), F32)],
        compiler_params=_params("parallel"),
    )(lr.transpose(0, 2, 1), li.transpose(0, 2, 1), log_dt.reshape(steps, 1, gb), lr, li,
      log_dt.reshape(steps, gb, 1),
      b_re.reshape(g, n, c), b_im.reshape(g, n, c), c_re.reshape(g, c, n), c_im.reshape(g, c, n),
      d_skip.reshape(g, c, 1))
    al = al.reshape(g // 2, 2, 2, n).transpose(0, 2, 1, 3).reshape(g // 2, 2, 1, 2 * n)
    return m, e, f, al


def _s5_in_kernel(h_ref, g_ref, w_ref, ut_ref, u_s, *, seq):
    for r in range(seq // ROW_TILE):
        rows = slice(r * ROW_TILE, (r + 1) * ROW_TILE)
        xn = _rms(h_ref[rows, :], g_ref[...]).astype(BF16)
        u = _dot(xn, w_ref[...])
        for lt in range(D_MODEL // LANES):
            u_s[lt, rows, :] = u[:, lt * LANES:(lt + 1) * LANES]
    nck = seq // S5_L
    gpl = LANES // S5_GROUP
    for j in range(S5_L):
        for lt in range(D_MODEL // LANES):
            xj = u_s[lt, pl.ds(j, nck, stride=S5_L), :]
            ut_ref[gpl * lt:gpl * (lt + 1), S5_GROUP * j:S5_GROUP * (j + 1), :] = (
                xj.T.reshape(gpl, S5_GROUP, nck).astype(BF16))


def _s5_core_kernel(ut_ref, m_ref, e_ref, f_ref, al_ref, yt_ref, sr_s, si_s, *, nck, batch):
    n = S5_STATE
    u = [ut_ref[g] for g in range(2)]
    s = [_dot(e_ref[g], u[g]) for g in range(2)]
    s_re = jnp.concatenate([s[0][0:n, :], s[1][0:n, :]], axis=0)
    s_im = jnp.concatenate([s[0][n:2 * n, :], s[1][n:2 * n, :]], axis=0)
    for b in range(batch):
        cols = slice(b * nck, (b + 1) * nck)
        sr_s[pl.ds(b, nck, stride=batch), :] = s_re[:, cols].T
        si_s[pl.ds(b, nck, stride=batch), :] = s_im[:, cols].T
    ar, ai = al_ref[0], al_ref[1]

    y_intra = [_dot(m_ref[g], u[g]) for g in range(2)]
    pr = pi = jnp.zeros((batch, 2 * n), F32)
    for k in range(nck):
        rows = slice(k * batch, (k + 1) * batch)
        cr, ci = sr_s[rows, :], si_s[rows, :]
        sr_s[rows, :] = pr
        si_s[rows, :] = pi
        pr, pi = ar * pr - ai * pi + cr, ar * pi + ai * pr + ci
    p_re = jnp.concatenate([sr_s[pl.ds(b, nck, stride=batch), :].T for b in range(batch)], axis=1)
    p_im = jnp.concatenate([si_s[pl.ds(b, nck, stride=batch), :].T for b in range(batch)], axis=1)
    for g in range(2):
        sp = jnp.concatenate([p_re[g * n:(g + 1) * n, :], p_im[g * n:(g + 1) * n, :]], axis=0)
        yt_ref[g] = (y_intra[g] + _dot(f_ref[g], sp.astype(BF16))).astype(BF16)


def _s5_post_kernel(yt_ref, wg_ref, mix_ref, y_s, *, seq):
    nck = seq // S5_L
    gpl = LANES // S5_GROUP
    nlt = D_MODEL // LANES
    for t in range(S5_L):
        for lt in range(nlt):
            piece = yt_ref[gpl * lt:gpl * (lt + 1), S5_GROUP * t:S5_GROUP * (t + 1), :]
            y_s[lt, pl.ds(t, nck, stride=S5_L), :] = piece.reshape(LANES, nck).T.astype(F32)
    for r in range(seq // ROW_TILE):
        rows = slice(r * ROW_TILE, (r + 1) * ROW_TILE)
        y = jnp.concatenate([y_s[lt, rows, :] for lt in range(nlt)], axis=1)
        a = jax.nn.gelu(y).astype(BF16)
        z = _dot(a, wg_ref[...])
        mix_ref[rows, :] = (z[:, :D_MODEL] * jax.nn.sigmoid(z[:, D_MODEL:])).astype(BF16)


def _s5_mixer(h, gm, layer, w_in, mats, w_glu, s5_layer, batch, seq):
    m_t, e_t, f_t, al = mats
    nck = seq // S5_L
    lanes = batch * nck
    assert nck % LANES == 0 and nck & (nck - 1) == 0
    ut = pl.pallas_call(
        functools.partial(_s5_in_kernel, seq=seq),
        grid=(batch,),
        in_specs=[pl.BlockSpec((seq, D_MODEL), lambda b: (b, 0)),
                  _layer((1, D_MODEL), layer), _layer(w_in.shape[1:], s5_layer)],
        out_specs=pl.BlockSpec((S5_GROUPS, S5_K, nck), lambda b: (0, 0, b)),
        out_shape=jax.ShapeDtypeStruct((S5_GROUPS, S5_K, lanes), BF16),
        scratch_shapes=[pltpu.VMEM((D_MODEL // LANES, seq, LANES), F32)],
        compiler_params=_params("parallel"),
    )(h, gm, w_in)
    p0 = s5_layer * (S5_GROUPS // 2)
    act = lambda *shape: pl.BlockSpec((2,) + shape, lambda g: (g,) + (0,) * len(shape))
    par = lambda *shape: pl.BlockSpec((2,) + shape, lambda g: (p0 + g,) + (0,) * len(shape))
    yt = pl.pallas_call(
        functools.partial(_s5_core_kernel, nck=nck, batch=batch),
        grid=(S5_GROUPS // 2,),
        in_specs=[act(S5_K, lanes), par(S5_K, S5_K), par(2 * S5_STATE, S5_K),
                  par(S5_K, 2 * S5_STATE),
                  pl.BlockSpec((None, 2, 1, 2 * S5_STATE), lambda g: (p0 + g, 0, 0, 0))],
        out_specs=act(S5_K, lanes),
        out_shape=jax.ShapeDtypeStruct((S5_GROUPS, S5_K, lanes), BF16),
        scratch_shapes=[pltpu.VMEM((lanes, 2 * S5_STATE), F32),
                        pltpu.VMEM((lanes, 2 * S5_STATE), F32)],
        compiler_params=_params("parallel"),
    )(ut, m_t, e_t, f_t, al)
    return pl.pallas_call(
        functools.partial(_s5_post_kernel, seq=seq),
        grid=(batch,),
        in_specs=[pl.BlockSpec((S5_GROUPS, S5_K, nck), lambda b: (0, 0, b)),
                  _layer(w_glu.shape[1:], s5_layer)],
        out_specs=pl.BlockSpec((seq, D_MODEL), lambda b: (b, 0)),
        out_shape=jax.ShapeDtypeStruct((batch * seq, D_MODEL), BF16),
        scratch_shapes=[pltpu.VMEM((D_MODEL // LANES, seq, LANES), F32)],
        compiler_params=_params("parallel"),
    )(yt, w_glu)


def _fox_in_kernel(h_ref, g_ref, wqkv_ref, wf_ref, bf_ref, q_ref, k_ref, v_ref, qc_ref, kc_ref,
                   carry_s):
    tm = ROW_TILE

    @pl.when(pl.program_id(1) == 0)
    def _():
        carry_s[...] = jnp.zeros_like(carry_s)

    row = lax.broadcasted_iota(jnp.int32, (tm, tm), 0)
    col = lax.broadcasted_iota(jnp.int32, (tm, tm), 1)
    tri = (row >= col).astype(BF16)
    lane = lax.broadcasted_iota(jnp.int32, (tm, LANES), 1)
    grp = lane >> FOX_HEADS_SHIFT
    carry = carry_s[...]
    for t in range(TILES_PER_STEP):
        rows = slice(t * tm, (t + 1) * tm)
        xn = _rms(h_ref[rows, :], g_ref[...]).astype(BF16)
        log_f = _log_sigmoid(_dot(xn, wf_ref[...]) + bf_ref[...])
        hi, mid, lo = _split3(log_f)
        q = _dot(xn, wqkv_ref[:, 0:D_MODEL])
        q_ref[rows, :] = (q * (FOX_HEAD_DIM ** -0.5 * LOG2E)).astype(BF16)
        cum = _dot(tri, hi) + _dot(tri, mid) + _dot(tri, lo) + carry
        carry = cum[tm - 1:tm, :]
        k_ref[rows, :] = _dot(xn, wqkv_ref[:, D_MODEL:2 * D_MODEL]).astype(BF16)
        hi, mid, lo = _split3(cum * LOG2E)
        hi, mid, lo = hi.astype(F32), mid.astype(F32), lo.astype(F32)
        qc = jnp.where(grp == 0, hi, jnp.where(grp == 1, mid, jnp.where(grp == 2, lo,
             jnp.where(grp < FOX_BIAS_GROUPS, 1.0, 0.0))))
        kc = jnp.where(grp < 3, 1.0, jnp.where(grp == 3, -hi, jnp.where(grp == 4, -mid,
             jnp.where(grp == 5, -lo, 0.0))))
        qc_ref[rows, :] = qc.astype(BF16)
        kc_ref[rows, :] = kc.astype(BF16)
        v = _dot(xn, wqkv_ref[:, 2 * D_MODEL:3 * D_MODEL])
        for pair in range(FOX_HEADS // 2):
            vp = v[:, pair * LANES:(pair + 1) * LANES]
            v_ref[2 * pair, rows, :] = jnp.where(
                lane < FOX_HEAD_DIM, vp, jnp.where(lane == FOX_HEAD_DIM, 1.0, 0.0)).astype(BF16)
            v_ref[2 * pair + 1, rows, :] = jnp.where(
                lane >= FOX_HEAD_DIM, vp, jnp.where(lane == 0, 1.0, 0.0)).astype(BF16)
    carry_s[...] = carry


def _fox_flash_kernel(q_ref, qc_ref, k_ref, kc_ref, v_ref, o_ref, *, seq):
    t = FOX_TQ
    pair = pl.program_id(1)
    lane = lax.broadcasted_iota(jnp.int32, (t, LANES), 1)
    row = lax.broadcasted_iota(jnp.int32, (t, t), 0)
    col = lax.broadcasted_iota(jnp.int32, (t, t), 1)
    causal = row >= col
    own = (lane < FOX_HEAD_DIM, lane >= FOX_HEAD_DIM)
    mine = [((lane & (FOX_HEADS - 1)) == 2 * pair + half) & (lane < FOX_HEADS * FOX_BIAS_GROUPS)
            for half in range(2)]
    causal2 = jnp.concatenate([causal, causal], axis=0)
    for i in range(seq // t):
        qrows = slice(i * t, (i + 1) * t)
        q = q_ref[qrows, :].astype(F32)
        qc = qc_ref[qrows, :].astype(F32)
        qcat = jnp.concatenate(
            [jnp.concatenate([jnp.where(own[half], q, 0.0), jnp.where(mine[half], qc, 0.0)], axis=1)
             for half in range(2)], axis=0).astype(BF16)
        m = jnp.full((2 * t, 1), NEG_INF, F32)
        acc = jnp.zeros((2 * t, LANES), F32)
        for j in range(i + 1):
            krows = slice(j * t, (j + 1) * t)
            kcat = jnp.concatenate([k_ref[krows, :], kc_ref[krows, :]], axis=1)
            vcat = jnp.concatenate([v_ref[0, krows, :], v_ref[1, krows, :]], axis=1)
            s = _dot_nt(qcat, kcat)
            if j == i:
                s = jnp.where(causal2, s, NEG_INF)
            m_new = jnp.maximum(m, jnp.max(s, axis=-1, keepdims=True))
            p = jnp.exp2(s - m_new)
            pv = _dot(p.astype(BF16), vcat)
            pv = jnp.concatenate([pv[0:t, 0:LANES], pv[t:2 * t, LANES:2 * LANES]], axis=0)
            acc = jnp.exp2(m - m_new) * acc + pv
            m = m_new
        out_a = acc[0:t] / acc[0:t, FOX_HEAD_DIM:FOX_HEAD_DIM + 1]
        out_b = acc[t:2 * t] / acc[t:2 * t, 0:1]
        o_ref[qrows, :] = jnp.where(lane < FOX_HEAD_DIM, out_a, out_b).astype(BF16)


def _fox_mixer(h, gm, layer, wqkv, fox_layer, wf, bf, batch, seq):
    tm = ROW_TILE * TILES_PER_STEP
    nt = seq // tm
    m = batch * seq
    tok = lambda b, i: (b * nt + i, 0)
    q, k, v, qc, kc = pl.pallas_call(
        _fox_in_kernel,
        grid=(batch, nt),
        in_specs=[pl.BlockSpec((tm, D_MODEL), tok),
                  _layer((1, D_MODEL), layer), _layer(wqkv.shape[1:], fox_layer),
                  _resident(wf.shape), _resident((1, LANES))],
        out_specs=[pl.BlockSpec((tm, D_MODEL), tok), pl.BlockSpec((tm, D_MODEL), tok),
                   pl.BlockSpec((None, FOX_HEADS, tm, LANES), lambda b, i: (b, 0, i, 0)),
                   pl.BlockSpec((tm, LANES), tok), pl.BlockSpec((tm, LANES), tok)],
        out_shape=[jax.ShapeDtypeStruct((m, D_MODEL), BF16),
                   jax.ShapeDtypeStruct((m, D_MODEL), BF16),
                   jax.ShapeDtypeStruct((batch, FOX_HEADS, seq, LANES), BF16),
                   jax.ShapeDtypeStruct((m, LANES), BF16),
                   jax.ShapeDtypeStruct((m, LANES), BF16)],
        scratch_shapes=[pltpu.VMEM((1, LANES), F32)],
        compiler_params=_params("parallel", "arbitrary"),
    )(h, gm, wqkv, wf, bf)
    slab = pl.BlockSpec((seq, LANES), lambda b, p: (b, p))
    bias = pl.BlockSpec((seq, LANES), lambda b, p: (b, 0))
    return pl.pallas_call(
        functools.partial(_fox_flash_kernel, seq=seq),
        grid=(batch, FOX_HEADS // 2),
        in_specs=[slab, bias, slab, bias,
                  pl.BlockSpec((None, 2, seq, LANES), lambda b, p: (b, p, 0, 0))],
        out_specs=slab,
        out_shape=jax.ShapeDtypeStruct((m, D_MODEL), BF16),
        compiler_params=_params("parallel", "parallel"),
    )(q, qc, k, kc, v)


def _gla_in_kernel(h_ref, g_ref, w_ref, wgl_ref, wg2_ref, bg_ref, qd_ref, ki_ref, ku_ref, v_ref,
                   rs_ref, dec_ref):
    tm = ROW_TILE
    nch = tm // GLA_CHUNK
    tb = V7X_MXU_DEPTH
    row = lax.broadcasted_iota(jnp.int32, (tb, tb), 0)
    col = lax.broadcasted_iota(jnp.int32, (tb, tb), 1)
    tri = (((row >> GLA_CHUNK_SHIFT) == (col >> GLA_CHUNK_SHIFT))
           & (row >= col)).astype(BF16)
    crow = lax.broadcasted_iota(jnp.int32, (nch, tm), 0)
    ccol = lax.broadcasted_iota(jnp.int32, (nch, tm), 1)
    pick = (crow == (ccol >> GLA_CHUNK_SHIFT)).astype(BF16)
    for t in range(TILES_PER_STEP):
        rows = slice(t * tm, (t + 1) * tm)
        xn = _rms(h_ref[rows, :], g_ref[...]).astype(BF16)
        g_lr = _dot(xn, wgl_ref[...]).astype(BF16)
        log_a = _log_sigmoid(_dot(g_lr, wg2_ref[...]) + bg_ref[...]) / GLA_GATE_NORM
        hi = log_a.astype(BF16)
        lo = (log_a - hi.astype(F32)).astype(BF16)
        v_ref[rows, :] = _dot(xn, w_ref[:, 2 * GLA_KD:2 * GLA_KD + GLA_VD]).astype(BF16)
        bcum = jnp.concatenate(
            [_dot(tri, hi[r0:r0 + tb, :]) + _dot(tri, lo[r0:r0 + tb, :]) for r0 in range(0, tm, tb)],
            axis=0)
        b_last = _dot(pick, hi) + _dot(pick, lo)
        dec_ref[t * nch:(t + 1) * nch, :] = jnp.exp(b_last)
        r = _dot(xn, w_ref[:, 2 * GLA_KD + GLA_VD:2 * GLA_KD + 2 * GLA_VD])
        rs_ref[rows, :] = (r * jax.nn.sigmoid(r)).astype(BF16)
        b_last_rows = jnp.broadcast_to(
            b_last[:, None, :], (nch, GLA_CHUNK, GLA_KD)).reshape(tm, GLA_KD)
        q = _dot(xn, w_ref[:, 0:GLA_KD])
        qd_ref[rows, :] = (q * GLA_DK ** -0.5 * jnp.exp(bcum)).astype(BF16)
        k = _dot(xn, w_ref[:, GLA_KD:2 * GLA_KD])
        ki_ref[rows, :] = (k * jnp.exp(-bcum)).astype(BF16)
        ku_ref[rows, :] = (k * jnp.exp(b_last_rows - bcum)).astype(BF16)


def _gla_core_kernel(qd_ref, ki_ref, ku_ref, v_ref, rs_ref, dec_ref, gn_ref, o_ref, *, seq):
    c = GLA_CHUNK
    nb = GLA_BLOCK_CHUNKS
    blk = nb * c
    row = lax.broadcasted_iota(jnp.int32, (blk, blk), 0)
    col = lax.broadcasted_iota(jnp.int32, (blk, blk), 1)
    causal = (((row >> GLA_CHUNK_SHIFT) == (col >> GLA_CHUNK_SHIFT))
              & (row >= col))
    gn = gn_ref[...]

    def block(rows, dec, states):
        new = []
        for hd in range(GLA_HEADS):
            kk = slice(hd * GLA_DK, (hd + 1) * GLA_DK)
            vv = slice(hd * GLA_DV, (hd + 1) * GLA_DV)
            qd, ki, ku, v = qd_ref[rows, kk], ki_ref[rows, kk], ku_ref[rows, kk], v_ref[rows, vv]
            att = jnp.where(causal, _dot_nt(qd, ki), 0.0)
            o = _dot(att.astype(BF16), v)
            st = states[hd]
            inter = []
            for j in range(nb):
                cr = slice(j * c, (j + 1) * c)
                inter.append(_dot_nt(qd[cr], st.astype(BF16)))
                st = st * dec[j:j + 1, kk] + _dot_tn(v[cr], ku[cr])
            o = o + jnp.concatenate(inter, axis=0)
            o = o * lax.rsqrt(jnp.mean(o * o, axis=-1, keepdims=True) + EPS)
            o_ref[rows, vv] = (o * gn[:, vv] * rs_ref[rows, vv].astype(F32)).astype(BF16)
            new.append(st)
        return tuple(new)

    def two_blocks(i, states):
        dec = dec_ref[pl.ds(pl.multiple_of(i * 2 * nb, 2 * nb), 2 * nb), :]
        for sub in range(2):
            rows = pl.ds(pl.multiple_of(i * 2 * blk + sub * blk, blk), blk)
            states = block(rows, dec[sub * nb:(sub + 1) * nb], states)
        return states

    lax.fori_loop(0, seq // (2 * blk), two_blocks,
                  (jnp.zeros((GLA_DV, GLA_DK), F32),) * GLA_HEADS)


def _gla_mixer(h, gm, layer, w_main, gla_layer, wgl, wg2, bg, gn, batch, seq):
    tm = ROW_TILE * TILES_PER_STEP
    m = batch * seq
    nch = tm // GLA_CHUNK
    row = lambda i: (i, 0)
    tok = lambda width, dtype: jax.ShapeDtypeStruct((m, width), dtype)
    qd, ki, ku, v, rs, dec = pl.pallas_call(
        _gla_in_kernel,
        grid=(m // tm,),
        in_specs=[pl.BlockSpec((tm, D_MODEL), row), _layer((1, D_MODEL), layer),
                  _layer(w_main.shape[1:], gla_layer), _resident(wgl.shape), _resident(wg2.shape),
                  _resident((1, GLA_KD))],
        out_specs=[pl.BlockSpec((tm, GLA_KD), row), pl.BlockSpec((tm, GLA_KD), row),
                   pl.BlockSpec((tm, GLA_KD), row), pl.BlockSpec((tm, GLA_VD), row),
                   pl.BlockSpec((tm, GLA_VD), row), pl.BlockSpec((nch, GLA_KD), row)],
        out_shape=[tok(GLA_KD, BF16), tok(GLA_KD, BF16), tok(GLA_KD, BF16), tok(GLA_VD, BF16),
                   tok(GLA_VD, BF16), jax.ShapeDtypeStruct((m // GLA_CHUNK, GLA_KD), F32)],
        compiler_params=_params("parallel"),
    )(h, gm, w_main, wgl, wg2, bg)
    blk = lambda rows, width: pl.BlockSpec((rows, width), lambda b: (b, 0))
    return pl.pallas_call(
        functools.partial(_gla_core_kernel, seq=seq),
        grid=(batch,),
        in_specs=[blk(seq, GLA_KD), blk(seq, GLA_KD), blk(seq, GLA_KD), blk(seq, GLA_VD),
                  blk(seq, GLA_VD), blk(seq // GLA_CHUNK, GLA_KD), _resident((1, GLA_VD))],
        out_specs=blk(seq, GLA_VD),
        out_shape=jax.ShapeDtypeStruct((m, GLA_VD), BF16),
        compiler_params=_params("parallel"),
    )(qd, ki, ku, v, rs, dec, gn)


def _pad_lanes(w, width):
    return jnp.pad(w, ((0, 0), (0, width - w.shape[1])))


def kernel(x, p, norm_mix, norm_mlp, norm_ple, s5_w_in, s5_lam_re, s5_lam_im, s5_log_dt, s5_b_re,
           s5_b_im, s5_c_re, s5_c_im, s5_d, s5_w_glu, s5_w_out, fox_w_in, fox_b_f, fox_w_out,
           gla_w_in, gla_w_g2, gla_b_g, gla_norm, gla_w_out, mlp_w1, mlp_w2, ple_proj, ple_gate,
           final_norm):
    batch, seq, _ = x.shape
    depth = p.shape[0]
    m = batch * seq
    h = x.reshape(m, D_MODEL)
    p_all = p.reshape(depth, m, PLE_DIM)
    gf = final_norm.reshape(1, D_MODEL)
    gm_mix = norm_mix.reshape(depth, 1, D_MODEL)
    gm_mlp = norm_mlp.reshape(depth, 1, D_MODEL)
    gm_ple = norm_ple.reshape(depth, 1, D_MODEL)
    w1, w2 = _to_bf16(mlp_w1), _to_bf16(mlp_w2)
    wpg, wpp = _to_bf16(ple_gate), _to_bf16(ple_proj)
    wo_s5, wo_fox, wo_gla = _to_bf16(s5_w_out), _to_bf16(fox_w_out), _to_bf16(gla_w_out)
    s5_in, s5_glu = _to_bf16(s5_w_in), _to_bf16(s5_w_glu)
    fox_qkv = _to_bf16(fox_w_in, cols=3 * D_MODEL)
    gla_main = _to_bf16(gla_w_in, cols=2 * GLA_KD + 2 * GLA_VD)
    s5_mats = _s5_prep(s5_lam_re, s5_lam_im, s5_log_dt, s5_b_re, s5_b_im, s5_c_re, s5_c_im, s5_d)
    for i in range(depth):
        mixer, j = i % 3, i // 3
        if mixer == 0:
            mix = _s5_mixer(h, gm_mix, i, s5_in, s5_mats, s5_glu, j, batch, seq)
            wo = wo_s5
        elif mixer == 1:
            w_f = fox_w_in[j, :, 3 * D_MODEL:]
            wf = _pad_lanes(jnp.tile(w_f, (1, FOX_BIAS_GROUPS)), LANES).astype(BF16)
            bf = _pad_lanes(jnp.tile(fox_b_f[j].reshape(1, FOX_HEADS), (1, FOX_BIAS_GROUPS)), LANES)
            mix = _fox_mixer(h, gm_mix, i, fox_qkv, j, wf, bf, batch, seq)
            wo = wo_fox
        else:
            wgl = _pad_lanes(gla_w_in[j, :, 2 * GLA_KD + 2 * GLA_VD:], LANES).astype(BF16)
            wg2 = jnp.pad(gla_w_g2[j], ((0, LANES - GLA_RANK), (0, 0))).astype(BF16)
            mix = _gla_mixer(h, gm_mix, i, gla_main, j, wgl, wg2, gla_b_g[j].reshape(1, GLA_KD),
                             gla_norm[j].reshape(1, GLA_VD), batch, seq)
            wo = wo_gla
        h = _tail(h, mix, wo, j, gm_mlp, w1, w2, gm_ple, wpg, p_all, wpp, gf, layer=i,
                  final=(i == depth - 1))
    return h.reshape(batch, seq, D_MODEL)
```

```python
import functools

import jax
import jax.numpy as jnp
from jax import lax
from jax.experimental import pallas as pl
from jax.experimental.pallas import tpu as pltpu

F32 = jnp.float32
BF16 = jnp.bfloat16

D_MODEL = 1024
D_FF = 4 * D_MODEL
PLE_DIM = 256
EPS = 1e-6
NEG_INF = -1e30
LOG2E = 1.4426950408889634

V7X_VMEM_BYTES = 64 * 1024 * 1024
VMEM_LIMIT_BYTES = V7X_VMEM_BYTES - 8 * 1024 * 1024
LANES = 128
V7X_MXU_DEPTH = 256

S5_GROUP = 16
S5_GROUPS = D_MODEL // S5_GROUP
S5_STATE = 64
S5_GROUP_SHIFT = S5_GROUP.bit_length() - 1
S5_L = 16
S5_K = S5_L * S5_GROUP
S5_PREP_GROUPS = 16

FOX_HEADS = 16
FOX_HEAD_DIM = D_MODEL // FOX_HEADS
FOX_HEADS_SHIFT = FOX_HEADS.bit_length() - 1
FOX_TQ = 512
FOX_BIAS_GROUPS = 6

GLA_HEADS = 4
GLA_KD = D_MODEL // 2
GLA_VD = D_MODEL
GLA_DK = GLA_KD // GLA_HEADS
GLA_DV = GLA_VD // GLA_HEADS
GLA_RANK = 16
GLA_GATE_NORM = 16.0
GLA_CHUNK = 64
GLA_CHUNK_SHIFT = GLA_CHUNK.bit_length() - 1
GLA_BLOCK_CHUNKS = 4

ROW_TILE = 512
TILES_PER_STEP = 2
TAIL_TILES_PER_STEP = 2
CAST_BLOCK_BYTES = 4 * 1024 * 1024


assert S5_GROUP == 1 << S5_GROUP_SHIFT and FOX_HEADS == 1 << FOX_HEADS_SHIFT
assert GLA_CHUNK == 1 << GLA_CHUNK_SHIFT


def _params(*sem):
    return pltpu.CompilerParams(dimension_semantics=sem, vmem_limit_bytes=VMEM_LIMIT_BYTES)


def _resident(shape):
    zeros = (0,) * len(shape)
    return pl.BlockSpec(shape, lambda *_: zeros, pipeline_mode=pl.Buffered(1))


def _layer(shape, layer):
    index = (layer,) + (0,) * len(shape)
    return pl.BlockSpec((None,) + tuple(shape), lambda *_: index, pipeline_mode=pl.Buffered(1))


def _cast_kernel(x_ref, o_ref):
    o_ref[...] = x_ref[:, 0:o_ref.shape[1]].astype(BF16)


def _to_bf16(w, cols=None):
    layers, k, n = w.shape
    cols = n if cols is None else cols
    tk = 8
    while tk * 2 <= k and tk * 2 * n * 4 <= CAST_BLOCK_BYTES:
        tk *= 2
    assert k % tk == 0
    return pl.pallas_call(
        _cast_kernel, grid=(layers, k // tk),
        in_specs=[pl.BlockSpec((None, tk, n), lambda l, i: (l, i, 0))],
        out_specs=pl.BlockSpec((None, tk, cols), lambda l, i: (l, i, 0)),
        out_shape=jax.ShapeDtypeStruct((layers, k, cols), BF16),
        compiler_params=_params("parallel", "parallel"),
    )(w)


def _rms(x, g):
    ms = jnp.mean(x * x, axis=-1, keepdims=True)
    return x * lax.rsqrt(ms + EPS) * g


def _log_sigmoid(x):
    return jnp.minimum(x, 0.0) - jnp.log1p(jnp.exp(-jnp.abs(x)))


def _dot(a, b):
    return jnp.dot(a, b, preferred_element_type=F32)


def _dot_nt(a, b):
    return lax.dot_general(a, b, (((1,), (1,)), ((), ())), preferred_element_type=F32)


def _dot_tn(a, b):
    return lax.dot_general(a, b, (((0,), (0,)), ((), ())), preferred_element_type=F32)


def _split3(x):
    hi = x.astype(BF16)
    r1 = x - hi.astype(F32)
    mid = r1.astype(BF16)
    lo = (r1 - mid.astype(F32)).astype(BF16)
    return hi, mid, lo


def _tail_kernel(h_ref, mix_ref, wo_ref, gm_ref, w1_ref, w2_ref, gp_ref, wpg_ref, p_ref,
                 wpp_ref, gf_ref, o_ref, *, final):
    ff_chunk = D_FF // 4
    for t in range(TAIL_TILES_PER_STEP):
        rows = slice(t * ROW_TILE, (t + 1) * ROW_TILE)
        h = h_ref[rows, :] + _dot(mix_ref[rows, :], wo_ref[...])
        xn = _rms(h, gm_ref[...]).astype(BF16)
        mlp = None
        for c in range(4):
            a = _dot(xn, w1_ref[:, c * ff_chunk:(c + 1) * ff_chunk])
            a = jnp.square(jnp.maximum(a, 0.0)).astype(BF16)
            part = _dot(a, w2_ref[c * ff_chunk:(c + 1) * ff_chunk, :])
            mlp = part if mlp is None else mlp + part
        h = h + mlp
        xg = _rms(h, gp_ref[...]).astype(BF16)
        gate = jax.nn.sigmoid(_dot(xg, wpg_ref[...]))
        pe = _dot(p_ref[rows, :].astype(BF16), wpp_ref[...])
        h = h + pe * gate
        if final:
            h = _rms(h, gf_ref[...])
        o_ref[rows, :] = h


def _tail(h, mix, wo, wo_layer, gm, w1, w2, gp, wpg, p_all, wpp, gf, layer, final):
    m = h.shape[0]
    tm = ROW_TILE * TAIL_TILES_PER_STEP
    row = lambda i: (i, 0)
    return pl.pallas_call(
        functools.partial(_tail_kernel, final=final),
        grid=(m // tm,),
        in_specs=[
            pl.BlockSpec((tm, D_MODEL), row),
            pl.BlockSpec((tm, mix.shape[1]), row),
            _layer(wo.shape[1:], wo_layer),
            _layer((1, D_MODEL), layer),
            _layer(w1.shape[1:], layer),
            _layer(w2.shape[1:], layer),
            _layer((1, D_MODEL), layer),
            _layer(wpg.shape[1:], layer),
            pl.BlockSpec((None, tm, PLE_DIM), lambda i: (layer, i, 0)),
            _layer(wpp.shape[1:], layer),
            _resident((1, D_MODEL)),
        ],
        out_specs=pl.BlockSpec((tm, D_MODEL), row),
        out_shape=jax.ShapeDtypeStruct((m, D_MODEL), F32),
        compiler_params=_params("parallel"),
    )(h, mix, wo, gm, w1, w2, gp, wpg, p_all, wpp, gf)


def _cmul(ar, ai, br, bi):
    return ar * br - ai * bi, ar * bi + ai * br


def _cpow_table(ar, ai, count):
    pr, pi = jnp.ones_like(ar), jnp.zeros_like(ar)
    table = [(pr, pi)]
    for _ in range(count):
        pr, pi = _cmul(pr, pi, ar, ai)
        table.append((pr, pi))
    return table


def _s5_discretize(lr, li, ld):
    dt = jnp.exp(ld)
    mag = jnp.exp(lr * dt)
    ar = mag * jnp.cos(li * dt)
    ai = mag * jnp.sin(li * dt)
    den = lr * lr + li * li
    nr = ar - 1.0
    coef_re = (nr * lr + ai * li) / den
    coef_im = (ai * lr - nr * li) / den
    return ar, ai, coef_re, coef_im


def _s5_prep_group(col_pow, cfr, cfi, row_pow, b_re, b_im, cr, ci, d):
    n, k, c = S5_STATE, S5_K, S5_GROUP
    hp = lax.Precision.HIGHEST
    rep = (lax.broadcasted_iota(jnp.int32, (c, k), 0)
           == (lax.broadcasted_iota(jnp.int32, (c, k), 1) & (c - 1))).astype(F32)
    brt = jnp.dot(b_re, rep, precision=hp, preferred_element_type=F32)
    bit = jnp.dot(b_im, rep, precision=hp, preferred_element_type=F32)
    bbr = cfr * brt - cfi * bit
    bbi = cfr * bit + cfi * brt
    grp = lax.broadcasted_iota(jnp.int32, (n, k), 1) >> S5_GROUP_SHIFT
    pr = jnp.broadcast_to(col_pow[0][0], (n, k))
    pi = jnp.broadcast_to(col_pow[0][1], (n, k))
    for i in range(S5_L - 1):
        pr = jnp.where(grp == i, col_pow[S5_L - 1 - i][0], pr)
        pi = jnp.where(grp == i, col_pow[S5_L - 1 - i][1], pi)
    er, ei = _cmul(pr, pi, bbr, bbi)
    krev = (jnp.dot(cr, er, precision=hp, preferred_element_type=F32)
            - jnp.dot(ci, ei, precision=hp, preferred_element_type=F32))
    row = lax.broadcasted_iota(jnp.int32, (c, k), 0)
    col = lax.broadcasted_iota(jnp.int32, (c, k), 1)
    krev = krev + jnp.where(col == row + (k - c), d, 0.0)
    m_rows = []
    for t in range(S5_L):
        width = c * (t + 1)
        shifted = krev if width == k else pltpu.roll(krev, width, axis=1)
        m_rows.append(jnp.where(col < width, shifted, 0.0))
    f_re = [cr * row_pow[t + 1][0] - ci * row_pow[t + 1][1] for t in range(S5_L)]
    f_im = [-(cr * row_pow[t + 1][1] + ci * row_pow[t + 1][0]) for t in range(S5_L)]
    return m_rows, er, ei, f_re, f_im


def _s5_prep_kernel(lrc_ref, lic_ref, ldc_ref, lrr_ref, lir_ref, ldr_ref, br_ref, bi_ref, cr_ref,
                    ci_ref, d_ref, m_ref, e_ref, f_ref, al_ref):
    n, c = S5_STATE, S5_GROUP
    ar, ai, cfr, cfi = _s5_discretize(lrc_ref[...], lic_ref[...], ldc_ref[...])
    col_pow = _cpow_table(ar, ai, S5_L - 1)
    ar, ai, _, _ = _s5_discretize(lrr_ref[...], lir_ref[...], ldr_ref[...])
    row_pow = _cpow_table(ar, ai, S5_L)
    al_ref[:, 0, :] = row_pow[S5_L][0]
    al_ref[:, 1, :] = row_pow[S5_L][1]
    for gi in range(S5_PREP_GROUPS):
        colg = [(pr[:, gi:gi + 1], pi[:, gi:gi + 1]) for pr, pi in col_pow]
        rowg = [(pr[gi:gi + 1, :], pi[gi:gi + 1, :]) for pr, pi in row_pow]
        m_rows, er, ei, f_re, f_im = _s5_prep_group(
            colg, cfr[:, gi:gi + 1], cfi[:, gi:gi + 1], rowg, br_ref[gi], bi_ref[gi], cr_ref[gi],
            ci_ref[gi], d_ref[gi])
        e_ref[gi, 0:n, :] = er.astype(BF16)
        e_ref[gi, n:2 * n, :] = ei.astype(BF16)
        for t in range(S5_L):
            m_ref[gi, c * t:c * (t + 1), :] = m_rows[t].astype(BF16)
            f_ref[gi, c * t:c * (t + 1), 0:n] = f_re[t].astype(BF16)
            f_ref[gi, c * t:c * (t + 1), n:2 * n] = f_im[t].astype(BF16)


def _s5_prep(lam_re, lam_im, log_dt, b_re, b_im, c_re, c_im, d_skip):
    n, k, c = S5_STATE, S5_K, S5_GROUP
    g = lam_re.shape[0] * S5_GROUPS
    gb = S5_PREP_GROUPS
    steps = g // gb
    blk = lambda *shape: pl.BlockSpec((gb,) + shape, lambda i: (i,) + (0,) * len(shape))
    one = lambda *shape: pl.BlockSpec((None,) + shape, lambda i: (i,) + (0,) * len(shape))
    lr, li = lam_re.reshape(steps, gb, n), lam_im.reshape(steps, gb, n)
    m, e, f, al = pl.pallas_call(
        _s5_prep_kernel,
        grid=(steps,),
        in_specs=[one(n, gb), one(n, gb), one(1, gb), one(gb, n), one(gb, n), one(gb, 1),
                  blk(n, c), blk(n, c), blk(c, n), blk(c, n), blk(c, 1)],
        out_specs=[blk(k, k), blk(2 * n, k), blk(k, 2 * n), blk(2, n)],
        out_shape=[jax.ShapeDtypeStruct((g, k, k), BF16),
                   jax.ShapeDtypeStruct((g, 2 * n, k), BF16),
                   jax.ShapeDtypeStruct((g, k, 2 * n), BF16),
                   jax.ShapeDtypeStruct((g, 2, n), F32)],
        compiler_params=_params("parallel"),
    )(lr.transpose(0, 2, 1), li.transpose(0, 2, 1), log_dt.reshape(steps, 1, gb), lr, li,
      log_dt.reshape(steps, gb, 1),
      b_re.reshape(g, n, c), b_im.reshape(g, n, c), c_re.reshape(g, c, n), c_im.reshape(g, c, n),
      d_skip.reshape(g, c, 1))
    al = al.reshape(g // 2, 2, 2, n).transpose(0, 2, 1, 3).reshape(g // 2, 2, 1, 2 * n)
    return m, e, f, al


def _s5_in_kernel(h_ref, g_ref, w_ref, ut_ref, u_s, *, seq):
    for r in range(seq // ROW_TILE):
        rows = slice(r * ROW_TILE, (r + 1) * ROW_TILE)
        xn = _rms(h_ref[rows, :], g_ref[...]).astype(BF16)
        u = _dot(xn, w_ref[...])
        for lt in range(D_MODEL // LANES):
            u_s[lt, rows, :] = u[:, lt * LANES:(lt + 1) * LANES]
    nck = seq // S5_L
    gpl = LANES // S5_GROUP
    for j in range(S5_L):
        for lt in range(D_MODEL // LANES):
            xj = u_s[lt, pl.ds(j, nck, stride=S5_L), :]
            ut_ref[gpl * lt:gpl * (lt + 1), S5_GROUP * j:S5_GROUP * (j + 1), :] = (
                xj.T.reshape(gpl, S5_GROUP, nck).astype(BF16))


def _s5_core_kernel(ut_ref, m_ref, e_ref, f_ref, al_ref, yt_ref, sr_s, si_s, *, nck, batch):
    n = S5_STATE
    u = [ut_ref[g] for g in range(2)]
    s = [_dot(e_ref[g], u[g]) for g in range(2)]
    s_re = jnp.concatenate([s[0][0:n, :], s[1][0:n, :]], axis=0)
    s_im = jnp.concatenate([s[0][n:2 * n, :], s[1][n:2 * n, :]], axis=0)
    for b in range(batch):
        cols = slice(b * nck, (b + 1) * nck)
        sr_s[pl.ds(b, nck, stride=batch), :] = s_re[:, cols].T
        si_s[pl.ds(b, nck, stride=batch), :] = s_im[:, cols].T
    ar, ai = al_ref[0], al_ref[1]

    y_intra = [_dot(m_ref[g], u[g]) for g in range(2)]
    pr = pi = jnp.zeros((batch, 2 * n), F32)
    for k in range(nck):
        rows = slice(k * batch, (k + 1) * batch)
        cr, ci = sr_s[rows, :], si_s[rows, :]
        sr_s[rows, :] = pr
        si_s[rows, :] = pi
        pr, pi = ar * pr - ai * pi + cr, ar * pi + ai * pr + ci
    p_re = jnp.concatenate([sr_s[pl.ds(b, nck, stride=batch), :].T for b in range(batch)], axis=1)
    p_im = jnp.concatenate([si_s[pl.ds(b, nck, stride=batch), :].T for b in range(batch)], axis=1)
    for g in range(2):
        sp = jnp.concatenate([p_re[g * n:(g + 1) * n, :], p_im[g * n:(g + 1) * n, :]], axis=0)
        yt_ref[g] = (y_intra[g] + _dot(f_ref[g], sp.astype(BF16))).astype(BF16)


def _s5_post_kernel(yt_ref, wg_ref, mix_ref, y_s, *, seq):
    nck = seq // S5_L
    gpl = LANES // S5_GROUP
    nlt = D_MODEL // LANES
    for t in range(S5_L):
        for lt in range(nlt):
            piece = yt_ref[gpl * lt:gpl * (lt + 1), S5_GROUP * t:S5_GROUP * (t + 1), :]
            y_s[lt, pl.ds(t, nck, stride=S5_L), :] = piece.reshape(LANES, nck).T.astype(F32)
    for r in range(seq // ROW_TILE):
        rows = slice(r * ROW_TILE, (r + 1) * ROW_TILE)
        y = jnp.concatenate([y_s[lt, rows, :] for lt in range(nlt)], axis=1)
        a = jax.nn.gelu(y).astype(BF16)
        gate = jax.nn.sigmoid(_dot(a, wg_ref[:, D_MODEL:2 * D_MODEL]))
        mix_ref[rows, :] = (_dot(a, wg_ref[:, 0:D_MODEL]) * gate).astype(BF16)


def _s5_mixer(h, gm, layer, w_in, mats, w_glu, s5_layer, batch, seq):
    m_t, e_t, f_t, al = mats
    nck = seq // S5_L
    lanes = batch * nck
    assert nck % LANES == 0 and nck & (nck - 1) == 0
    ut = pl.pallas_call(
        functools.partial(_s5_in_kernel, seq=seq),
        grid=(batch,),
        in_specs=[pl.BlockSpec((seq, D_MODEL), lambda b: (b, 0)),
                  _layer((1, D_MODEL), layer), _layer(w_in.shape[1:], s5_layer)],
        out_specs=pl.BlockSpec((S5_GROUPS, S5_K, nck), lambda b: (0, 0, b)),
        out_shape=jax.ShapeDtypeStruct((S5_GROUPS, S5_K, lanes), BF16),
        scratch_shapes=[pltpu.VMEM((D_MODEL // LANES, seq, LANES), F32)],
        compiler_params=_params("parallel"),
    )(h, gm, w_in)
    p0 = s5_layer * (S5_GROUPS // 2)
    act = lambda *shape: pl.BlockSpec((2,) + shape, lambda g: (g,) + (0,) * len(shape))
    par = lambda *shape: pl.BlockSpec((2,) + shape, lambda g: (p0 + g,) + (0,) * len(shape))
    yt = pl.pallas_call(
        functools.partial(_s5_core_kernel, nck=nck, batch=batch),
        grid=(S5_GROUPS // 2,),
        in_specs=[act(S5_K, lanes), par(S5_K, S5_K), par(2 * S5_STATE, S5_K),
                  par(S5_K, 2 * S5_STATE),
                  pl.BlockSpec((None, 2, 1, 2 * S5_STATE), lambda g: (p0 + g, 0, 0, 0))],
        out_specs=act(S5_K, lanes),
        out_shape=jax.ShapeDtypeStruct((S5_GROUPS, S5_K, lanes), BF16),
        scratch_shapes=[pltpu.VMEM((lanes, 2 * S5_STATE), F32),
                        pltpu.VMEM((lanes, 2 * S5_STATE), F32)],
        compiler_params=_params("parallel"),
    )(ut, m_t, e_t, f_t, al)
    return pl.pallas_call(
        functools.partial(_s5_post_kernel, seq=seq),
        grid=(batch,),
        in_specs=[pl.BlockSpec((S5_GROUPS, S5_K, nck), lambda b: (0, 0, b)),
                  _layer(w_glu.shape[1:], s5_layer)],
        out_specs=pl.BlockSpec((seq, D_MODEL), lambda b: (b, 0)),
        out_shape=jax.ShapeDtypeStruct((batch * seq, D_MODEL), BF16),
        scratch_shapes=[pltpu.VMEM((D_MODEL // LANES, seq, LANES), F32)],
        compiler_params=_params("parallel"),
    )(yt, w_glu)


def _fox_in_kernel(h_ref, g_ref, wqkv_ref, wf_ref, bf_ref, q_ref, k_ref, v_ref, qc_ref, kc_ref,
                   carry_s):
    tm = ROW_TILE

    @pl.when(pl.program_id(1) == 0)
    def _():
        carry_s[...] = jnp.zeros_like(carry_s)

    row = lax.broadcasted_iota(jnp.int32, (tm, tm), 0)
    col = lax.broadcasted_iota(jnp.int32, (tm, tm), 1)
    tri = (row >= col).astype(BF16)
    lane = lax.broadcasted_iota(jnp.int32, (tm, LANES), 1)
    grp = lane >> FOX_HEADS_SHIFT
    carry = carry_s[...]
    for t in range(TILES_PER_STEP):
        rows = slice(t * tm, (t + 1) * tm)
        xn = _rms(h_ref[rows, :], g_ref[...]).astype(BF16)
        log_f = _log_sigmoid(_dot(xn, wf_ref[...]) + bf_ref[...])
        hi, mid, lo = _split3(log_f)
        q = _dot(xn, wqkv_ref[:, 0:D_MODEL])
        q_ref[rows, :] = (q * (FOX_HEAD_DIM ** -0.5 * LOG2E)).astype(BF16)
        cum = _dot(tri, hi) + _dot(tri, mid) + _dot(tri, lo) + carry
        carry = cum[tm - 1:tm, :]
        k_ref[rows, :] = _dot(xn, wqkv_ref[:, D_MODEL:2 * D_MODEL]).astype(BF16)
        hi, mid, lo = _split3(cum * LOG2E)
        hi, mid, lo = hi.astype(F32), mid.astype(F32), lo.astype(F32)
        qc = jnp.where(grp == 0, hi, jnp.where(grp == 1, mid, jnp.where(grp == 2, lo,
             jnp.where(grp < FOX_BIAS_GROUPS, 1.0, 0.0))))
        kc = jnp.where(grp < 3, 1.0, jnp.where(grp == 3, -hi, jnp.where(grp == 4, -mid,
             jnp.where(grp == 5, -lo, 0.0))))
        qc_ref[rows, :] = qc.astype(BF16)
        kc_ref[rows, :] = kc.astype(BF16)
        v = _dot(xn, wqkv_ref[:, 2 * D_MODEL:3 * D_MODEL])
        for pair in range(FOX_HEADS // 2):
            vp = v[:, pair * LANES:(pair + 1) * LANES]
            v_ref[2 * pair, rows, :] = jnp.where(
                lane < FOX_HEAD_DIM, vp, jnp.where(lane == FOX_HEAD_DIM, 1.0, 0.0)).astype(BF16)
            v_ref[2 * pair + 1, rows, :] = jnp.where(
                lane >= FOX_HEAD_DIM, vp, jnp.where(lane == 0, 1.0, 0.0)).astype(BF16)
    carry_s[...] = carry


def _fox_flash_kernel(q_ref, qc_ref, k_ref, kc_ref, v_ref, o_ref, *, seq):
    t = FOX_TQ
    pair = pl.program_id(1)
    lane = lax.broadcasted_iota(jnp.int32, (t, LANES), 1)
    row = lax.broadcasted_iota(jnp.int32, (t, t), 0)
    col = lax.broadcasted_iota(jnp.int32, (t, t), 1)
    causal = row >= col
    own = (lane < FOX_HEAD_DIM, lane >= FOX_HEAD_DIM)
    mine = [((lane & (FOX_HEADS - 1)) == 2 * pair + half) & (lane < FOX_HEADS * FOX_BIAS_GROUPS)
            for half in range(2)]
    causal2 = jnp.concatenate([causal, causal], axis=0)
    for i in range(seq // t):
        qrows = slice(i * t, (i + 1) * t)
        q = q_ref[qrows, :].astype(F32)
        qc = qc_ref[qrows, :].astype(F32)
        qcat = jnp.concatenate(
            [jnp.concatenate([jnp.where(own[half], q, 0.0), jnp.where(mine[half], qc, 0.0)], axis=1)
             for half in range(2)], axis=0).astype(BF16)
        m = jnp.full((2 * t, 1), NEG_INF, F32)
        acc = jnp.zeros((2 * t, LANES), F32)
        for j in range(i + 1):
            krows = slice(j * t, (j + 1) * t)
            kcat = jnp.concatenate([k_ref[krows, :], kc_ref[krows, :]], axis=1)
            vcat = jnp.concatenate([v_ref[0, krows, :], v_ref[1, krows, :]], axis=1)
            s = _dot_nt(qcat, kcat)
            if j == i:
                s = jnp.where(causal2, s, NEG_INF)
            m_new = jnp.maximum(m, jnp.max(s, axis=-1, keepdims=True))
            p = jnp.exp2(s - m_new)
            pv = _dot(p.astype(BF16), vcat)
            pv = jnp.concatenate([pv[0:t, 0:LANES], pv[t:2 * t, LANES:2 * LANES]], axis=0)
            acc = jnp.exp2(m - m_new) * acc + pv
            m = m_new
        out_a = acc[0:t] / acc[0:t, FOX_HEAD_DIM:FOX_HEAD_DIM + 1]
        out_b = acc[t:2 * t] / acc[t:2 * t, 0:1]
        o_ref[qrows, :] = jnp.where(lane < FOX_HEAD_DIM, out_a, out_b).astype(BF16)


def _fox_mixer(h, gm, layer, wqkv, fox_layer, wf, bf, batch, seq):
    tm = ROW_TILE * TILES_PER_STEP
    nt = seq // tm
    m = batch * seq
    tok = lambda b, i: (b * nt + i, 0)
    q, k, v, qc, kc = pl.pallas_call(
        _fox_in_kernel,
        grid=(batch, nt),
        in_specs=[pl.BlockSpec((tm, D_MODEL), tok),
                  _layer((1, D_MODEL), layer), _layer(wqkv.shape[1:], fox_layer),
                  _resident(wf.shape), _resident((1, LANES))],
        out_specs=[pl.BlockSpec((tm, D_MODEL), tok), pl.BlockSpec((tm, D_MODEL), tok),
                   pl.BlockSpec((None, FOX_HEADS, tm, LANES), lambda b, i: (b, 0, i, 0)),
                   pl.BlockSpec((tm, LANES), tok), pl.BlockSpec((tm, LANES), tok)],
        out_shape=[jax.ShapeDtypeStruct((m, D_MODEL), BF16),
                   jax.ShapeDtypeStruct((m, D_MODEL), BF16),
                   jax.ShapeDtypeStruct((batch, FOX_HEADS, seq, LANES), BF16),
                   jax.ShapeDtypeStruct((m, LANES), BF16),
                   jax.ShapeDtypeStruct((m, LANES), BF16)],
        scratch_shapes=[pltpu.VMEM((1, LANES), F32)],
        compiler_params=_params("parallel", "arbitrary"),
    )(h, gm, wqkv, wf, bf)
    slab = pl.BlockSpec((seq, LANES), lambda b, p: (b, p))
    bias = pl.BlockSpec((seq, LANES), lambda b, p: (b, 0))
    return pl.pallas_call(
        functools.partial(_fox_flash_kernel, seq=seq),
        grid=(batch, FOX_HEADS // 2),
        in_specs=[slab, bias, slab, bias,
                  pl.BlockSpec((None, 2, seq, LANES), lambda b, p: (b, p, 0, 0))],
        out_specs=slab,
        out_shape=jax.ShapeDtypeStruct((m, D_MODEL), BF16),
        compiler_params=_params("parallel", "parallel"),
    )(q, qc, k, kc, v)


def _gla_in_kernel(h_ref, g_ref, w_ref, wgl_ref, wg2_ref, bg_ref, qd_ref, ki_ref, ku_ref, v_ref,
                   rs_ref, dec_ref):
    tm = ROW_TILE
    nch = tm // GLA_CHUNK
    tb = V7X_MXU_DEPTH
    row = lax.broadcasted_iota(jnp.int32, (tb, tb), 0)
    col = lax.broadcasted_iota(jnp.int32, (tb, tb), 1)
    tri = (((row >> GLA_CHUNK_SHIFT) == (col >> GLA_CHUNK_SHIFT))
           & (row >= col)).astype(BF16)
    crow = lax.broadcasted_iota(jnp.int32, (nch, tm), 0)
    ccol = lax.broadcasted_iota(jnp.int32, (nch, tm), 1)
    pick = (crow == (ccol >> GLA_CHUNK_SHIFT)).astype(BF16)
    for t in range(TILES_PER_STEP):
        rows = slice(t * tm, (t + 1) * tm)
        xn = _rms(h_ref[rows, :], g_ref[...]).astype(BF16)
        g_lr = _dot(xn, wgl_ref[...]).astype(BF16)
        log_a = _log_sigmoid(_dot(g_lr, wg2_ref[...]) + bg_ref[...]) / GLA_GATE_NORM
        hi = log_a.astype(BF16)
        lo = (log_a - hi.astype(F32)).astype(BF16)
        v_ref[rows, :] = _dot(xn, w_ref[:, 2 * GLA_KD:2 * GLA_KD + GLA_VD]).astype(BF16)
        bcum = jnp.concatenate(
            [_dot(tri, hi[r0:r0 + tb, :]) + _dot(tri, lo[r0:r0 + tb, :]) for r0 in range(0, tm, tb)],
            axis=0)
        b_last = _dot(pick, hi) + _dot(pick, lo)
        dec_ref[t * nch:(t + 1) * nch, :] = jnp.exp(b_last)
        r = _dot(xn, w_ref[:, 2 * GLA_KD + GLA_VD:2 * GLA_KD + 2 * GLA_VD])
        rs_ref[rows, :] = (r * jax.nn.sigmoid(r)).astype(BF16)
        b_last_rows = jnp.broadcast_to(
            b_last[:, None, :], (nch, GLA_CHUNK, GLA_KD)).reshape(tm, GLA_KD)
        q = _dot(xn, w_ref[:, 0:GLA_KD])
        qd_ref[rows, :] = (q * GLA_DK ** -0.5 * jnp.exp(bcum)).astype(BF16)
        k = _dot(xn, w_ref[:, GLA_KD:2 * GLA_KD])
        ki_ref[rows, :] = (k * jnp.exp(-bcum)).astype(BF16)
        ku_ref[rows, :] = (k * jnp.exp(b_last_rows - bcum)).astype(BF16)


def _gla_core_kernel(qd_ref, ki_ref, ku_ref, v_ref, rs_ref, dec_ref, gn_ref, o_ref, *, seq):
    c = GLA_CHUNK
    nb = GLA_BLOCK_CHUNKS
    blk = nb * c
    row = lax.broadcasted_iota(jnp.int32, (blk, blk), 0)
    col = lax.broadcasted_iota(jnp.int32, (blk, blk), 1)
    causal = (((row >> GLA_CHUNK_SHIFT) == (col >> GLA_CHUNK_SHIFT))
              & (row >= col))
    gn = gn_ref[...]

    def block(rows, dec, states):
        new = []
        for hd in range(GLA_HEADS):
            kk = slice(hd * GLA_DK, (hd + 1) * GLA_DK)
            vv = slice(hd * GLA_DV, (hd + 1) * GLA_DV)
            qd, ki, ku, v = qd_ref[rows, kk], ki_ref[rows, kk], ku_ref[rows, kk], v_ref[rows, vv]
            att = jnp.where(causal, _dot_nt(qd, ki), 0.0)
            o = _dot(att.astype(BF16), v)
            st = states[hd]
            inter = []
            for j in range(nb):
                cr = slice(j * c, (j + 1) * c)
                inter.append(_dot_nt(qd[cr], st.astype(BF16)))
                st = st * dec[j:j + 1, kk] + _dot_tn(v[cr], ku[cr])
            o = o + jnp.concatenate(inter, axis=0)
            o = o * lax.rsqrt(jnp.mean(o * o, axis=-1, keepdims=True) + EPS)
            o_ref[rows, vv] = (o * gn[:, vv] * rs_ref[rows, vv].astype(F32)).astype(BF16)
            new.append(st)
        return tuple(new)

    def two_blocks(i, states):
        dec = dec_ref[pl.ds(pl.multiple_of(i * 2 * nb, 2 * nb), 2 * nb), :]
        for sub in range(2):
            rows = pl.ds(pl.multiple_of(i * 2 * blk + sub * blk, blk), blk)
            states = block(rows, dec[sub * nb:(sub + 1) * nb], states)
        return states

    lax.fori_loop(0, seq // (2 * blk), two_blocks,
                  (jnp.zeros((GLA_DV, GLA_DK), F32),) * GLA_HEADS)


def _gla_mixer(h, gm, layer, w_main, gla_layer, wgl, wg2, bg, gn, batch, seq):
    tm = ROW_TILE * TILES_PER_STEP
    m = batch * seq
    nch = tm // GLA_CHUNK
    row = lambda i: (i, 0)
    tok = lambda width, dtype: jax.ShapeDtypeStruct((m, width), dtype)
    qd, ki, ku, v, rs, dec = pl.pallas_call(
        _gla_in_kernel,
        grid=(m // tm,),
        in_specs=[pl.BlockSpec((tm, D_MODEL), row), _layer((1, D_MODEL), layer),
                  _layer(w_main.shape[1:], gla_layer), _resident(wgl.shape), _resident(wg2.shape),
                  _resident((1, GLA_KD))],
        out_specs=[pl.BlockSpec((tm, GLA_KD), row), pl.BlockSpec((tm, GLA_KD), row),
                   pl.BlockSpec((tm, GLA_KD), row), pl.BlockSpec((tm, GLA_VD), row),
                   pl.BlockSpec((tm, GLA_VD), row), pl.BlockSpec((nch, GLA_KD), row)],
        out_shape=[tok(GLA_KD, BF16), tok(GLA_KD, BF16), tok(GLA_KD, BF16), tok(GLA_VD, BF16),
                   tok(GLA_VD, BF16), jax.ShapeDtypeStruct((m // GLA_CHUNK, GLA_KD), F32)],
        compiler_params=_params("parallel"),
    )(h, gm, w_main, wgl, wg2, bg)
    blk = lambda rows, width: pl.BlockSpec((rows, width), lambda b: (b, 0))
    return pl.pallas_call(
        functools.partial(_gla_core_kernel, seq=seq),
        grid=(batch,),
        in_specs=[blk(seq, GLA_KD), blk(seq, GLA_KD), blk(seq, GLA_KD), blk(seq, GLA_VD),
                  blk(seq, GLA_VD), blk(seq // GLA_CHUNK, GLA_KD), _resident((1, GLA_VD))],
        out_specs=blk(seq, GLA_VD),
        out_shape=jax.ShapeDtypeStruct((m, GLA_VD), BF16),
        compiler_params=_params("parallel"),
    )(qd, ki, ku, v, rs, dec, gn)


def _pad_lanes(w, width):
    return jnp.pad(w, ((0, 0), (0, width - w.shape[1])))


def kernel(x, p, norm_mix, norm_mlp, norm_ple, s5_w_in, s5_lam_re, s5_lam_im, s5_log_dt, s5_b_re,
           s5_b_im, s5_c_re, s5_c_im, s5_d, s5_w_glu, s5_w_out, fox_w_in, fox_b_f, fox_w_out,
           gla_w_in, gla_w_g2, gla_b_g, gla_norm, gla_w_out, mlp_w1, mlp_w2, ple_proj, ple_gate,
           final_norm):
    batch, seq, _ = x.shape
    depth = p.shape[0]
    m = batch * seq
    h = x.reshape(m, D_MODEL)
    p_all = p.reshape(depth, m, PLE_DIM)
    gf = final_norm.reshape(1, D_MODEL)
    gm_mix = norm_mix.reshape(depth, 1, D_MODEL)
    gm_mlp = norm_mlp.reshape(depth, 1, D_MODEL)
    gm_ple = norm_ple.reshape(depth, 1, D_MODEL)
    w1, w2 = _to_bf16(mlp_w1), _to_bf16(mlp_w2)
    wpg, wpp = _to_bf16(ple_gate), _to_bf16(ple_proj)
    wo_s5, wo_fox, wo_gla = _to_bf16(s5_w_out), _to_bf16(fox_w_out), _to_bf16(gla_w_out)
    s5_in, s5_glu = _to_bf16(s5_w_in), _to_bf16(s5_w_glu)
    fox_qkv = _to_bf16(fox_w_in, cols=3 * D_MODEL)
    gla_main = _to_bf16(gla_w_in, cols=2 * GLA_KD + 2 * GLA_VD)
    s5_mats = _s5_prep(s5_lam_re, s5_lam_im, s5_log_dt, s5_b_re, s5_b_im, s5_c_re, s5_c_im, s5_d)
    for i in range(depth):
        mixer, j = i % 3, i // 3
        if mixer == 0:
            mix = _s5_mixer(h, gm_mix, i, s5_in, s5_mats, s5_glu, j, batch, seq)
            wo = wo_s5
        elif mixer == 1:
            w_f = fox_w_in[j, :, 3 * D_MODEL:]
            wf = _pad_lanes(jnp.tile(w_f, (1, FOX_BIAS_GROUPS)), LANES).astype(BF16)
            bf = _pad_lanes(jnp.tile(fox_b_f[j].reshape(1, FOX_HEADS), (1, FOX_BIAS_GROUPS)), LANES)
            mix = _fox_mixer(h, gm_mix, i, fox_qkv, j, wf, bf, batch, seq)
            wo = wo_fox
        else:
            wgl = _pad_lanes(gla_w_in[j, :, 2 * GLA_KD + 2 * GLA_VD:], LANES).astype(BF16)
            wg2 = jnp.pad(gla_w_g2[j], ((0, LANES - GLA_RANK), (0, 0))).astype(BF16)
            mix = _gla_mixer(h, gm_mix, i, gla_main, j, wgl, wg2, gla_b_g[j].reshape(1, GLA_KD),
                             gla_norm[j].reshape(1, GLA_VD), batch, seq)
            wo = wo_gla
        h = _tail(h, mix, wo, j, gm_mlp, w1, w2, gm_ple, wpg, p_all, wpp, gf, layer=i,
                  final=(i == depth - 1))
    return h.reshape(batch, seq, D_MODEL)
```

```python
import functools

import jax
import jax.numpy as jnp
from jax import lax
from jax.experimental import pallas as pl
from jax.experimental.pallas import tpu as pltpu

F32 = jnp.float32
BF16 = jnp.bfloat16

D_MODEL = 1024
D_FF = 4 * D_MODEL
PLE_DIM = 256
EPS = 1e-6
NEG_INF = -1e30
LOG2E = 1.4426950408889634

V7X_VMEM_BYTES = 64 * 1024 * 1024
VMEM_LIMIT_BYTES = V7X_VMEM_BYTES - 8 * 1024 * 1024
LANES = 128
V7X_MXU_DEPTH = 256

S5_GROUP = 16
S5_GROUPS = D_MODEL // S5_GROUP
S5_STATE = 64
S5_GROUP_SHIFT = S5_GROUP.bit_length() - 1
S5_L = 16
S5_K = S5_L * S5_GROUP
S5_PREP_GROUPS = 16

FOX_HEADS = 16
FOX_HEAD_DIM = D_MODEL // FOX_HEADS
FOX_HEADS_SHIFT = FOX_HEADS.bit_length() - 1
FOX_TQ = 512
FOX_BIAS_GROUPS = 6

GLA_HEADS = 4
GLA_KD = D_MODEL // 2
GLA_VD = D_MODEL
GLA_DK = GLA_KD // GLA_HEADS
GLA_DV = GLA_VD // GLA_HEADS
GLA_RANK = 16
GLA_GATE_NORM = 16.0
GLA_CHUNK = 64
GLA_CHUNK_SHIFT = GLA_CHUNK.bit_length() - 1
GLA_BLOCK_CHUNKS = 4

ROW_TILE = 512
TILES_PER_STEP = 2
TAIL_TILES_PER_STEP = 2
CAST_BLOCK_BYTES = 4 * 1024 * 1024


assert S5_GROUP == 1 << S5_GROUP_SHIFT and FOX_HEADS == 1 << FOX_HEADS_SHIFT
assert GLA_CHUNK == 1 << GLA_CHUNK_SHIFT


def _params(*sem):
    return pltpu.CompilerParams(dimension_semantics=sem, vmem_limit_bytes=VMEM_LIMIT_BYTES)


def _resident(shape):
    zeros = (0,) * len(shape)
    return pl.BlockSpec(shape, lambda *_: zeros, pipeline_mode=pl.Buffered(1))


def _layer(shape, layer):
    index = (layer,) + (0,) * len(shape)
    return pl.BlockSpec((None,) + tuple(shape), lambda *_: index, pipeline_mode=pl.Buffered(1))


def _cast_kernel(x_ref, o_ref):
    o_ref[...] = x_ref[:, 0:o_ref.shape[1]].astype(BF16)


def _to_bf16(w, cols=None):
    layers, k, n = w.shape
    cols = n if cols is None else cols
    tk = 8
    while tk * 2 <= k and tk * 2 * n * 4 <= CAST_BLOCK_BYTES:
        tk *= 2
    assert k % tk == 0
    return pl.pallas_call(
        _cast_kernel, grid=(layers, k // tk),
        in_specs=[pl.BlockSpec((None, tk, n), lambda l, i: (l, i, 0))],
        out_specs=pl.BlockSpec((None, tk, cols), lambda l, i: (l, i, 0)),
        out_shape=jax.ShapeDtypeStruct((layers, k, cols), BF16),
        compiler_params=_params("parallel", "parallel"),
    )(w)


def _rms(x, g):
    ms = jnp.mean(x * x, axis=-1, keepdims=True)
    return x * lax.rsqrt(ms + EPS) * g


def _log_sigmoid(x):
    return jnp.minimum(x, 0.0) - jnp.log1p(jnp.exp(-jnp.abs(x)))


def _dot(a, b):
    return jnp.dot(a, b, preferred_element_type=F32)


def _dot_nt(a, b):
    return lax.dot_general(a, b, (((1,), (1,)), ((), ())), preferred_element_type=F32)


def _dot_tn(a, b):
    return lax.dot_general(a, b, (((0,), (0,)), ((), ())), preferred_element_type=F32)


def _split3(x):
    hi = x.astype(BF16)
    r1 = x - hi.astype(F32)
    mid = r1.astype(BF16)
    lo = (r1 - mid.astype(F32)).astype(BF16)
    return hi, mid, lo


def _tail_kernel(h_ref, mix_ref, wo_ref, gm_ref, w1_ref, w2_ref, gp_ref, wpg_ref, p_ref,
                 wpp_ref, gf_ref, o_ref, *, final):
    ff_chunk = D_FF // 4
    for t in range(TAIL_TILES_PER_STEP):
        rows = slice(t * ROW_TILE, (t + 1) * ROW_TILE)
        h = h_ref[rows, :] + _dot(mix_ref[rows, :], wo_ref[...])
        xn = _rms(h, gm_ref[...]).astype(BF16)
        mlp = None
        for c in range(4):
            a = _dot(xn, w1_ref[:, c * ff_chunk:(c + 1) * ff_chunk])
            a = jnp.square(jnp.maximum(a, 0.0)).astype(BF16)
            part = _dot(a, w2_ref[c * ff_chunk:(c + 1) * ff_chunk, :])
            mlp = part if mlp is None else mlp + part
        h = h + mlp
        xg = _rms(h, gp_ref[...]).astype(BF16)
        gate = jax.nn.sigmoid(_dot(xg, wpg_ref[...]))
        pe = _dot(p_ref[rows, :].astype(BF16), wpp_ref[...])
        h = h + pe * gate
        if final:
            h = _rms(h, gf_ref[...])
        o_ref[rows, :] = h


def _tail(h, mix, wo, wo_layer, gm, w1, w2, gp, wpg, p_all, wpp, gf, layer, final):
    m = h.shape[0]
    tm = ROW_TILE * TAIL_TILES_PER_STEP
    row = lambda i: (i, 0)
    return pl.pallas_call(
        functools.partial(_tail_kernel, final=final),
        grid=(m // tm,),
        in_specs=[
            pl.BlockSpec((tm, D_MODEL), row),
            pl.BlockSpec((tm, mix.shape[1]), row),
            _layer(wo.shape[1:], wo_layer),
            _layer((1, D_MODEL), layer),
            _layer(w1.shape[1:], layer),
            _layer(w2.shape[1:], layer),
            _layer((1, D_MODEL), layer),
            _layer(wpg.shape[1:], layer),
            pl.BlockSpec((None, tm, PLE_DIM), lambda i: (layer, i, 0)),
            _layer(wpp.shape[1:], layer),
            _resident((1, D_MODEL)),
        ],
        out_specs=pl.BlockSpec((tm, D_MODEL), row),
        out_shape=jax.ShapeDtypeStruct((m, D_MODEL), F32),
        compiler_params=_params("parallel"),
    )(h, mix, wo, gm, w1, w2, gp, wpg, p_all, wpp, gf)


def _cmul(ar, ai, br, bi):
    return ar * br - ai * bi, ar * bi + ai * br


def _cpow_table(ar, ai, count):
    pr, pi = jnp.ones_like(ar), jnp.zeros_like(ar)
    table = [(pr, pi)]
    for _ in range(count):
        pr, pi = _cmul(pr, pi, ar, ai)
        table.append((pr, pi))
    return table


def _s5_discretize(lr, li, ld):
    dt = jnp.exp(ld)
    mag = jnp.exp(lr * dt)
    ar = mag * jnp.cos(li * dt)
    ai = mag * jnp.sin(li * dt)
    den = lr * lr + li * li
    nr = ar - 1.0
    coef_re = (nr * lr + ai * li) / den
    coef_im = (ai * lr - nr * li) / den
    return ar, ai, coef_re, coef_im


def _s5_prep_group(col_pow, cfr, cfi, row_pow, b_re, b_im, cr, ci, d):
    n, k, c = S5_STATE, S5_K, S5_GROUP
    hp = lax.Precision.HIGHEST
    rep = (lax.broadcasted_iota(jnp.int32, (c, k), 0)
           == (lax.broadcasted_iota(jnp.int32, (c, k), 1) & (c - 1))).astype(F32)
    brt = jnp.dot(b_re, rep, precision=hp, preferred_element_type=F32)
    bit = jnp.dot(b_im, rep, precision=hp, preferred_element_type=F32)
    bbr = cfr * brt - cfi * bit
    bbi = cfr * bit + cfi * brt
    grp = lax.broadcasted_iota(jnp.int32, (n, k), 1) >> S5_GROUP_SHIFT
    pr = jnp.broadcast_to(col_pow[0][0], (n, k))
    pi = jnp.broadcast_to(col_pow[0][1], (n, k))
    for i in range(S5_L - 1):
        pr = jnp.where(grp == i, col_pow[S5_L - 1 - i][0], pr)
        pi = jnp.where(grp == i, col_pow[S5_L - 1 - i][1], pi)
    er, ei = _cmul(pr, pi, bbr, bbi)
    krev = (jnp.dot(cr, er, precision=hp, preferred_element_type=F32)
            - jnp.dot(ci, ei, precision=hp, preferred_element_type=F32))
    row = lax.broadcasted_iota(jnp.int32, (c, k), 0)
    col = lax.broadcasted_iota(jnp.int32, (c, k), 1)
    krev = krev + jnp.where(col == row + (k - c), d, 0.0)
    m_rows = []
    for t in range(S5_L):
        width = c * (t + 1)
        shifted = krev if width == k else pltpu.roll(krev, width, axis=1)
        m_rows.append(jnp.where(col < width, shifted, 0.0))
    f_re = [cr * row_pow[t + 1][0] - ci * row_pow[t + 1][1] for t in range(S5_L)]
    f_im = [-(cr * row_pow[t + 1][1] + ci * row_pow[t + 1][0]) for t in range(S5_L)]
    return m_rows, er, ei, f_re, f_im


def _s5_prep_kernel(lrc_ref, lic_ref, ldc_ref, lrr_ref, lir_ref, ldr_ref, br_ref, bi_ref, cr_ref,
                    ci_ref, d_ref, m_ref, e_ref, f_ref, al_ref):
    n, c = S5_STATE, S5_GROUP
    ar, ai, cfr, cfi = _s5_discretize(lrc_ref[...], lic_ref[...], ldc_ref[...])
    col_pow = _cpow_table(ar, ai, S5_L - 1)
    ar, ai, _, _ = _s5_discretize(lrr_ref[...], lir_ref[...], ldr_ref[...])
    row_pow = _cpow_table(ar, ai, S5_L)
    al_ref[:, 0, :] = row_pow[S5_L][0]
    al_ref[:, 1, :] = row_pow[S5_L][1]
    for gi in range(S5_PREP_GROUPS):
        colg = [(pr[:, gi:gi + 1], pi[:, gi:gi + 1]) for pr, pi in col_pow]
        rowg = [(pr[gi:gi + 1, :], pi[gi:gi + 1, :]) for pr, pi in row_pow]
        m_rows, er, ei, f_re, f_im = _s5_prep_group(
            colg, cfr[:, gi:gi + 1], cfi[:, gi:gi + 1], rowg, br_ref[gi], bi_ref[gi], cr_ref[gi],
            ci_ref[gi], d_ref[gi])
        e_ref[gi, 0:n, :] = er.astype(BF16)
        e_ref[gi, n:2 * n, :] = ei.astype(BF16)
        for t in range(S5_L):
            m_ref[gi, c * t:c * (t + 1), :] = m_rows[t].astype(BF16)
            f_ref[gi, c * t:c * (t + 1), 0:n] = f_re[t].astype(BF16)
            f_ref[gi, c * t:c * (t + 1), n:2 * n] = f_im[t].astype(BF16)


def _s5_prep(lam_re, lam_im, log_dt, b_re, b_im, c_re, c_im, d_skip):
    n, k, c = S5_STATE, S5_K, S5_GROUP
    g = lam_re.shape[0] * S5_GROUPS
    gb = S5_PREP_GROUPS
    steps = g // gb
    blk = lambda *shape: pl.BlockSpec((gb,) + shape, lambda i: (i,) + (0,) * len(shape))
    one = lambda *shape: pl.BlockSpec((None,) + shape, lambda i: (i,) + (0,) * len(shape))
    lr, li = lam_re.reshape(steps, gb, n), lam_im.reshape(steps, gb, n)
    m, e, f, al = pl.pallas_call(
        _s5_prep_kernel,
        grid=(steps,),
        in_specs=[one(n, gb), one(n, gb), one(1, gb), one(gb, n), one(gb, n), one(gb, 1),
                  blk(n, c), blk(n, c), blk(c, n), blk(c, n), blk(c, 1)],
        out_specs=[blk(k, k), blk(2 * n, k), blk(k, 2 * n), blk(2, n)],
        out_shape=[jax.ShapeDtypeStruct((g, k, k), BF16),
                   jax.ShapeDtypeStruct((g, 2 * n, k), BF16),
                   jax.ShapeDtypeStruct((g, k, 2 * n), BF16),
                   jax.ShapeDtypeStruct((g, 2, n), F32)],
        compiler_params=_params("parallel"),
    )(lr.transpose(0, 2, 1), li.transpose(0, 2, 1), log_dt.reshape(steps, 1, gb), lr, li,
      log_dt.reshape(steps, gb, 1),
      b_re.reshape(g, n, c), b_im.reshape(g, n, c), c_re.reshape(g, c, n), c_im.reshape(g, c, n),
      d_skip.reshape(g, c, 1))
    al = al.reshape(g // 2, 2, 2, n).transpose(0, 2, 1, 3).reshape(g // 2, 2, 1, 2 * n)
    return m, e, f, al


def _s5_in_kernel(h_ref, g_ref, w_ref, ut_ref, u_s, *, seq):
    for r in range(seq // ROW_TILE):
        rows = slice(r * ROW_TILE, (r + 1) * ROW_TILE)
        xn = _rms(h_ref[rows, :], g_ref[...]).astype(BF16)
        u = _dot(xn, w_ref[...])
        for lt in range(D_MODEL // LANES):
            u_s[lt, rows, :] = u[:, lt * LANES:(lt + 1) * LANES]
    nck = seq // S5_L
    gpl = LANES // S5_GROUP
    for j in range(S5_L):
        for lt in range(D_MODEL // LANES):
            xj = u_s[lt, pl.ds(j, nck, stride=S5_L), :]
            ut_ref[gpl * lt:gpl * (lt + 1), S5_GROUP * j:S5_GROUP * (j + 1), :] = (
                xj.T.reshape(gpl, S5_GROUP, nck).astype(BF16))


def _s5_core_kernel(ut_ref, m_ref, e_ref, f_ref, al_ref, yt_ref, sr_s, si_s, *, nck, batch):
    n = S5_STATE
    u = [ut_ref[g] for g in range(2)]
    s = [_dot(e_ref[g], u[g]) for g in range(2)]
    s_re = jnp.concatenate([s[0][0:n, :], s[1][0:n, :]], axis=0)
    s_im = jnp.concatenate([s[0][n:2 * n, :], s[1][n:2 * n, :]], axis=0)
    for b in range(batch):
        cols = slice(b * nck, (b + 1) * nck)
        sr_s[pl.ds(b, nck, stride=batch), :] = s_re[:, cols].T
        si_s[pl.ds(b, nck, stride=batch), :] = s_im[:, cols].T
    ar, ai = al_ref[0], al_ref[1]

    y_intra = [_dot(m_ref[g], u[g]) for g in range(2)]
    pr = pi = jnp.zeros((batch, 2 * n), F32)
    for k in range(nck):
        rows = slice(k * batch, (k + 1) * batch)
        cr, ci = sr_s[rows, :], si_s[rows, :]
        sr_s[rows, :] = pr
        si_s[rows, :] = pi
        pr, pi = ar * pr - ai * pi + cr, ar * pi + ai * pr + ci
    p_re = jnp.concatenate([sr_s[pl.ds(b, nck, stride=batch), :].T for b in range(batch)], axis=1)
    p_im = jnp.concatenate([si_s[pl.ds(b, nck, stride=batch), :].T for b in range(batch)], axis=1)
    for g in range(2):
        sp = jnp.concatenate([p_re[g * n:(g + 1) * n, :], p_im[g * n:(g + 1) * n, :]], axis=0)
        yt_ref[g] = (y_intra[g] + _dot(f_ref[g], sp.astype(BF16))).astype(BF16)


def _s5_post_kernel(yt_ref, wg_ref, mix_ref, y_s, *, seq):
    nck = seq // S5_L
    gpl = LANES // S5_GROUP
    nlt = D_MODEL // LANES
    for t in range(S5_L):
        for lt in range(nlt):
            piece = yt_ref[gpl * lt:gpl * (lt + 1), S5_GROUP * t:S5_GROUP * (t + 1), :]
            y_s[lt, pl.ds(t, nck, stride=S5_L), :] = piece.reshape(LANES, nck).T.astype(F32)
    for r in range(seq // ROW_TILE):
        rows = slice(r * ROW_TILE, (r + 1) * ROW_TILE)
        y = jnp.concatenate([y_s[lt, rows, :] for lt in range(nlt)], axis=1)
        a = jax.nn.gelu(y).astype(BF16)
        gate = jax.nn.sigmoid(_dot(a, wg_ref[:, D_MODEL:2 * D_MODEL]))
        mix_ref[rows, :] = (_dot(a, wg_ref[:, 0:D_MODEL]) * gate).astype(BF16)


def _s5_mixer(h, gm, layer, w_in, mats, w_glu, s5_layer, batch, seq):
    m_t, e_t, f_t, al = mats
    nck = seq // S5_L
    lanes = batch * nck
    assert nck % LANES == 0 and nck & (nck - 1) == 0
    ut = pl.pallas_call(
        functools.partial(_s5_in_kernel, seq=seq),
        grid=(batch,),
        in_specs=[pl.BlockSpec((seq, D_MODEL), lambda b: (b, 0)),
                  _layer((1, D_MODEL), layer), _layer(w_in.shape[1:], s5_layer)],
        out_specs=pl.BlockSpec((S5_GROUPS, S5_K, nck), lambda b: (0, 0, b)),
        out_shape=jax.ShapeDtypeStruct((S5_GROUPS, S5_K, lanes), BF16),
        scratch_shapes=[pltpu.VMEM((D_MODEL // LANES, seq, LANES), F32)],
        compiler_params=_params("parallel"),
    )(h, gm, w_in)
    p0 = s5_layer * (S5_GROUPS // 2)
    act = lambda *shape: pl.BlockSpec((2,) + shape, lambda g: (g,) + (0,) * len(shape))
    par = lambda *shape: pl.BlockSpec((2,) + shape, lambda g: (p0 + g,) + (0,) * len(shape))
    yt = pl.pallas_call(
        functools.partial(_s5_core_kernel, nck=nck, batch=batch),
        grid=(S5_GROUPS // 2,),
        in_specs=[act(S5_K, lanes), par(S5_K, S5_K), par(2 * S5_STATE, S5_K),
                  par(S5_K, 2 * S5_STATE),
                  pl.BlockSpec((None, 2, 1, 2 * S5_STATE), lambda g: (p0 + g, 0, 0, 0))],
        out_specs=act(S5_K, lanes),
        out_shape=jax.ShapeDtypeStruct((S5_GROUPS, S5_K, lanes), BF16),
        scratch_shapes=[pltpu.VMEM((lanes, 2 * S5_STATE), F32),
                        pltpu.VMEM((lanes, 2 * S5_STATE), F32)],
        compiler_params=_params("parallel"),
    )(ut, m_t, e_t, f_t, al)
    return pl.pallas_call(
        functools.partial(_s5_post_kernel, seq=seq),
        grid=(batch,),
        in_specs=[pl.BlockSpec((S5_GROUPS, S5_K, nck), lambda b: (0, 0, b)),
                  _layer(w_glu.shape[1:], s5_layer)],
        out_specs=pl.BlockSpec((seq, D_MODEL), lambda b: (b, 0)),
        out_shape=jax.ShapeDtypeStruct((batch * seq, D_MODEL), BF16),
        scratch_shapes=[pltpu.VMEM((D_MODEL // LANES, seq, LANES), F32)],
        compiler_params=_params("parallel"),
    )(yt, w_glu)


def _fox_in_kernel(h_ref, g_ref, wqkv_ref, wf_ref, bf_ref, q_ref, k_ref, v_ref, qc_ref, kc_ref,
                   carry_s):
    tm = ROW_TILE

    @pl.when(pl.program_id(1) == 0)
    def _():
        carry_s[...] = jnp.zeros_like(carry_s)

    row = lax.broadcasted_iota(jnp.int32, (tm, tm), 0)
    col = lax.broadcasted_iota(jnp.int32, (tm, tm), 1)
    tri = (row >= col).astype(BF16)
    lane = lax.broadcasted_iota(jnp.int32, (tm, LANES), 1)
    grp = lane >> FOX_HEADS_SHIFT
    carry = carry_s[...]
    for t in range(TILES_PER_STEP):
        rows = slice(t * tm, (t + 1) * tm)
        xn = _rms(h_ref[rows, :], g_ref[...]).astype(BF16)
        log_f = _log_sigmoid(_dot(xn, wf_ref[...]) + bf_ref[...])
        hi, mid, lo = _split3(log_f)
        q = _dot(xn, wqkv_ref[:, 0:D_MODEL])
        q_ref[rows, :] = (q * (FOX_HEAD_DIM ** -0.5 * LOG2E)).astype(BF16)
        cum = _dot(tri, hi) + _dot(tri, mid) + _dot(tri, lo) + carry
        carry = cum[tm - 1:tm, :]
        k_ref[rows, :] = _dot(xn, wqkv_ref[:, D_MODEL:2 * D_MODEL]).astype(BF16)
        hi, mid, lo = _split3(cum * LOG2E)
        hi, mid, lo = hi.astype(F32), mid.astype(F32), lo.astype(F32)
        qc = jnp.where(grp == 0, hi, jnp.where(grp == 1, mid, jnp.where(grp == 2, lo,
             jnp.where(grp < FOX_BIAS_GROUPS, 1.0, 0.0))))
        kc = jnp.where(grp < 3, 1.0, jnp.where(grp == 3, -hi, jnp.where(grp == 4, -mid,
             jnp.where(grp == 5, -lo, 0.0))))
        qc_ref[rows, :] = qc.astype(BF16)
        kc_ref[rows, :] = kc.astype(BF16)
        v = _dot(xn, wqkv_ref[:, 2 * D_MODEL:3 * D_MODEL])
        for pair in range(FOX_HEADS // 2):
            vp = v[:, pair * LANES:(pair + 1) * LANES]
            v_ref[2 * pair, rows, :] = jnp.where(
                lane < FOX_HEAD_DIM, vp, jnp.where(lane == FOX_HEAD_DIM, 1.0, 0.0)).astype(BF16)
            v_ref[2 * pair + 1, rows, :] = jnp.where(
                lane >= FOX_HEAD_DIM, vp, jnp.where(lane == 0, 1.0, 0.0)).astype(BF16)
    carry_s[...] = carry


def _fox_flash_kernel(q_ref, qc_ref, k_ref, kc_ref, v_ref, o_ref, *, seq):
    t = FOX_TQ
    pair = pl.program_id(1)
    lane = lax.broadcasted_iota(jnp.int32, (t, LANES), 1)
    row = lax.broadcasted_iota(jnp.int32, (t, t), 0)
    col = lax.broadcasted_iota(jnp.int32, (t, t), 1)
    causal = row >= col
    own = (lane < FOX_HEAD_DIM, lane >= FOX_HEAD_DIM)
    mine = [((lane & (FOX_HEADS - 1)) == 2 * pair + half) & (lane < FOX_HEADS * FOX_BIAS_GROUPS)
            for half in range(2)]
    causal2 = jnp.concatenate([causal, causal], axis=0)
    for i in range(seq // t):
        qrows = slice(i * t, (i + 1) * t)
        q = q_ref[qrows, :].astype(F32)
        qc = qc_ref[qrows, :].astype(F32)
        qcat = jnp.concatenate(
            [jnp.concatenate([jnp.where(own[half], q, 0.0), jnp.where(mine[half], qc, 0.0)], axis=1)
             for half in range(2)], axis=0).astype(BF16)
        m = jnp.full((2 * t, 1), NEG_INF, F32)
        acc = jnp.zeros((2 * t, LANES), F32)
        for j in range(i + 1):
            krows = slice(j * t, (j + 1) * t)
            kcat = jnp.concatenate([k_ref[krows, :], kc_ref[krows, :]], axis=1)
            vcat = jnp.concatenate([v_ref[0, krows, :], v_ref[1, krows, :]], axis=1)
            s = _dot_nt(qcat, kcat)
            if j == i:
                s = jnp.where(causal2, s, NEG_INF)
            m_new = jnp.maximum(m, jnp.max(s, axis=-1, keepdims=True))
            p = jnp.exp2(s - m_new)
            pv = _dot(p.astype(BF16), vcat)
            pv = jnp.concatenate([pv[0:t, 0:LANES], pv[t:2 * t, LANES:2 * LANES]], axis=0)
            acc = jnp.exp2(m - m_new) * acc + pv
            m = m_new
        out_a = acc[0:t] / acc[0:t, FOX_HEAD_DIM:FOX_HEAD_DIM + 1]
        out_b = acc[t:2 * t] / acc[t:2 * t, 0:1]
        o_ref[qrows, :] = jnp.where(lane < FOX_HEAD_DIM, out_a, out_b).astype(BF16)


def _fox_mixer(h, gm, layer, wqkv, fox_layer, wf, bf, batch, seq):
    tm = ROW_TILE * TILES_PER_STEP
    nt = seq // tm
    m = batch * seq
    tok = lambda b, i: (b * nt + i, 0)
    q, k, v, qc, kc = pl.pallas_call(
        _fox_in_kernel,
        grid=(batch, nt),
        in_specs=[pl.BlockSpec((tm, D_MODEL), tok),
                  _layer((1, D_MODEL), layer), _layer(wqkv.shape[1:], fox_layer),
                  _resident(wf.shape), _resident((1, LANES))],
        out_specs=[pl.BlockSpec((tm, D_MODEL), tok), pl.BlockSpec((tm, D_MODEL), tok),
                   pl.BlockSpec((None, FOX_HEADS, tm, LANES), lambda b, i: (b, 0, i, 0)),
                   pl.BlockSpec((tm, LANES), tok), pl.BlockSpec((tm, LANES), tok)],
        out_shape=[jax.ShapeDtypeStruct((m, D_MODEL), BF16),
                   jax.ShapeDtypeStruct((m, D_MODEL), BF16),
                   jax.ShapeDtypeStruct((batch, FOX_HEADS, seq, LANES), BF16),
                   jax.ShapeDtypeStruct((m, LANES), BF16),
                   jax.ShapeDtypeStruct((m, LANES), BF16)],
        scratch_shapes=[pltpu.VMEM((1, LANES), F32)],
        compiler_params=_params("parallel", "arbitrary"),
    )(h, gm, wqkv, wf, bf)
    slab = pl.BlockSpec((seq, LANES), lambda b, p: (b, p))
    bias = pl.BlockSpec((seq, LANES), lambda b, p: (b, 0))
    return pl.pallas_call(
        functools.partial(_fox_flash_kernel, seq=seq),
        grid=(batch, FOX_HEADS // 2),
        in_specs=[slab, bias, slab, bias,
                  pl.BlockSpec((None, 2, seq, LANES), lambda b, p: (b, p, 0, 0))],
        out_specs=slab,
        out_shape=jax.ShapeDtypeStruct((m, D_MODEL), BF16),
        compiler_params=_params("parallel", "parallel"),
    )(q, qc, k, kc, v)


def _gla_in_kernel(h_ref, g_ref, w_ref, wgl_ref, wg2_ref, bg_ref, qd_ref, ki_ref, ku_ref, v_ref,
                   rs_ref, dec_ref):
    tm = ROW_TILE
    nch = tm // GLA_CHUNK
    tb = V7X_MXU_DEPTH
    row = lax.broadcasted_iota(jnp.int32, (tb, tb), 0)
    col = lax.broadcasted_iota(jnp.int32, (tb, tb), 1)
    tri = (((row >> GLA_CHUNK_SHIFT) == (col >> GLA_CHUNK_SHIFT))
           & (row >= col)).astype(BF16)
    crow = lax.broadcasted_iota(jnp.int32, (nch, tm), 0)
    ccol = lax.broadcasted_iota(jnp.int32, (nch, tm), 1)
    pick = (crow == (ccol >> GLA_CHUNK_SHIFT)).astype(BF16)
    for t in range(TILES_PER_STEP):
        rows = slice(t * tm, (t + 1) * tm)
        xn = _rms(h_ref[rows, :], g_ref[...]).astype(BF16)
        g_lr = _dot(xn, wgl_ref[...]).astype(BF16)
        log_a = _log_sigmoid(_dot(g_lr, wg2_ref[...]) + bg_ref[...]) / GLA_GATE_NORM
        hi = log_a.astype(BF16)
        lo = (log_a - hi.astype(F32)).astype(BF16)
        v_ref[rows, :] = _dot(xn, w_ref[:, 2 * GLA_KD:2 * GLA_KD + GLA_VD]).astype(BF16)
        bcum = jnp.concatenate(
            [_dot(tri, hi[r0:r0 + tb, :]) + _dot(tri, lo[r0:r0 + tb, :]) for r0 in range(0, tm, tb)],
            axis=0)
        b_last = _dot(pick, hi) + _dot(pick, lo)
        dec_ref[t * nch:(t + 1) * nch, :] = jnp.exp(b_last)
        for c0 in range(0, GLA_VD, GLA_KD):
            r = _dot(xn, w_ref[:, 2 * GLA_KD + GLA_VD + c0:2 * GLA_KD + GLA_VD + c0 + GLA_KD])
            rs_ref[rows, c0:c0 + GLA_KD] = (r * jax.nn.sigmoid(r)).astype(BF16)
        b_last_rows = jnp.broadcast_to(
            b_last[:, None, :], (nch, GLA_CHUNK, GLA_KD)).reshape(tm, GLA_KD)
        q = _dot(xn, w_ref[:, 0:GLA_KD])
        qd_ref[rows, :] = (q * GLA_DK ** -0.5 * jnp.exp(bcum)).astype(BF16)
        k = _dot(xn, w_ref[:, GLA_KD:2 * GLA_KD])
        ki_ref[rows, :] = (k * jnp.exp(-bcum)).astype(BF16)
        ku_ref[rows, :] = (k * jnp.exp(b_last_rows - bcum)).astype(BF16)


def _gla_core_kernel(qd_ref, ki_ref, ku_ref, v_ref, rs_ref, dec_ref, gn_ref, o_ref, *, seq):
    c = GLA_CHUNK
    nb = GLA_BLOCK_CHUNKS
    blk = nb * c
    row = lax.broadcasted_iota(jnp.int32, (blk, blk), 0)
    col = lax.broadcasted_iota(jnp.int32, (blk, blk), 1)
    causal = (((row >> GLA_CHUNK_SHIFT) == (col >> GLA_CHUNK_SHIFT))
              & (row >= col))
    gn = gn_ref[...]

    def block(rows, dec, states):
        new = []
        for hd in range(GLA_HEADS):
            kk = slice(hd * GLA_DK, (hd + 1) * GLA_DK)
            vv = slice(hd * GLA_DV, (hd + 1) * GLA_DV)
            qd, ki, ku, v = qd_ref[rows, kk], ki_ref[rows, kk], ku_ref[rows, kk], v_ref[rows, vv]
            att = jnp.where(causal, _dot_nt(qd, ki), 0.0)
            o = _dot(att.astype(BF16), v)
            st = states[hd]
            inter = []
            for j in range(nb):
                cr = slice(j * c, (j + 1) * c)
                inter.append(_dot_nt(qd[cr], st.astype(BF16)))
                st = st * dec[j:j + 1, kk] + _dot_tn(v[cr], ku[cr])
            o = o + jnp.concatenate(inter, axis=0)
            o = o * lax.rsqrt(jnp.mean(o * o, axis=-1, keepdims=True) + EPS)
            o_ref[rows, vv] = (o * gn[:, vv] * rs_ref[rows, vv].astype(F32)).astype(BF16)
            new.append(st)
        return tuple(new)

    def two_blocks(i, states):
        dec = dec_ref[pl.ds(pl.multiple_of(i * 2 * nb, 2 * nb), 2 * nb), :]
        for sub in range(2):
            rows = pl.ds(pl.multiple_of(i * 2 * blk + sub * blk, blk), blk)
            states = block(rows, dec[sub * nb:(sub + 1) * nb], states)
        return states

    lax.fori_loop(0, seq // (2 * blk), two_blocks,
                  (jnp.zeros((GLA_DV, GLA_DK), F32),) * GLA_HEADS)


def _gla_mixer(h, gm, layer, w_main, gla_layer, wgl, wg2, bg, gn, batch, seq):
    tm = ROW_TILE * TILES_PER_STEP
    m = batch * seq
    nch = tm // GLA_CHUNK
    row = lambda i: (i, 0)
    tok = lambda width, dtype: jax.ShapeDtypeStruct((m, width), dtype)
    qd, ki, ku, v, rs, dec = pl.pallas_call(
        _gla_in_kernel,
        grid=(m // tm,),
        in_specs=[pl.BlockSpec((tm, D_MODEL), row), _layer((1, D_MODEL), layer),
                  _layer(w_main.shape[1:], gla_layer), _resident(wgl.shape), _resident(wg2.shape),
                  _resident((1, GLA_KD))],
        out_specs=[pl.BlockSpec((tm, GLA_KD), row), pl.BlockSpec((tm, GLA_KD), row),
                   pl.BlockSpec((tm, GLA_KD), row), pl.BlockSpec((tm, GLA_VD), row),
                   pl.BlockSpec((tm, GLA_VD), row), pl.BlockSpec((nch, GLA_KD), row)],
        out_shape=[tok(GLA_KD, BF16), tok(GLA_KD, BF16), tok(GLA_KD, BF16), tok(GLA_VD, BF16),
                   tok(GLA_VD, BF16), jax.ShapeDtypeStruct((m // GLA_CHUNK, GLA_KD), F32)],
        compiler_params=_params("parallel"),
    )(h, gm, w_main, wgl, wg2, bg)
    blk = lambda rows, width: pl.BlockSpec((rows, width), lambda b: (b, 0))
    return pl.pallas_call(
        functools.partial(_gla_core_kernel, seq=seq),
        grid=(batch,),
        in_specs=[blk(seq, GLA_KD), blk(seq, GLA_KD), blk(seq, GLA_KD), blk(seq, GLA_VD),
                  blk(seq, GLA_VD), blk(seq // GLA_CHUNK, GLA_KD), _resident((1, GLA_VD))],
        out_specs=blk(seq, GLA_VD),
        out_shape=jax.ShapeDtypeStruct((m, GLA_VD), BF16),
        compiler_params=_params("parallel"),
    )(qd, ki, ku, v, rs, dec, gn)


def _pad_lanes(w, width):
    return jnp.pad(w, ((0, 0), (0, width - w.shape[1])))


def kernel(x, p, norm_mix, norm_mlp, norm_ple, s5_w_in, s5_lam_re, s5_lam_im, s5_log_dt, s5_b_re,
           s5_b_im, s5_c_re, s5_c_im, s5_d, s5_w_glu, s5_w_out, fox_w_in, fox_b_f, fox_w_out,
           gla_w_in, gla_w_g2, gla_b_g, gla_norm, gla_w_out, mlp_w1, mlp_w2, ple_proj, ple_gate,
           final_norm):
    batch, seq, _ = x.shape
    depth = p.shape[0]
    m = batch * seq
    h = x.reshape(m, D_MODEL)
    p_all = p.reshape(depth, m, PLE_DIM)
    gf = final_norm.reshape(1, D_MODEL)
    gm_mix = norm_mix.reshape(depth, 1, D_MODEL)
    gm_mlp = norm_mlp.reshape(depth, 1, D_MODEL)
    gm_ple = norm_ple.reshape(depth, 1, D_MODEL)
    w1, w2 = _to_bf16(mlp_w1), _to_bf16(mlp_w2)
    wpg, wpp = _to_bf16(ple_gate), _to_bf16(ple_proj)
    wo_s5, wo_fox, wo_gla = _to_bf16(s5_w_out), _to_bf16(fox_w_out), _to_bf16(gla_w_out)
    s5_in, s5_glu = _to_bf16(s5_w_in), _to_bf16(s5_w_glu)
    fox_qkv = _to_bf16(fox_w_in, cols=3 * D_MODEL)
    gla_main = _to_bf16(gla_w_in, cols=2 * GLA_KD + 2 * GLA_VD)
    s5_mats = _s5_prep(s5_lam_re, s5_lam_im, s5_log_dt, s5_b_re, s5_b_im, s5_c_re, s5_c_im, s5_d)
    for i in range(depth):
        mixer, j = i % 3, i // 3
        if mixer == 0:
            mix = _s5_mixer(h, gm_mix, i, s5_in, s5_mats, s5_glu, j, batch, seq)
            wo = wo_s5
        elif mixer == 1:
            w_f = fox_w_in[j, :, 3 * D_MODEL:]
            wf = _pad_lanes(jnp.tile(w_f, (1, FOX_BIAS_GROUPS)), LANES).astype(BF16)
            bf = _pad_lanes(jnp.tile(fox_b_f[j].reshape(1, FOX_HEADS), (1, FOX_BIAS_GROUPS)), LANES)
            mix = _fox_mixer(h, gm_mix, i, fox_qkv, j, wf, bf, batch, seq)
            wo = wo_fox
        else:
            wgl = _pad_lanes(gla_w_in[j, :, 2 * GLA_KD + 2 * GLA_VD:], LANES).astype(BF16)
            wg2 = jnp.pad(gla_w_g2[j], ((0, LANES - GLA_RANK), (0, 0))).astype(BF16)
            mix = _gla_mixer(h, gm_mix, i, gla_main, j, wgl, wg2, gla_b_g[j].reshape(1, GLA_KD),
                             gla_norm[j].reshape(1, GLA_VD), batch, seq)
            wo = wo_gla
        h = _tail(h, mix, wo, j, gm_mlp, w1, w2, gm_ple, wpg, p_all, wpp, gf, layer=i,
                  final=(i == depth - 1))
    return h.reshape(batch, seq, D_MODEL)
```
